```python
import math
import jax, jax.numpy as jnp
from jax import lax
import numpy as np

D_MODEL = 1024
BATCH = 16
SEQ = 2048
DEPTH = 1

MEM_LEN = 256

SSM_WIDTH = 512
SSM_GROUP = 16
SSM_GROUPS = SSM_WIDTH // SSM_GROUP
SSM_STATE = 64
SSM_DT_MIN = 1e-3
SSM_DT_MAX = 1e-1
SSM_EIG_CLIP = -1e-4

N_HEADS = 16
N_KV_HEADS = 4
HEAD_DIM = 64
GQA_GROUP = N_HEADS // N_KV_HEADS
CMP_BLOCK = 32
CMP_STRIDE = 16
CMP_HIDDEN = 256
SLC_BLOCK = 64
SLC_TOP_N = 8
WINDOW = 512
Q_BLOCK = 64
SEL_BIG = 1e9

X_HEADS = 4
X_HEAD_DIM = D_MODEL // X_HEADS

D_FF = 256 * ((8 * D_MODEL + 3 * 256 - 1) // (3 * 256))

DEEPNORM_ALPHA = (2.0 * DEPTH) ** 0.25
DEEPNORM_BETA = (8.0 * DEPTH) ** -0.25
LN_EPS = 1e-5

Q_WIDTH = N_HEADS * HEAD_DIM
KV_WIDTH = 2 * N_KV_HEADS * HEAD_DIM
IN_SPLITS = (SSM_WIDTH, Q_WIDTH, KV_WIDTH, KV_WIDTH, KV_WIDTH, 3 * N_HEADS, 2 * D_MODEL)
IN_WIDTH = sum(IN_SPLITS)

kernel_name = "hybrid_s5_nsa_deepnorm_layer"


def _layer_norm(x, g, b):
    xf = x.astype(jnp.float32)
    mu = jnp.mean(xf, axis=-1, keepdims=True)
    var = jnp.mean(jnp.square(xf - mu), axis=-1, keepdims=True)
    return ((xf - mu) * lax.rsqrt(var + LN_EPS) * g + b).astype(x.dtype)


def _masked_softmax(s, mask):
    s = jnp.where(mask, s, -jnp.inf)
    m = jnp.max(s, axis=-1, keepdims=True)
    m = jnp.where(jnp.isfinite(m), m, 0.0)
    p = jnp.exp(s - m)
    return p / jnp.maximum(jnp.sum(p, axis=-1, keepdims=True), 1e-30)


def _alibi_slopes():
    h = jnp.arange(1, N_HEADS + 1, dtype=jnp.float32)
    return jnp.exp2(-8.0 * h / N_HEADS)


def _s5_branch(u, a_re, a_im, b_re, b_im, c_re, c_im, d_skip, log_dt, w_glu, b_glu, w_o):
    bsz, L, _ = u.shape
    uf = u.astype(jnp.float32)
    ug = uf.reshape(bsz, L, SSM_GROUPS, SSM_GROUP)
    lam_re = jnp.minimum(a_re.astype(jnp.float32), SSM_EIG_CLIP)
    lam_im = a_im.astype(jnp.float32)
    dt = jnp.exp(log_dt.astype(jnp.float32))[:, None]
    mag = jnp.exp(lam_re * dt)
    lb_re = mag * jnp.cos(lam_im * dt)
    lb_im = mag * jnp.sin(lam_im * dt)
    den = jnp.square(lam_re) + jnp.square(lam_im)
    nr = lb_re - 1.0
    f_re = (nr * lam_re + lb_im * lam_im) / den
    f_im = (lb_im * lam_re - nr * lam_im) / den
    br = b_re.astype(jnp.float32)
    bi = b_im.astype(jnp.float32)
    bb_re = f_re[..., None] * br - f_im[..., None] * bi
    bb_im = f_re[..., None] * bi + f_im[..., None] * br
    bu_re = jnp.einsum('blgc,gnc->blgn', ug, bb_re)
    bu_im = jnp.einsum('blgc,gnc->blgn', ug, bb_im)
    shape_a = (1, L, SSM_GROUPS, SSM_STATE)
    a_seq_re = jnp.broadcast_to(lb_re[None, None], shape_a)
    a_seq_im = jnp.broadcast_to(lb_im[None, None], shape_a)

    def combine(e1, e2):
        a1r, a1i, b1r, b1i = e1
        a2r, a2i, b2r, b2i = e2
        return (a1r * a2r - a1i * a2i,
                a1r * a2i + a1i * a2r,
                a2r * b1r - a2i * b1i + b2r,
                a2r * b1i + a2i * b1r + b2i)

    _, _, h_re, h_im = lax.associative_scan(combine, (a_seq_re, a_seq_im, bu_re, bu_im), axis=1)
    y = (jnp.einsum('blgn,gcn->blgc', h_re, c_re.astype(jnp.float32))
         - jnp.einsum('blgn,gcn->blgc', h_im, c_im.astype(jnp.float32)))
    y = y.reshape(bsz, L, SSM_WIDTH) + d_skip.astype(jnp.float32) * uf
    g = jax.nn.gelu(y)
    y = g * jax.nn.sigmoid(g @ w_glu.astype(jnp.float32) + b_glu.astype(jnp.float32))
    return y.astype(u.dtype) @ w_o


def _nsa_branch(q, kv_cmp, kv_slc, kv_win, gate_logits, cmp_pos, cmp_w1, cmp_b1, cmp_w2, w_o):
    bsz, L = q.shape[0], q.shape[1]
    n_cmp = L // CMP_STRIDE - CMP_BLOCK // CMP_STRIDE + 1
    n_slc = L // SLC_BLOCK
    n_sel = min(SLC_TOP_N, n_slc)
    scale = HEAD_DIM ** -0.5
    slopes = _alibi_slopes().reshape(N_KV_HEADS, GQA_GROUP)

    kv = jnp.moveaxis(kv_cmp, 2, 0)
    chunks = kv.reshape(2, bsz, L // CMP_STRIDE, CMP_STRIDE, N_KV_HEADS, HEAD_DIM)
    blocks = jnp.concatenate([chunks[:, :, r:r + n_cmp] for r in range(CMP_BLOCK // CMP_STRIDE)], axis=3)
    blocks = blocks + cmp_pos[:, None, None, :, None, :]
    flat = jnp.moveaxis(blocks, 3, 4).reshape(2, bsz, n_cmp, N_KV_HEADS, CMP_BLOCK * HEAD_DIM)
    hid = jax.nn.gelu(jnp.einsum('zbnhf,zfe->zbnhe', flat, cmp_w1) + cmp_b1[:, None, None, None, :])
    comp = jnp.einsum('zbnhe,zed->zbnhd', hid, cmp_w2)
    k_c, v_c = comp[0], comp[1]
    cmp_start = jnp.arange(n_cmp) * CMP_STRIDE
    cmp_end = cmp_start + CMP_BLOCK - 1
    slc_start = jnp.arange(n_slc) * SLC_BLOCK
    overlap = ((cmp_start[:, None] < slc_start[None, :] + SLC_BLOCK)
               & (cmp_end[:, None] >= slc_start[None, :])).astype(jnp.float32)

    k_s = kv_slc[:, :, 0].reshape(bsz, n_slc, SLC_BLOCK, N_KV_HEADS, HEAD_DIM).transpose(0, 3, 1, 2, 4)
    v_s = kv_slc[:, :, 1].reshape(bsz, n_slc, SLC_BLOCK, N_KV_HEADS, HEAD_DIM).transpose(0, 3, 1, 2, 4)

    pad = ((0, 0), (WINDOW, 0), (0, 0), (0, 0))
    k_w = jnp.pad(kv_win[:, :, 0], pad)
    v_w = jnp.pad(kv_win[:, :, 1], pad)

    qg = q.reshape(bsz, L, N_KV_HEADS, GQA_GROUP, HEAD_DIM)
    b_idx = jnp.arange(bsz)[:, None, None, None]
    h_idx = jnp.arange(N_KV_HEADS)[None, :, None, None]
    blk = jnp.arange(n_slc)
    in_blk = jnp.arange(SLC_BLOCK)
    win_off = jnp.arange(WINDOW + Q_BLOCK) - WINDOW

    def query_block(c):
        t0 = c * Q_BLOCK
        t = t0 + jnp.arange(Q_BLOCK)
        qc = lax.dynamic_slice_in_dim(qg, t0, Q_BLOCK, axis=1)

        s = jnp.einsum('bqhgd,bnhd->bhgqn', qc, k_c, preferred_element_type=jnp.float32) * scale
        dist = t[:, None] - cmp_end[None, :]
        s = s - slopes[:, :, None, None] * dist
        p_cmp = _masked_softmax(s, dist >= 0)
        o_cmp = jnp.einsum('bhgqn,bnhd->bqhgd', p_cmp, v_c)

        imp = jnp.einsum('bhgqn,nj->bhqj', p_cmp, overlap)
        cur = t // SLC_BLOCK
        forced = (blk[None, :] == 0) | (blk[None, :] == cur[:, None]) | (blk[None, :] == cur[:, None] - 1)
        future = blk[None, :] * SLC_BLOCK > t[:, None]
        imp = jnp.where(forced, SEL_BIG, jnp.where(future, -SEL_BIG, imp))
        _, idx = lax.top_k(imp, n_sel)
        k_sel = k_s[b_idx, h_idx, idx].reshape(bsz, N_KV_HEADS, Q_BLOCK, n_sel * SLC_BLOCK, HEAD_DIM)
        v_sel = v_s[b_idx, h_idx, idx].reshape(bsz, N_KV_HEADS, Q_BLOCK, n_sel * SLC_BLOCK, HEAD_DIM)
        s_pos = (idx[..., None] * SLC_BLOCK + in_blk).reshape(bsz, N_KV_HEADS, Q_BLOCK, n_sel * SLC_BLOCK)
        dist = t[:, None] - s_pos
        s = jnp.einsum('bqhgd,bhqsd->bhgqs', qc, k_sel, preferred_element_type=jnp.float32) * scale
        s = s - slopes[:, :, None, None] * dist[:, :, None]
        p = _masked_softmax(s, (dist >= 0)[:, :, None])
        o_slc = jnp.einsum('bhgqs,bhqsd->bqhgd', p, v_sel)

        k_wc = lax.dynamic_slice_in_dim(k_w, t0, WINDOW + Q_BLOCK, axis=1)
        v_wc = lax.dynamic_slice_in_dim(v_w, t0, WINDOW + Q_BLOCK, axis=1)
        w_pos = t0 + win_off
        dist = t[:, None] - w_pos[None, :]
        mask = (dist >= 0) & (dist < WINDOW) & (w_pos[None, :] >= 0)
        s = jnp.einsum('bqhgd,bkhd->bhgqk', qc, k_wc, preferred_element_type=jnp.float32) * scale
        s = s - slopes[:, :, None, None] * dist
        p = _masked_softmax(s, mask)
        o_win = jnp.einsum('bhgqk,bkhd->bqhgd', p, v_wc)
        return jnp.stack([o_cmp, o_slc, o_win], axis=0).astype(q.dtype)

    o = lax.map(query_block, jnp.arange(L // Q_BLOCK))
    o = jnp.moveaxis(o, 0, 2).reshape(3, bsz, L, N_HEADS, HEAD_DIM)
    g = jax.nn.sigmoid(gate_logits.astype(jnp.float32)).reshape(bsz, L, N_HEADS, 3)
    o = jnp.einsum('zblhd,blhz->blhd', o, g)
    return o.reshape(bsz, L, Q_WIDTH).astype(q.dtype) @ w_o


def _memory_cross_attention(h, mem, w_q, w_kv, w_o):
    bsz, L, _ = h.shape
    m = mem.shape[1]
    q = (h @ w_q).reshape(bsz, L, X_HEADS, X_HEAD_DIM)
    kv = (mem @ w_kv).reshape(bsz, m, 2, X_HEADS, X_HEAD_DIM)
    s = jnp.einsum('blhd,bmhd->bhlm', q, kv[:, :, 0], preferred_element_type=jnp.float32) * (X_HEAD_DIM ** -0.5)
    p = jax.nn.softmax(s, axis=-1)
    o = jnp.einsum('bhlm,bmhd->blhd', p, kv[:, :, 1]).astype(h.dtype)
    return o.reshape(bsz, L, D_MODEL) @ w_o


def _swiglu(h, w_in, w_out):
    gate, up = jnp.split(h @ w_in, 2, axis=-1)
    return (jax.nn.silu(gate) * up) @ w_out


def setup_inputs(seed: int = 0) -> dict:
    key = jax.random.key(seed)
    keys = iter(jax.random.split(key, 40))

    def nrm(shape, std):
        return std * jax.random.normal(next(keys), shape, jnp.float32)

    G, N = SSM_GROUPS, SSM_STATE
    n_idx = jnp.arange(N, dtype=jnp.float32)
    return {
        "x": nrm((BATCH, SEQ, D_MODEL), 1.0),
        "mem": nrm((BATCH, MEM_LEN, D_MODEL), 1.0),
        "ln_emb_g": 1.0 + nrm((D_MODEL,), 0.02),
        "ln_emb_b": nrm((D_MODEL,), 0.02),
        "w_in": nrm((DEPTH, D_MODEL, IN_WIDTH), D_MODEL ** -0.5),
        "ssm_a_re": -0.5 + nrm((DEPTH, G, N), 0.01),
        "ssm_a_im": jnp.pi * n_idx + nrm((DEPTH, G, N), 0.01),
        "ssm_b_re": nrm((DEPTH, G, N, SSM_GROUP), (2.0 * SSM_GROUP) ** -0.5),
        "ssm_b_im": nrm((DEPTH, G, N, SSM_GROUP), (2.0 * SSM_GROUP) ** -0.5),
        "ssm_c_re": nrm((DEPTH, G, SSM_GROUP, N), (2.0 * N) ** -0.5),
        "ssm_c_im": nrm((DEPTH, G, SSM_GROUP, N), (2.0 * N) ** -0.5),
        "ssm_d": nrm((DEPTH, SSM_WIDTH), 1.0),
        "ssm_log_dt": jax.random.uniform(next(keys), (DEPTH, G), jnp.float32,
                                         minval=math.log(SSM_DT_MIN), maxval=math.log(SSM_DT_MAX)),
        "ssm_w_glu": nrm((DEPTH, SSM_WIDTH, SSM_WIDTH), SSM_WIDTH ** -0.5),
        "ssm_b_glu": nrm((DEPTH, SSM_WIDTH), 0.01),
        "ssm_w_out": nrm((DEPTH, SSM_WIDTH, D_MODEL), SSM_WIDTH ** -0.5),
        "cmp_pos": nrm((DEPTH, 2, CMP_BLOCK, HEAD_DIM), 0.02),
        "cmp_w1": nrm((DEPTH, 2, CMP_BLOCK * HEAD_DIM, CMP_HIDDEN), (CMP_BLOCK * HEAD_DIM) ** -0.5),
        "cmp_b1": nrm((DEPTH, 2, CMP_HIDDEN), 0.01),
        "cmp_w2": nrm((DEPTH, 2, CMP_HIDDEN, HEAD_DIM), CMP_HIDDEN ** -0.5),
        "nsa_w_out": nrm((DEPTH, Q_WIDTH, D_MODEL), Q_WIDTH ** -0.5),
        "w_out": nrm((DEPTH, D_MODEL, D_MODEL), DEEPNORM_BETA * D_MODEL ** -0.5),
        "ln1_g": 1.0 + nrm((DEPTH, D_MODEL), 0.02),
        "ln1_b": nrm((DEPTH, D_MODEL), 0.02),
        "xattn_w_q": nrm((DEPTH, D_MODEL, D_MODEL), D_MODEL ** -0.5),
        "xattn_w_kv": nrm((DEPTH, D_MODEL, 2 * D_MODEL), D_MODEL ** -0.5),
        "xattn_w_o": nrm((DEPTH, D_MODEL, D_MODEL), DEEPNORM_BETA * D_MODEL ** -0.5),
        "ln2_g": 1.0 + nrm((DEPTH, D_MODEL), 0.02),
        "ln2_b": nrm((DEPTH, D_MODEL), 0.02),
        "ffn_w_in": nrm((DEPTH, D_MODEL, 2 * D_FF), D_MODEL ** -0.5),
        "ffn_w_out": nrm((DEPTH, D_FF, D_MODEL), DEEPNORM_BETA * D_FF ** -0.5),
        "ln3_g": 1.0 + nrm((DEPTH, D_MODEL), 0.02),
        "ln3_b": nrm((DEPTH, D_MODEL), 0.02),
    }


def reference(x, mem, ln_emb_g, ln_emb_b, w_in, ssm_a_re, ssm_a_im, ssm_b_re, ssm_b_im,
              ssm_c_re, ssm_c_im, ssm_d, ssm_log_dt, ssm_w_glu, ssm_b_glu, ssm_w_out,
              cmp_pos, cmp_w1, cmp_b1, cmp_w2, nsa_w_out, w_out, ln1_g, ln1_b,
              xattn_w_q, xattn_w_kv, xattn_w_o, ln2_g, ln2_b,
              ffn_w_in, ffn_w_out, ln3_g, ln3_b):
    bsz, L, _ = x.shape
    cuts = np.cumsum(IN_SPLITS)[:-1].tolist()
    h = _layer_norm(x, ln_emb_g, ln_emb_b)
    for l in range(DEPTH):
        z = h @ w_in[l]
        u, q, kv_c, kv_s, kv_w, g_nsa, g_mix = jnp.split(z, cuts, axis=-1)
        y_ssm = _s5_branch(u, ssm_a_re[l], ssm_a_im[l], ssm_b_re[l], ssm_b_im[l],
                           ssm_c_re[l], ssm_c_im[l], ssm_d[l], ssm_log_dt[l],
                           ssm_w_glu[l], ssm_b_glu[l], ssm_w_out[l])
        kv_shape = (bsz, L, 2, N_KV_HEADS, HEAD_DIM)
        y_nsa = _nsa_branch(q.reshape(bsz, L, N_HEADS, HEAD_DIM), kv_c.reshape(kv_shape),
                            kv_s.reshape(kv_shape), kv_w.reshape(kv_shape), g_nsa,
                            cmp_pos[l], cmp_w1[l], cmp_b1[l], cmp_w2[l], nsa_w_out[l])
        gate_ssm, gate_nsa = jnp.split(jax.nn.sigmoid(g_mix), 2, axis=-1)
        mixed = (gate_ssm * y_ssm + gate_nsa * y_nsa) @ w_out[l]
        h = _layer_norm(DEEPNORM_ALPHA * h + mixed, ln1_g[l], ln1_b[l])
        h = _layer_norm(DEEPNORM_ALPHA * h + _memory_cross_attention(h, mem, xattn_w_q[l], xattn_w_kv[l], xattn_w_o[l]),
                        ln2_g[l], ln2_b[l])
        h = _layer_norm(DEEPNORM_ALPHA * h + _swiglu(h, ffn_w_in[l], ffn_w_out[l]), ln3_g[l], ln3_b[l])
    return h
```

```python
import functools
import math

import jax
import jax.numpy as jnp
from jax import lax
from jax.experimental import pallas as pl
from jax.experimental.pallas import tpu as pltpu

F32 = jnp.float32
BF16 = jnp.bfloat16

D_MODEL = 1024
SSM_WIDTH = 512
SSM_GROUP = 16
SSM_GROUPS = SSM_WIDTH // SSM_GROUP
SSM_STATE = 64
SSM_STATES = SSM_GROUPS * SSM_STATE
SSM_EIG_CLIP = -1e-4
N_HEADS = 16
N_KV_HEADS = 4
HEAD_DIM = 64
GQA_GROUP = N_HEADS // N_KV_HEADS
KV_DIM = N_KV_HEADS * HEAD_DIM
CMP_BLOCK = 32
CMP_STRIDE = 16
CMP_HIDDEN = 256
SLC_BLOCK = 64
SLC_TOP_N = 8
WINDOW = 512
SEL_BIG = 1e9
X_HEADS = 4
X_HEAD_DIM = D_MODEL // X_HEADS
D_FF = 2816
DEEPNORM_ALPHA = 2.0 ** 0.25
LN_EPS = 1e-5
NEG_BIG = -1e30
GATE_ROWS = 16

V7X_VMEM_LIMIT_BYTES = 56 * 1024 * 1024
SSM_LANE_SLAB = 128
SSM_SLABS = SSM_WIDTH // SSM_LANE_SLAB
SSM_STATE_SLAB = SSM_STATES // SSM_SLABS


def _cparams(*sem):
    return pltpu.CompilerParams(dimension_semantics=sem, vmem_limit_bytes=V7X_VMEM_LIMIT_BYTES)


def _full(shape):
    zeros = (0,) * len(shape)
    return pl.BlockSpec(shape, lambda *_: zeros)


def _layer_norm(x, g, b):
    mu = jnp.mean(x, axis=-1, keepdims=True)
    xc = x - mu
    var = jnp.mean(xc * xc, axis=-1, keepdims=True)
    return xc * lax.rsqrt(var + LN_EPS) * g + b


def _gelu_tanh(x):
    return 0.5 * x * (1.0 + jnp.tanh(math.sqrt(2.0 / math.pi) * (x + 0.044715 * (x * x * x))))


def _sigmoid(x):
    return 1.0 / (1.0 + jnp.exp(-x))


def _dot(a, b):
    return jnp.dot(a, b, preferred_element_type=F32)


def _dot_nt(a, b):
    return lax.dot_general(a, b, (((1,), (1,)), ((), ())), preferred_element_type=F32)


def _zoh_kernel(a_re, a_im, log_dt, b_re, b_im, lb_re_o, lb_im_o, bb_re_o, bb_im_o):
    lam_re = jnp.minimum(a_re[...], SSM_EIG_CLIP)
    lam_im = a_im[...]
    dt = jnp.exp(log_dt[...])
    mag = jnp.exp(lam_re * dt)
    lb_re = mag * jnp.cos(lam_im * dt)
    lb_im = mag * jnp.sin(lam_im * dt)
    den = lam_re * lam_re + lam_im * lam_im
    nr = lb_re - 1.0
    f_re = (nr * lam_re + lb_im * lam_im) / den
    f_im = (lb_im * lam_re - nr * lam_im) / den
    br = b_re[...]
    bi = b_im[...]
    lb_re_o[...] = lb_re
    lb_im_o[...] = lb_im
    bb_re_o[...] = f_re * br - f_im * bi
    bb_im_o[...] = f_re * bi + f_im * br


def _zoh_prep(a_re, a_im, log_dt, b_re, b_im):
    gn = SSM_STATES
    col = lambda v: v.astype(F32).reshape(gn, 1)
    dt_col = jnp.broadcast_to(log_dt.astype(F32)[:, None], (SSM_GROUPS, SSM_STATE)).reshape(gn, 1)
    outs = pl.pallas_call(
        _zoh_kernel,
        out_shape=[jax.ShapeDtypeStruct((gn, 1), F32)] * 2 + [jax.ShapeDtypeStruct((gn, SSM_GROUP), F32)] * 2,
        name="zoh_prep",
    )(col(a_re), col(a_im), dt_col, b_re.astype(F32).reshape(gn, SSM_GROUP), b_im.astype(F32).reshape(gn, SSM_GROUP))
    lb_re, lb_im, bb_re, bb_im = outs
    shape_b = (SSM_GROUPS, SSM_STATE, SSM_GROUP)
    return lb_re.reshape(1, gn), lb_im.reshape(1, gn), bb_re.reshape(shape_b), bb_im.reshape(shape_b)


def _inproj_kernel(x_ref, g_ref, b_ref, wu_ref, wqt_ref, wkc_ref, wks_ref, wkw_ref, wvst_ref, wvwt_ref, wgt_ref,
                   h_o, u_o, qt_o, kvc_o, ks_o, kw_o, vst_o, vwt_o, gt_o):
    h = _layer_norm(x_ref[0], g_ref[...], b_ref[...])
    h_o[0] = h
    hb = h.astype(BF16)
    u_o[...] = _dot(hb, wu_ref[...])
    qt_o[0] = (_dot_nt(wqt_ref[...], hb) * (HEAD_DIM ** -0.5)).astype(BF16)
    for j in range(2 * N_KV_HEADS):
        kvc_o[0, j] = _dot(hb, wkc_ref[j])
    for j in range(N_KV_HEADS):
        ks_o[0, j] = _dot(hb, wks_ref[j]).astype(BF16)
        kw_o[0, j] = _dot(hb, wkw_ref[j]).astype(BF16)
    vst_o[0] = _dot_nt(wvst_ref[...], hb).astype(BF16)
    vwt_o[0] = _dot_nt(wvwt_ref[...], hb).astype(BF16)
    gt_o[0] = _sigmoid(_dot_nt(wgt_ref[...], hb))


def _ln_inproj(x, ln_g, ln_b, w, tl):
    bsz, L, d = x.shape
    n_gate = N_KV_HEADS * GATE_ROWS
    grid = (bsz, L // tl)
    in_specs = [
        pl.BlockSpec((1, tl, d), lambda b, i: (b, i, 0)),
        _full((1, d)), _full((1, d)),
        _full((d, SSM_WIDTH)),
        _full((D_MODEL, d)),
        _full((2 * N_KV_HEADS, d, HEAD_DIM)),
        _full((N_KV_HEADS, d, HEAD_DIM)),
        _full((N_KV_HEADS, d, HEAD_DIM)),
        _full((KV_DIM, d)),
        _full((KV_DIM, d)),
        _full((n_gate, d)),
    ]
    out_shape = [
        jax.ShapeDtypeStruct((bsz, L, d), F32),
        jax.ShapeDtypeStruct((L, bsz * SSM_WIDTH), F32),
        jax.ShapeDtypeStruct((bsz, D_MODEL, L), BF16),
        jax.ShapeDtypeStruct((bsz, 2 * N_KV_HEADS, L, HEAD_DIM), F32),
        jax.ShapeDtypeStruct((bsz, N_KV_HEADS, L, HEAD_DIM), BF16),
        jax.ShapeDtypeStruct((bsz, N_KV_HEADS, L, HEAD_DIM), BF16),
        jax.ShapeDtypeStruct((bsz, KV_DIM, L), BF16),
        jax.ShapeDtypeStruct((bsz, KV_DIM, L), BF16),
        jax.ShapeDtypeStruct((bsz, n_gate, L), F32),
    ]
    out_specs = [
        pl.BlockSpec((1, tl, d), lambda b, i: (b, i, 0)),
        pl.BlockSpec((tl, SSM_WIDTH), lambda b, i: (i, b)),
        pl.BlockSpec((1, D_MODEL, tl), lambda b, i: (b, 0, i)),
        pl.BlockSpec((1, 2 * N_KV_HEADS, tl, HEAD_DIM), lambda b, i: (b, 0, i, 0)),
        pl.BlockSpec((1, N_KV_HEADS, tl, HEAD_DIM), lambda b, i: (b, 0, i, 0)),
        pl.BlockSpec((1, N_KV_HEADS, tl, HEAD_DIM), lambda b, i: (b, 0, i, 0)),
        pl.BlockSpec((1, KV_DIM, tl), lambda b, i: (b, 0, i)),
        pl.BlockSpec((1, KV_DIM, tl), lambda b, i: (b, 0, i)),
        pl.BlockSpec((1, n_gate, tl), lambda b, i: (b, 0, i)),
    ]
    return pl.pallas_call(
        _inproj_kernel, grid=grid, in_specs=in_specs, out_specs=out_specs, out_shape=out_shape,
        compiler_params=_cparams("parallel", "parallel"), name="ln_inproj",
    )(x, ln_g.reshape(1, d), ln_b.reshape(1, d), w["wu"], w["wqt"], w["wkc"], w["wks"], w["wkw"],
      w["wvst"], w["wvwt"], w["wgt"])


def _s5_kernel(u_ref, lre_ref, lim_ref, wbre_ref, wbim_ref, cre_ref, cim_ref, d_ref, wglu_ref, bglu_ref, wo_ref,
               y_o, sre, sim, hre, him, *, bsz, steps):
    @pl.when(pl.program_id(0) == 0)
    def _():
        sre[...] = jnp.zeros_like(sre)
        sim[...] = jnp.zeros_like(sim)

    u = u_ref[...]
    ub = u.astype(BF16)
    for j in range(SSM_SLABS):
        uj = ub[:, j * SSM_LANE_SLAB:(j + 1) * SSM_LANE_SLAB]
        cols = slice(j * SSM_STATE_SLAB, (j + 1) * SSM_STATE_SLAB)
        hre[:, cols] = _dot(uj, wbre_ref[j])
        him[:, cols] = _dot(uj, wbim_ref[j])

    for j in range(SSM_SLABS):
        cols = slice(j * SSM_STATE_SLAB, (j + 1) * SSM_STATE_SLAB)
        lr = jnp.broadcast_to(lre_ref[:, cols], (bsz, SSM_STATE_SLAB))
        li = jnp.broadcast_to(lim_ref[:, cols], (bsz, SSM_STATE_SLAB))

        def step(t, carry, cols=cols, lr=lr, li=li):
            pr, pi = carry
            rows = pl.ds(pl.multiple_of(t * bsz, bsz), bsz)
            nr = lr * pr - li * pi + hre[rows, cols]
            ni = lr * pi + li * pr + him[rows, cols]
            hre[rows, cols] = nr
            him[rows, cols] = ni
            return nr, ni

        fr, fi = lax.fori_loop(0, steps, step, (sre[:, cols], sim[:, cols]), unroll=4)
        sre[:, cols] = fr
        sim[:, cols] = fi

    ys = []
    for j in range(SSM_SLABS):
        cols = slice(j * SSM_STATE_SLAB, (j + 1) * SSM_STATE_SLAB)
        ys.append(_dot(hre[:, cols].astype(BF16), cre_ref[j]) - _dot(him[:, cols].astype(BF16), cim_ref[j]))
    y = jnp.concatenate(ys, axis=-1) + d_ref[...] * u
    g = _gelu_tanh(y)
    y2 = g * _sigmoid(_dot(g.astype(BF16), wglu_ref[...]) + bglu_ref[...])
    y_o[...] = _dot(y2.astype(BF16), wo_ref[...])


def _s5(u_tb, bsz, lb_re, lb_im, w, steps):
    rows_total = u_tb.shape[0]
    rows = steps * bsz
    grid = (rows_total // rows,)
    kern = functools.partial(_s5_kernel, bsz=bsz, steps=steps)
    in_specs = [
        pl.BlockSpec((rows, SSM_WIDTH), lambda c: (c, 0)),
        _full((1, SSM_STATES)), _full((1, SSM_STATES)),
        _full((SSM_SLABS, SSM_LANE_SLAB, SSM_STATE_SLAB)), _full((SSM_SLABS, SSM_LANE_SLAB, SSM_STATE_SLAB)),
        _full((SSM_SLABS, SSM_STATE_SLAB, SSM_LANE_SLAB)), _full((SSM_SLABS, SSM_STATE_SLAB, SSM_LANE_SLAB)),
        _full((1, SSM_WIDTH)),
        _full((SSM_WIDTH, SSM_WIDTH)), _full((1, SSM_WIDTH)),
        _full((SSM_WIDTH, D_MODEL)),
    ]
    return pl.pallas_call(
        kern, grid=grid, in_specs=in_specs,
        out_specs=pl.BlockSpec((rows, D_MODEL), lambda c: (c, 0)),
        out_shape=jax.ShapeDtypeStruct((rows_total, D_MODEL), F32),
        scratch_shapes=[pltpu.VMEM((bsz, SSM_STATES), F32), pltpu.VMEM((bsz, SSM_STATES), F32),
                        pltpu.VMEM((rows, SSM_STATES), F32), pltpu.VMEM((rows, SSM_STATES), F32)],
        compiler_params=_cparams("arbitrary"), name="s5",
    )(u_tb, lb_re, lb_im, w["wbre"], w["wbim"], w["cre"], w["cim"], w["d"], w["wglu"], w["bglu"], w["wo"])


def _compress_kernel(kv_ref, pos_ref, w1_ref, b1_ref, w2k_ref, w2vt_ref, kc_o, vct_o, *, n_chunks):
    half = CMP_BLOCK // 2
    for z in range(2):
        for hh in range(N_KV_HEADS):
            j = z * N_KV_HEADS + hh
            first = jnp.zeros((n_chunks, CMP_HIDDEN), F32)
            second = jnp.zeros((n_chunks, CMP_HIDDEN), F32)
            for p in range(half):
                xp = kv_ref[0, j, pl.ds(p, n_chunks, stride=CMP_STRIDE), :]
                first += _dot((xp + pos_ref[z, p:p + 1, :]).astype(BF16), w1_ref[z, p * HEAD_DIM:(p + 1) * HEAD_DIM, :])
                q = half + p
                second += _dot((xp + pos_ref[z, q:q + 1, :]).astype(BF16), w1_ref[z, q * HEAD_DIM:(q + 1) * HEAD_DIM, :])
            pre = first + pltpu.roll(second, n_chunks - 1, 0) + b1_ref[z]
            hid = _gelu_tanh(pre).astype(BF16)
            if z == 0:
                kc_o[0, hh] = _dot(hid, w2k_ref[...]).astype(BF16)
            else:
                vct_o[0, hh * HEAD_DIM:(hh + 1) * HEAD_DIM, :] = _dot_nt(w2vt_ref[...], hid).astype(BF16)


def _compress(kvc, pos, w1, b1, w2k, w2vt):
    bsz, _, L, _ = kvc.shape
    n_chunks = L // CMP_STRIDE
    kern = functools.partial(_compress_kernel, n_chunks=n_chunks)
    return pl.pallas_call(
        kern, grid=(bsz,),
        in_specs=[
            pl.BlockSpec((1, 2 * N_KV_HEADS, L, HEAD_DIM), lambda b: (b, 0, 0, 0)),
            _full((2, CMP_BLOCK, HEAD_DIM)),
            _full((2, CMP_BLOCK * HEAD_DIM, CMP_HIDDEN)),
            _full((2, 1, CMP_HIDDEN)),
            _full((CMP_HIDDEN, HEAD_DIM)),
            _full((HEAD_DIM, CMP_HIDDEN)),
        ],
        out_specs=[
            pl.BlockSpec((1, N_KV_HEADS, n_chunks, HEAD_DIM), lambda b: (b, 0, 0, 0)),
            pl.BlockSpec((1, KV_DIM, n_chunks), lambda b: (b, 0, 0)),
        ],
        out_shape=[
            jax.ShapeDtypeStruct((bsz, N_KV_HEADS, n_chunks, HEAD_DIM), BF16),
            jax.ShapeDtypeStruct((bsz, KV_DIM, n_chunks), BF16),
        ],
        compiler_params=_cparams("parallel"), name="compress",
    )(kvc, pos, w1, b1, w2k, w2vt)


def _nsa_kernel(qt_ref, kc_ref, vct_ref, ks_ref, vst_ref, kw_ref, vwt_ref, gt_ref, ov_ref, o_ref,
                sel_ref, m_ref, l_ref, acc_ref, *, tq, n_cmp_pad, n_slc):
    hkv = pl.program_id(1)
    i = pl.program_id(2)
    t0 = i * tq
    tk = tq
    n_sel = min(SLC_TOP_N, n_slc)
    slopes = [jnp.exp2(jnp.full((1, tq), -0.5, F32) * (hkv * GQA_GROUP + g + 1).astype(F32))
              for g in range(GQA_GROUP)]
    q_heads = [qt_ref[0, g * HEAD_DIM:(g + 1) * HEAD_DIM, :] for g in range(GQA_GROUP)]

    t_lane = t0 + lax.broadcasted_iota(jnp.int32, (1, tq), 1)

    n_idx = lax.broadcasted_iota(jnp.int32, (n_cmp_pad, tq), 0)
    dist_c = (t0 + lax.broadcasted_iota(jnp.int32, (n_cmp_pad, tq), 1)) - (n_idx * CMP_STRIDE + (CMP_BLOCK - 1))
    mask_c = dist_c >= 0
    dist_cf = dist_c.astype(F32)
    kc = kc_ref[0, 0]
    vct = vct_ref[0]
    o_cmp = []
    p_sum = jnp.zeros((n_cmp_pad, tq), F32)
    for g in range(GQA_GROUP):
        s = _dot(kc, q_heads[g]) - slopes[g] * dist_cf
        s = jnp.where(mask_c, s, NEG_BIG)
        m = jnp.max(s, axis=0, keepdims=True)
        p = jnp.where(mask_c, jnp.exp(s - m), 0.0)
        p = p / jnp.maximum(jnp.sum(p, axis=0, keepdims=True), 1e-30)
        o_cmp.append(_dot(vct, p.astype(BF16)))
        p_sum = p_sum + p

    imp = lax.dot_general(ov_ref[...], p_sum, (((1,), (0,)), ((), ())), precision=lax.Precision.HIGHEST,
                          preferred_element_type=F32)
    blk = lax.broadcasted_iota(jnp.int32, (n_slc, tq), 0)
    cur = t_lane // SLC_BLOCK
    forced = (blk == 0) | (blk == cur) | (blk == cur - 1)
    future = blk * SLC_BLOCK > t_lane
    imp = jnp.where(forced, SEL_BIG, jnp.where(future, -SEL_BIG, imp))
    rank = jnp.zeros((n_slc, tq), jnp.int32)
    for j in range(n_slc):
        row = imp[j:j + 1, :]
        beats = (row > imp) | ((row == imp) & (blk > j))
        rank = rank + beats.astype(jnp.int32)
    sel_ref[...] = (rank < n_sel).astype(F32)

    lane_minus_sub = (lax.broadcasted_iota(jnp.int32, (tk, tq), 1) - lax.broadcasted_iota(jnp.int32, (tk, tq), 0))
    blocks_per_chunk = tk // SLC_BLOCK

    def attend(k_ref, vt_ref, c, mask_fn):
        k0 = pl.multiple_of(c * tk, tk)
        k = k_ref[0, 0, pl.ds(k0, tk), :]
        vt = vt_ref[0, :, pl.ds(k0, tk)]
        dist = lane_minus_sub + (t0 - k0)
        mask = mask_fn(c, dist)
        dist_f = dist.astype(F32)
        for g in range(GQA_GROUP):
            s = _dot(k, q_heads[g]) - slopes[g] * dist_f
            s = jnp.where(mask, s, NEG_BIG)
            m_old = m_ref[g]
            m_new = jnp.maximum(m_old, jnp.max(s, axis=0, keepdims=True))
            alpha = jnp.exp(m_old - m_new)
            p = jnp.where(mask, jnp.exp(s - m_new), 0.0)
            l_ref[g] = alpha * l_ref[g] + jnp.sum(p, axis=0, keepdims=True)
            acc_ref[g] = alpha * acc_ref[g] + _dot(vt, p.astype(BF16))
            m_ref[g] = m_new

    def reset():
        m_ref[...] = jnp.full_like(m_ref, NEG_BIG)
        l_ref[...] = jnp.zeros_like(l_ref)
        acc_ref[...] = jnp.zeros_like(acc_ref)

    def finish():
        return [acc_ref[g] / jnp.maximum(l_ref[g], 1e-30) for g in range(GQA_GROUP)]

    def slc_mask(c, dist):
        rows = [jnp.broadcast_to(sel_ref[pl.ds(c * blocks_per_chunk + r, 1), :], (SLC_BLOCK, tq))
                for r in range(blocks_per_chunk)]
        return (jnp.concatenate(rows, axis=0) > 0.5) & (dist >= 0)

    reset()
    def slc_step(c, carry):
        attend(ks_ref, vst_ref, c, slc_mask)
        return carry

    lax.fori_loop(0, i + 1, slc_step, 0)
    o_slc = finish()

    def win_mask(c, dist):
        return (dist >= 0) & (dist < WINDOW)

    reset()
    def win_step(c, carry):
        attend(kw_ref, vwt_ref, c, win_mask)
        return carry

    lax.fori_loop(jnp.maximum(i - WINDOW // tk, 0), i + 1, win_step, 0)
    o_win = finish()

    for g in range(GQA_GROUP):
        gates = [gt_ref[0, 3 * g + z:3 * g + z + 1, :] for z in range(3)]
        o_ref[0, g * HEAD_DIM:(g + 1) * HEAD_DIM, :] = gates[0] * o_cmp[g] + gates[1] * o_slc[g] + gates[2] * o_win[g]


def _nsa_attn(qt, kc, vct, ks, vst, kw, vwt, gt, tq):
    bsz, _, L = qt.shape
    n_cmp_pad = kc.shape[2]
    n_slc = L // SLC_BLOCK
    group_rows = GQA_GROUP * HEAD_DIM
    n = jnp.arange(n_cmp_pad)[None, :]
    j = jnp.arange(n_slc)[:, None]
    overlap_t = ((n * CMP_STRIDE < (j + 1) * SLC_BLOCK) & (n * CMP_STRIDE + CMP_BLOCK - 1 >= j * SLC_BLOCK)).astype(F32)
    kern = functools.partial(_nsa_kernel, tq=tq, n_cmp_pad=n_cmp_pad, n_slc=n_slc)
    return pl.pallas_call(
        kern, grid=(bsz, N_KV_HEADS, L // tq),
        in_specs=[
            pl.BlockSpec((1, group_rows, tq), lambda b, h, i: (b, h, i)),
            pl.BlockSpec((1, 1, n_cmp_pad, HEAD_DIM), lambda b, h, i: (b, h, 0, 0)),
            pl.BlockSpec((1, HEAD_DIM, n_cmp_pad), lambda b, h, i: (b, h, 0)),
            pl.BlockSpec((1, 1, L, HEAD_DIM), lambda b, h, i: (b, h, 0, 0)),
            pl.BlockSpec((1, HEAD_DIM, L), lambda b, h, i: (b, h, 0)),
            pl.BlockSpec((1, 1, L, HEAD_DIM), lambda b, h, i: (b, h, 0, 0)),
            pl.BlockSpec((1, HEAD_DIM, L), lambda b, h, i: (b, h, 0)),
            pl.BlockSpec((1, GATE_ROWS, tq), lambda b, h, i: (b, h, i)),
            _full((n_slc, n_cmp_pad)),
        ],
        out_specs=pl.BlockSpec((1, group_rows, tq), lambda b, h, i: (b, h, i)),
        out_shape=jax.ShapeDtypeStruct((bsz, D_MODEL, L), F32),
        scratch_shapes=[pltpu.VMEM((n_slc, tq), F32),
                        pltpu.VMEM((GQA_GROUP, 1, tq), F32), pltpu.VMEM((GQA_GROUP, 1, tq), F32),
                        pltpu.VMEM((GQA_GROUP, HEAD_DIM, tq), F32)],
        compiler_params=_cparams("parallel", "parallel", "parallel"), name="nsa_attn",
    )(qt, kc, vct, ks, vst, kw, vwt, gt, overlap_t)


def _mix_kernel(h_ref, ys_ref, ot_ref, wgm_ref, wno_ref, wout_ref, g_ref, b_ref, h1_o):
    h = h_ref[0]
    o = ot_ref[0].T.astype(BF16)
    y_nsa = _dot(o, wno_ref[...])
    gm = _sigmoid(_dot(h.astype(BF16), wgm_ref[...]))
    mix = gm[:, :D_MODEL] * ys_ref[...] + gm[:, D_MODEL:] * y_nsa
    mixed = _dot(mix.astype(BF16), wout_ref[...])
    h1_o[0] = _layer_norm(DEEPNORM_ALPHA * h + mixed, g_ref[...], b_ref[...])


def _mix_ln1(h, y_ssm_tb, o_t, wgm, wno, wout, g, b, tm):
    bsz, L, d = h.shape
    return pl.pallas_call(
        _mix_kernel, grid=(bsz, L // tm),
        in_specs=[
            pl.BlockSpec((1, tm, d), lambda b, i: (b, i, 0)),
            pl.BlockSpec((tm, d), lambda b, i: (i, b)),
            pl.BlockSpec((1, d, tm), lambda b, i: (b, 0, i)),
            _full((d, 2 * d)), _full((d, d)), _full((d, d)), _full((1, d)), _full((1, d)),
        ],
        out_specs=pl.BlockSpec((1, tm, d), lambda b, i: (b, i, 0)),
        out_shape=jax.ShapeDtypeStruct((bsz, L, d), F32),
        compiler_params=_cparams("parallel", "parallel"), name="mix_ln1",
    )(h, y_ssm_tb, o_t, wgm, wno, wout, g.reshape(1, d), b.reshape(1, d))


def _memkv_kernel(mem_ref, w_ref, k_o, v_o):
    kv = _dot(mem_ref[0].astype(BF16), w_ref[...])
    k_o[0] = kv[:, :D_MODEL].astype(BF16)
    v_o[0] = kv[:, D_MODEL:].astype(BF16)


def _mem_kv(mem, w_kv):
    bsz, m, d = mem.shape
    spec = pl.BlockSpec((1, m, d), lambda b: (b, 0, 0))
    return pl.pallas_call(
        _memkv_kernel, grid=(bsz,),
        in_specs=[spec, _full((d, 2 * d))],
        out_specs=[spec, spec],
        out_shape=[jax.ShapeDtypeStruct((bsz, m, d), BF16)] * 2,
        compiler_params=_cparams("parallel"), name="mem_kv",
    )(mem, w_kv)


def _xattn_kernel(h_ref, k_ref, v_ref, wq_ref, wo_ref, g_ref, b_ref, h2_o):
    h = h_ref[0]
    q = (_dot(h.astype(BF16), wq_ref[...]) * (X_HEAD_DIM ** -0.5)).astype(BF16)
    outs = []
    for hd in range(X_HEADS):
        cols = slice(hd * X_HEAD_DIM, (hd + 1) * X_HEAD_DIM)
        s = _dot_nt(q[:, cols], k_ref[0, :, cols])
        p = jnp.exp(s - jnp.max(s, axis=-1, keepdims=True))
        p = p / jnp.sum(p, axis=-1, keepdims=True)
        outs.append(_dot(p.astype(BF16), v_ref[0, :, cols]))
    o = jnp.concatenate(outs, axis=-1).astype(BF16)
    h2_o[0] = _layer_norm(DEEPNORM_ALPHA * h + _dot(o, wo_ref[...]), g_ref[...], b_ref[...])


def _xattn_ln2(h, k, v, wq, wo, g, b, tm):
    bsz, L, d = h.shape
    m = k.shape[1]
    row = pl.BlockSpec((1, tm, d), lambda b, i: (b, i, 0))
    kv = pl.BlockSpec((1, m, d), lambda b, i: (b, 0, 0))
    return pl.pallas_call(
        _xattn_kernel, grid=(bsz, L // tm),
        in_specs=[row, kv, kv, _full((d, d)), _full((d, d)), _full((1, d)), _full((1, d))],
        out_specs=row,
        out_shape=jax.ShapeDtypeStruct((bsz, L, d), F32),
        compiler_params=_cparams("parallel", "parallel"), name="xattn_ln2",
    )(h, k, v, wq, wo, g.reshape(1, d), b.reshape(1, d))


def _ffn_kernel(h_ref, win_ref, wout_ref, g_ref, b_ref, o_ref):
    h = h_ref[...]
    gu = _dot(h.astype(BF16), win_ref[...])
    gate = gu[:, :D_FF]
    act = gate * _sigmoid(gate) * gu[:, D_FF:]
    o_ref[...] = _layer_norm(DEEPNORM_ALPHA * h + _dot(act.astype(BF16), wout_ref[...]), g_ref[...], b_ref[...])


def _ffn_ln3(h, win, wout, g, b, tm):
    rows, d = h.shape
    row = pl.BlockSpec((tm, d), lambda i: (i, 0))
    return pl.pallas_call(
        _ffn_kernel, grid=(rows // tm,),
        in_specs=[row, _full((d, 2 * D_FF)), _full((D_FF, d)), _full((1, d)), _full((1, d))],
        out_specs=row,
        out_shape=jax.ShapeDtypeStruct((rows, d), F32),
        compiler_params=_cparams("parallel"), name="ffn_ln3",
    )(h, win, wout, g.reshape(1, d), b.reshape(1, d))


def _inproj_weights(w_in):
    d = D_MODEL
    c0 = SSM_WIDTH
    c1 = c0 + N_HEADS * HEAD_DIM
    c2 = c1 + 2 * KV_DIM
    c3 = c2 + 2 * KV_DIM
    c4 = c3 + 2 * KV_DIM
    c5 = c4 + 3 * N_HEADS
    per_head = lambda m: m.reshape(d, -1, HEAD_DIM).transpose(1, 0, 2)
    wg = w_in[:, c4:c5].T.reshape(N_KV_HEADS, 3 * GQA_GROUP, d)
    wg = jnp.pad(wg, ((0, 0), (0, GATE_ROWS - 3 * GQA_GROUP), (0, 0))).reshape(N_KV_HEADS * GATE_ROWS, d)
    return {
        "wu": w_in[:, :c0].astype(BF16),
        "wqt": w_in[:, c0:c1].T.astype(BF16),
        "wkc": per_head(w_in[:, c1:c2]).astype(BF16),
        "wks": per_head(w_in[:, c2:c2 + KV_DIM]).astype(BF16),
        "wvst": w_in[:, c2 + KV_DIM:c3].T.astype(BF16),
        "wkw": per_head(w_in[:, c3:c3 + KV_DIM]).astype(BF16),
        "wvwt": w_in[:, c3 + KV_DIM:c4].T.astype(BF16),
        "wgt": wg.astype(BF16),
        "wgm": w_in[:, c5:].astype(BF16),
    }


def _s5_weights(bb_re, bb_im, c_re, c_im, d_skip, w_glu, b_glu, w_o):
    per_slab = SSM_LANE_SLAB // SSM_GROUP
    eye = jnp.eye(per_slab, dtype=F32)

    def b_blocks(bb):
        bb = bb.reshape(SSM_SLABS, per_slab, SSM_STATE, SSM_GROUP)
        return jnp.einsum('jgnc,gh->jgchn', bb, eye).reshape(SSM_SLABS, SSM_LANE_SLAB, SSM_STATE_SLAB).astype(BF16)

    def c_blocks(c):
        c = c.astype(F32).reshape(SSM_SLABS, per_slab, SSM_GROUP, SSM_STATE)
        return jnp.einsum('jgcn,gh->jgnhc', c, eye).reshape(SSM_SLABS, SSM_STATE_SLAB, SSM_LANE_SLAB).astype(BF16)

    return {
        "wbre": b_blocks(bb_re), "wbim": b_blocks(bb_im), "cre": c_blocks(c_re), "cim": c_blocks(c_im),
        "d": d_skip.astype(F32).reshape(1, SSM_WIDTH), "wglu": w_glu.astype(BF16),
        "bglu": b_glu.astype(F32).reshape(1, SSM_WIDTH), "wo": w_o.astype(BF16),
    }


def _pick(total, want):
    return want if total % want == 0 else total


def kernel(x, mem, ln_emb_g, ln_emb_b, w_in, ssm_a_re, ssm_a_im, ssm_b_re, ssm_b_im, ssm_c_re, ssm_c_im, ssm_d,
           ssm_log_dt, ssm_w_glu, ssm_b_glu, ssm_w_out, cmp_pos, cmp_w1, cmp_b1, cmp_w2, nsa_w_out, w_out,
           ln1_g, ln1_b, xattn_w_q, xattn_w_kv, xattn_w_o, ln2_g, ln2_b, ffn_w_in, ffn_w_out, ln3_g, ln3_b):
    bsz, L, d = x.shape
    assert w_in.shape[0] == 1, "one layer: the trunk-entry LayerNorm is fused into its input projection"
    l = 0
    tile = _pick(L, 256)
    wi = _inproj_weights(w_in[l])
    h, u_tb, qt, kvc, ks, kw, vst, vwt, gt = _ln_inproj(x, ln_emb_g, ln_emb_b, wi, tile)

    lb_re, lb_im, bb_re, bb_im = _zoh_prep(ssm_a_re[l], ssm_a_im[l], ssm_log_dt[l], ssm_b_re[l], ssm_b_im[l])
    ws = _s5_weights(bb_re, bb_im, ssm_c_re[l], ssm_c_im[l], ssm_d[l], ssm_w_glu[l], ssm_b_glu[l], ssm_w_out[l])
    y_ssm = _s5(u_tb.reshape(L * bsz, SSM_WIDTH), bsz, lb_re, lb_im, ws, _pick(L, 32))

    kc, vct = _compress(kvc, cmp_pos[l].astype(F32), cmp_w1[l].astype(BF16),
                        cmp_b1[l].astype(F32).reshape(2, 1, CMP_HIDDEN),
                        cmp_w2[l, 0].astype(BF16), cmp_w2[l, 1].T.astype(BF16))
    o_t = _nsa_attn(qt, kc, vct, ks, vst, kw, vwt, gt, tile)

    h = _mix_ln1(h, y_ssm.reshape(L, bsz * d), o_t, wi["wgm"], nsa_w_out[l].astype(BF16), w_out[l].astype(BF16),
                 ln1_g[l], ln1_b[l], tile)
    mk, mv = _mem_kv(mem, xattn_w_kv[l].astype(BF16))
    h = _xattn_ln2(h, mk, mv, xattn_w_q[l].astype(BF16), xattn_w_o[l].astype(BF16), ln2_g[l], ln2_b[l], tile)
    h = _ffn_ln3(h.reshape(bsz * L, d), ffn_w_in[l].astype(BF16), ffn_w_out[l].astype(BF16),
                 ln3_g[l], ln3_b[l], tile)
    return h.reshape(bsz, L, d)
```

```python
import functools
import math

import jax
import jax.numpy as jnp
from jax import lax
from jax.experimental import pallas as pl
from jax.experimental.pallas import tpu as pltpu

F32 = jnp.float32
BF16 = jnp.bfloat16

D_MODEL = 1024
SSM_WIDTH = 512
SSM_GROUP = 16
SSM_GROUPS = SSM_WIDTH // SSM_GROUP
SSM_STATE = 64
SSM_STATES = SSM_GROUPS * SSM_STATE
SSM_EIG_CLIP = -1e-4
N_HEADS = 16
N_KV_HEADS = 4
HEAD_DIM = 64
GQA_GROUP = N_HEADS // N_KV_HEADS
KV_DIM = N_KV_HEADS * HEAD_DIM
CMP_BLOCK = 32
CMP_STRIDE = 16
CMP_HIDDEN = 256
SLC_BLOCK = 64
SLC_TOP_N = 8
WINDOW = 512
SEL_BIG = 1e9
X_HEADS = 4
X_HEAD_DIM = D_MODEL // X_HEADS
D_FF = 2816
DEEPNORM_ALPHA = 2.0 ** 0.25
LN_EPS = 1e-5
NEG_BIG = -1e30
GATE_ROWS = 16
LOG2E = 1.4426950408889634
ATT_TILE = 256
KEY_LANES = 128
ALIBI_LANE = HEAD_DIM
ONEHOT_LANE = HEAD_DIM + 8
VT_ROWS = 80

V7X_VMEM_LIMIT_BYTES = 56 * 1024 * 1024
SSM_LANE_SLAB = 128
SSM_SLABS = SSM_WIDTH // SSM_LANE_SLAB
SSM_STATE_SLAB = SSM_STATES // SSM_SLABS


def _cparams(*sem):
    return pltpu.CompilerParams(dimension_semantics=sem, vmem_limit_bytes=V7X_VMEM_LIMIT_BYTES)


def _full(shape):
    zeros = (0,) * len(shape)
    return pl.BlockSpec(shape, lambda *_: zeros)


def _layer_norm(x, g, b):
    mu = jnp.mean(x, axis=-1, keepdims=True)
    xc = x - mu
    var = jnp.mean(xc * xc, axis=-1, keepdims=True)
    return xc * lax.rsqrt(var + LN_EPS) * g + b


def _gelu_tanh(x):
    return 0.5 * x * (1.0 + jnp.tanh(math.sqrt(2.0 / math.pi) * (x + 0.044715 * (x * x * x))))


def _sigmoid(x):
    return 1.0 / (1.0 + jnp.exp(-x))


def _dot(a, b):
    return jnp.dot(a, b, preferred_element_type=F32)


def _dot_nt(a, b):
    return lax.dot_general(a, b, (((1,), (1,)), ((), ())), preferred_element_type=F32)


def _zoh_kernel(a_re, a_im, log_dt, b_re, b_im, lb_re_o, lb_im_o, bb_re_o, bb_im_o):
    lam_re = jnp.minimum(a_re[...], SSM_EIG_CLIP)
    lam_im = a_im[...]
    dt = jnp.exp(log_dt[...])
    mag = jnp.exp(lam_re * dt)
    lb_re = mag * jnp.cos(lam_im * dt)
    lb_im = mag * jnp.sin(lam_im * dt)
    den = lam_re * lam_re + lam_im * lam_im
    nr = lb_re - 1.0
    f_re = (nr * lam_re + lb_im * lam_im) / den
    f_im = (lb_im * lam_re - nr * lam_im) / den
    br = b_re[...]
    bi = b_im[...]
    lb_re_o[...] = lb_re
    lb_im_o[...] = lb_im
    bb_re_o[...] = f_re * br - f_im * bi
    bb_im_o[...] = f_re * bi + f_im * br


def _zoh_prep(a_re, a_im, log_dt, b_re, b_im):
    gn = SSM_STATES
    col = lambda v: v.astype(F32).reshape(gn, 1)
    dt_col = jnp.broadcast_to(log_dt.astype(F32)[:, None], (SSM_GROUPS, SSM_STATE)).reshape(gn, 1)
    outs = pl.pallas_call(
        _zoh_kernel,
        out_shape=[jax.ShapeDtypeStruct((gn, 1), F32)] * 2 + [jax.ShapeDtypeStruct((gn, SSM_GROUP), F32)] * 2,
        name="zoh_prep",
    )(col(a_re), col(a_im), dt_col, b_re.astype(F32).reshape(gn, SSM_GROUP), b_im.astype(F32).reshape(gn, SSM_GROUP))
    lb_re, lb_im, bb_re, bb_im = outs
    shape_b = (SSM_GROUPS, SSM_STATE, SSM_GROUP)
    return lb_re.reshape(1, gn), lb_im.reshape(1, gn), bb_re.reshape(shape_b), bb_im.reshape(shape_b)


def _inproj_kernel(x_ref, g_ref, b_ref, wu_ref, wqt_ref, wkc_ref, wks_ref, wkw_ref, wvst_ref, wvwt_ref, wgt_ref,
                   h_o, u_o, qt_o, kvc_o, ks_o, kw_o, vst_o, vwt_o, gt_o):
    tl = x_ref.shape[1]
    h = _layer_norm(x_ref[0], g_ref[...], b_ref[...])
    h_o[0] = h
    hb = h.astype(BF16)
    u_o[0] = _dot(hb, wu_ref[...])
    qt_o[0] = (_dot_nt(wqt_ref[...], hb) * (HEAD_DIM ** -0.5 * LOG2E)).astype(BF16)
    for j in range(2 * N_KV_HEADS):
        kvc_o[0, j] = _dot(hb, wkc_ref[j])

    lane = lax.broadcasted_iota(jnp.int32, (tl, KEY_LANES), 1)
    pos = pl.program_id(1) * tl + lax.broadcasted_iota(jnp.int32, (tl, KEY_LANES), 0)
    alibi = jnp.where((lane >= ALIBI_LANE) & (lane < ALIBI_LANE + 3), (pos % ATT_TILE).astype(F32), 0.0)
    onehot = jnp.where((lane >= ONEHOT_LANE) & (lane - ONEHOT_LANE == pos // SLC_BLOCK), 1.0, 0.0)
    for j in range(N_KV_HEADS):
        ks_o[0, j] = (_dot(hb, wks_ref[j]) + (alibi + onehot)).astype(BF16)
        kw_o[0, j] = (_dot(hb, wkw_ref[j]) + alibi).astype(BF16)
    row = lax.broadcasted_iota(jnp.int32, (N_KV_HEADS * VT_ROWS, tl), 0)
    ones_row = jnp.where(row % VT_ROWS == HEAD_DIM, 1.0, 0.0)
    vst_o[0] = (_dot_nt(wvst_ref[...], hb) + ones_row).astype(BF16)
    vwt_o[0] = (_dot_nt(wvwt_ref[...], hb) + ones_row).astype(BF16)
    gt_o[0] = _sigmoid(_dot_nt(wgt_ref[...], hb))


def _ln_inproj(x, ln_g, ln_b, w, tl):
    bsz, L, d = x.shape
    n_gate = N_KV_HEADS * GATE_ROWS
    grid = (bsz, L // tl)
    in_specs = [
        pl.BlockSpec((1, tl, d), lambda b, i: (b, i, 0)),
        _full((1, d)), _full((1, d)),
        _full((d, SSM_WIDTH)),
        _full((D_MODEL, d)),
        _full((2 * N_KV_HEADS, d, HEAD_DIM)),
        _full((N_KV_HEADS, d, KEY_LANES)),
        _full((N_KV_HEADS, d, KEY_LANES)),
        _full((N_KV_HEADS * VT_ROWS, d)),
        _full((N_KV_HEADS * VT_ROWS, d)),
        _full((n_gate, d)),
    ]
    assert tl % ATT_TILE == 0 or ATT_TILE % tl == 0
    assert ONEHOT_LANE + L // SLC_BLOCK <= KEY_LANES
    out_shape = [
        jax.ShapeDtypeStruct((bsz, L, d), F32),
        jax.ShapeDtypeStruct((bsz, L, SSM_WIDTH), F32),
        jax.ShapeDtypeStruct((bsz, D_MODEL, L), BF16),
        jax.ShapeDtypeStruct((bsz, 2 * N_KV_HEADS, L, HEAD_DIM), F32),
        jax.ShapeDtypeStruct((bsz, N_KV_HEADS, L, KEY_LANES), BF16),
        jax.ShapeDtypeStruct((bsz, N_KV_HEADS, L, KEY_LANES), BF16),
        jax.ShapeDtypeStruct((bsz, N_KV_HEADS * VT_ROWS, L), BF16),
        jax.ShapeDtypeStruct((bsz, N_KV_HEADS * VT_ROWS, L), BF16),
        jax.ShapeDtypeStruct((bsz, n_gate, L), F32),
    ]
    out_specs = [
        pl.BlockSpec((1, tl, d), lambda b, i: (b, i, 0)),
        pl.BlockSpec((1, tl, SSM_WIDTH), lambda b, i: (b, i, 0)),
        pl.BlockSpec((1, D_MODEL, tl), lambda b, i: (b, 0, i)),
        pl.BlockSpec((1, 2 * N_KV_HEADS, tl, HEAD_DIM), lambda b, i: (b, 0, i, 0)),
        pl.BlockSpec((1, N_KV_HEADS, tl, KEY_LANES), lambda b, i: (b, 0, i, 0)),
        pl.BlockSpec((1, N_KV_HEADS, tl, KEY_LANES), lambda b, i: (b, 0, i, 0)),
        pl.BlockSpec((1, N_KV_HEADS * VT_ROWS, tl), lambda b, i: (b, 0, i)),
        pl.BlockSpec((1, N_KV_HEADS * VT_ROWS, tl), lambda b, i: (b, 0, i)),
        pl.BlockSpec((1, n_gate, tl), lambda b, i: (b, 0, i)),
    ]
    return pl.pallas_call(
        _inproj_kernel, grid=grid, in_specs=in_specs, out_specs=out_specs, out_shape=out_shape,
        compiler_params=_cparams("parallel", "parallel"), name="ln_inproj",
    )(x, ln_g.reshape(1, d), ln_b.reshape(1, d), w["wu"], w["wqt"], w["wkc"], w["wks"], w["wkw"],
      w["wvst"], w["wvwt"], w["wgt"])


def _s5_kernel(u_ref, lre_ref, lim_ref, wbre_ref, wbim_ref, cre_ref, cim_ref, d_ref, wglu_ref, bglu_ref, wo_ref,
               y_o, sre, sim, hre, him, *, bsz, steps, pitch):
    @pl.when(pl.program_id(0) == 0)
    def _():
        sre[...] = jnp.zeros_like(sre)
        sim[...] = jnp.zeros_like(sim)

    lanes = SSM_LANE_SLAB
    per_slab = SSM_STATE_SLAB // lanes
    u = u_ref[...].reshape(bsz * steps, SSM_WIDTH)
    ub = u.astype(BF16)
    for j in range(SSM_SLABS):
        uj = ub[:, j * lanes:(j + 1) * lanes]
        for w_ref, h_ref in ((wbre_ref, hre), (wbim_ref, him)):
            r = _dot(uj, w_ref[j])
            for b in range(bsz):
                for k in range(per_slab):
                    h_ref[j * per_slab + k, pl.ds(b, steps, stride=pitch), :] = (
                        r[b * steps:(b + 1) * steps, k * lanes:(k + 1) * lanes])

    for j in range(SSM_SLABS):
        slabs = [j * per_slab + k for k in range(per_slab)]
        lr = [jnp.broadcast_to(lre_ref[:, s * lanes:(s + 1) * lanes], (bsz, lanes)) for s in slabs]
        li = [jnp.broadcast_to(lim_ref[:, s * lanes:(s + 1) * lanes], (bsz, lanes)) for s in slabs]

        def step(t, carry, slabs=slabs, lr=lr, li=li):
            rows = pl.ds(pl.multiple_of(t * pitch, 8), bsz)
            new = []
            for k, s in enumerate(slabs):
                pr, pi = carry[2 * k], carry[2 * k + 1]
                nr = lr[k] * pr - li[k] * pi + hre[s, rows, :]
                ni = lr[k] * pi + li[k] * pr + him[s, rows, :]
                hre[s, rows, :] = nr
                him[s, rows, :] = ni
                new += [nr, ni]
            return tuple(new)

        init = []
        for s in slabs:
            init += [sre[:, s * lanes:(s + 1) * lanes], sim[:, s * lanes:(s + 1) * lanes]]
        fin = lax.fori_loop(0, steps, step, tuple(init), unroll=4)
        for k, s in enumerate(slabs):
            sre[:, s * lanes:(s + 1) * lanes] = fin[2 * k]
            sim[:, s * lanes:(s + 1) * lanes] = fin[2 * k + 1]

    def states(h_ref, j):
        return jnp.concatenate(
            [jnp.concatenate([h_ref[j * per_slab + k, pl.ds(b, steps, stride=pitch), :] for k in range(per_slab)],
                             axis=-1) for b in range(bsz)], axis=0)

    ys = []
    for j in range(SSM_SLABS):
        ys.append(_dot(states(hre, j).astype(BF16), cre_ref[j]) - _dot(states(him, j).astype(BF16), cim_ref[j]))
    y = jnp.concatenate(ys, axis=-1) + d_ref[...] * u
    g = _gelu_tanh(y)
    y2 = g * _sigmoid(_dot(g.astype(BF16), wglu_ref[...]) + bglu_ref[...])
    y_o[...] = _dot(y2.astype(BF16), wo_ref[...]).reshape(bsz, steps, D_MODEL)


def _s5(u, lb_re, lb_im, w, steps):
    bsz, L, _ = u.shape
    pitch = -(-bsz // 8) * 8
    pitch += 8 if (pitch // 8) % 2 == 0 else 0
    grid = (L // steps,)
    kern = functools.partial(_s5_kernel, bsz=bsz, steps=steps, pitch=pitch)
    n_slabs = SSM_STATES // SSM_LANE_SLAB
    in_specs = [
        pl.BlockSpec((bsz, steps, SSM_WIDTH), lambda c: (0, c, 0)),
        _full((1, SSM_STATES)), _full((1, SSM_STATES)),
        _full((SSM_SLABS, SSM_LANE_SLAB, SSM_STATE_SLAB)), _full((SSM_SLABS, SSM_LANE_SLAB, SSM_STATE_SLAB)),
        _full((SSM_SLABS, SSM_STATE_SLAB, SSM_LANE_SLAB)), _full((SSM_SLABS, SSM_STATE_SLAB, SSM_LANE_SLAB)),
        _full((1, SSM_WIDTH)),
        _full((SSM_WIDTH, SSM_WIDTH)), _full((1, SSM_WIDTH)),
        _full((SSM_WIDTH, D_MODEL)),
    ]
    return pl.pallas_call(
        kern, grid=grid, in_specs=in_specs,
        out_specs=pl.BlockSpec((bsz, steps, D_MODEL), lambda c: (0, c, 0)),
        out_shape=jax.ShapeDtypeStruct((bsz, L, D_MODEL), F32),
        scratch_shapes=[pltpu.VMEM((bsz, SSM_STATES), F32), pltpu.VMEM((bsz, SSM_STATES), F32),
                        pltpu.VMEM((n_slabs, steps * pitch, SSM_LANE_SLAB), F32),
                        pltpu.VMEM((n_slabs, steps * pitch, SSM_LANE_SLAB), F32)],
        compiler_params=_cparams("arbitrary"), name="s5",
    )(u, lb_re, lb_im, w["wbre"], w["wbim"], w["cre"], w["cim"], w["d"], w["wglu"], w["bglu"], w["wo"])


def _compress_kernel(kv_ref, pos_ref, w1_ref, b1_ref, w2k_ref, w2vt_ref, kc_o, vct_o, *, n_chunks):
    half = CMP_BLOCK // 2
    for z in range(2):
        for hh in range(N_KV_HEADS):
            j = z * N_KV_HEADS + hh
            first = jnp.zeros((n_chunks, CMP_HIDDEN), F32)
            second = jnp.zeros((n_chunks, CMP_HIDDEN), F32)
            for p in range(half):
                xp = kv_ref[0, j, pl.ds(p, n_chunks, stride=CMP_STRIDE), :]
                first += _dot((xp + pos_ref[z, p:p + 1, :]).astype(BF16), w1_ref[z, p * HEAD_DIM:(p + 1) * HEAD_DIM, :])
                q = half + p
                second += _dot((xp + pos_ref[z, q:q + 1, :]).astype(BF16), w1_ref[z, q * HEAD_DIM:(q + 1) * HEAD_DIM, :])
            pre = first + pltpu.roll(second, n_chunks - 1, 0) + b1_ref[z]
            hid = _gelu_tanh(pre).astype(BF16)
            if z == 0:
                kc_o[0, hh] = _dot(hid, w2k_ref[...]).astype(BF16)
            else:
                vct_o[0, hh * HEAD_DIM:(hh + 1) * HEAD_DIM, :] = _dot_nt(w2vt_ref[...], hid).astype(BF16)


def _compress(kvc, pos, w1, b1, w2k, w2vt):
    bsz, _, L, _ = kvc.shape
    n_chunks = L // CMP_STRIDE
    kern = functools.partial(_compress_kernel, n_chunks=n_chunks)
    return pl.pallas_call(
        kern, grid=(bsz,),
        in_specs=[
            pl.BlockSpec((1, 2 * N_KV_HEADS, L, HEAD_DIM), lambda b: (b, 0, 0, 0)),
            _full((2, CMP_BLOCK, HEAD_DIM)),
            _full((2, CMP_BLOCK * HEAD_DIM, CMP_HIDDEN)),
            _full((2, 1, CMP_HIDDEN)),
            _full((CMP_HIDDEN, HEAD_DIM)),
            _full((HEAD_DIM, CMP_HIDDEN)),
        ],
        out_specs=[
            pl.BlockSpec((1, N_KV_HEADS, n_chunks, HEAD_DIM), lambda b: (b, 0, 0, 0)),
            pl.BlockSpec((1, KV_DIM, n_chunks), lambda b: (b, 0, 0)),
        ],
        out_shape=[
            jax.ShapeDtypeStruct((bsz, N_KV_HEADS, n_chunks, HEAD_DIM), BF16),
            jax.ShapeDtypeStruct((bsz, KV_DIM, n_chunks), BF16),
        ],
        compiler_params=_cparams("parallel"), name="compress",
    )(kvc, pos, w1, b1, w2k, w2vt)


def _nsa_kernel(qt_ref, kc_ref, vct_ref, ks_ref, vst_ref, kw_ref, vwt_ref, gt_ref, ov_ref, causal_ref, far_ref, o_ref,
                qa_ref, qw_ref, m_ref, acc_ref, s_ref, p_ref, mx_ref, al_ref, sc_ref, pc_ref, *, tq, n_cmp_pad, n_slc):
    hkv = pl.program_id(1)
    i = pl.program_id(2)
    t0 = i * tq
    tk = tq
    n_sel = min(SLC_TOP_N, n_slc)
    slopes = [jnp.exp2(jnp.full((1, tq), -0.5, F32) * (hkv * GQA_GROUP + g + 1).astype(F32)) * LOG2E
              for g in range(GQA_GROUP)]
    q_heads = [qt_ref[0, g * HEAD_DIM:(g + 1) * HEAD_DIM, :] for g in range(GQA_GROUP)]
    gate = lambda g, z: gt_ref[0, 3 * g + z:3 * g + z + 1, :]

    t_lane = t0 + lax.broadcasted_iota(jnp.int32, (1, tq), 1)
    row8 = lax.broadcasted_iota(jnp.int32, (ONEHOT_LANE - ALIBI_LANE, tq), 0)

    def augmented(g, tail_rows):
        hi = slopes[g].astype(BF16).astype(F32)
        mid = (slopes[g] - hi).astype(BF16).astype(F32)
        lo = slopes[g] - hi - mid
        parts = jnp.where(row8 == 0, hi, jnp.where(row8 == 1, mid, jnp.where(row8 == 2, lo, 0.0)))
        return jnp.concatenate([q_heads[g].astype(F32), parts, tail_rows], axis=0).astype(BF16)

    def attend(state, q_ref, items, first, slot0):
        starts = []
        for n, (k_ref, _, c, bias_ref, valid) in enumerate(items):
            k0 = pl.multiple_of((c if valid is None else jnp.maximum(c, 0)) * tk, tk)
            starts.append(k0)
            k = k_ref[0, 0, pl.ds(k0, tk), :]
            for g in range(GQA_GROUP):
                s = _dot(k, q_ref[g])
                if bias_ref is not None:
                    s = s + bias_ref[...]
                s_ref[slot0 + n * GQA_GROUP + g] = s
                mx_ref[slot0 + n * GQA_GROUP + g] = jnp.max(s, axis=0, keepdims=True)
        for g in range(GQA_GROUP):
            shifts = []
            for _, _, c, _, valid in items:
                shift = slopes[g] * ((c - i) * tk).astype(F32)
                shifts.append(shift if valid is None else jnp.where(valid, shift, NEG_BIG))
            m_new = functools.reduce(jnp.maximum, [mx_ref[slot0 + n * GQA_GROUP + g] + shifts[n]
                                                   for n in range(len(items))])
            if not first:
                m_new = jnp.maximum(m_ref[state, g], m_new)
                al_ref[g] = jnp.exp2(m_ref[state, g] - m_new)
            for n in range(len(items)):
                slot = slot0 + n * GQA_GROUP + g
                p_ref[slot] = jnp.exp2(s_ref[slot] - (m_new - shifts[n])).astype(BF16)
            m_ref[state, g] = m_new
        for g in range(GQA_GROUP):
            pv = None
            for n, (_, vt_ref, _, _, _) in enumerate(items):
                part = _dot(vt_ref[0, :, pl.ds(starts[n], tk)], p_ref[slot0 + n * GQA_GROUP + g])
                pv = part if pv is None else pv + part
            acc_ref[state, g] = pv if first else al_ref[g] * acc_ref[state, g] + pv

    def emit(state, z, accumulate):
        for g in range(GQA_GROUP):
            rows = slice(g * HEAD_DIM, (g + 1) * HEAD_DIM)
            out = gate(g, z) * (acc_ref[state, g, :HEAD_DIM, :] / acc_ref[state, g, HEAD_DIM:HEAD_DIM + 1, :])
            o_ref[0, rows, :] = o_ref[0, rows, :] + out if accumulate else out

    zero_rows = jnp.zeros((KEY_LANES - ONEHOT_LANE, tq), F32)
    for g in range(GQA_GROUP):
        qw_ref[g] = augmented(g, zero_rows)
    n_back = WINDOW // tk
    n_win_items = n_back + 1
    win_items = [(kw_ref, vwt_ref, i, causal_ref, None)]
    for back in range(1, n_back + 1):
        win_items.append((kw_ref, vwt_ref, i - back, far_ref if back == n_back else None, i >= back))
    attend(0, qw_ref, win_items, True, 0)
    emit(0, 2, False)

    n_idx = lax.broadcasted_iota(jnp.int32, (n_cmp_pad, tq), 0)
    dist_c = (t0 + lax.broadcasted_iota(jnp.int32, (n_cmp_pad, tq), 1)) - (n_idx * CMP_STRIDE + (CMP_BLOCK - 1))
    mask_c = dist_c >= 0
    dist_cf = dist_c.astype(F32)
    kc = kc_ref[0, 0]
    vct = vct_ref[0]
    for g in range(GQA_GROUP):
        s = _dot(kc, q_heads[g]) - slopes[g] * dist_cf
        sc_ref[g] = jnp.where(mask_c, s, NEG_BIG)
    p_sum = jnp.zeros((n_cmp_pad, tq), F32)
    for g in range(GQA_GROUP):
        s = sc_ref[g]
        p = jnp.where(mask_c, jnp.exp2(s - jnp.max(s, axis=0, keepdims=True)), 0.0)
        p = p / jnp.maximum(jnp.sum(p, axis=0, keepdims=True), 1e-30)
        pc_ref[g] = p.astype(BF16)
        p_sum = p_sum + p
    for g in range(GQA_GROUP):
        rows = slice(g * HEAD_DIM, (g + 1) * HEAD_DIM)
        o_ref[0, rows, :] = o_ref[0, rows, :] + gate(g, 0) * _dot(vct, pc_ref[g])

    imp = lax.dot_general(ov_ref[...], p_sum, (((1,), (0,)), ((), ())), precision=lax.Precision.HIGHEST,
                          preferred_element_type=F32)
    blk = lax.broadcasted_iota(jnp.int32, (n_slc, tq), 0)
    cur = t_lane // SLC_BLOCK
    forced = (blk == 0) | (blk == cur) | (blk == cur - 1)
    future = blk * SLC_BLOCK > t_lane
    imp = jnp.where(forced, SEL_BIG, jnp.where(future, -SEL_BIG, imp))
    rank = jnp.zeros((n_slc, tq), jnp.int32)
    for j in range(n_slc):
        row = imp[j:j + 1, :]
        beats = (row > imp) | ((row == imp) & (blk > j))
        rank = rank + beats.astype(jnp.int32)
    sel_bias = jnp.where(rank < n_sel, 0.0, NEG_BIG)

    qa_rows = jnp.concatenate([sel_bias, jnp.zeros((KEY_LANES - ONEHOT_LANE - n_slc, tq), F32)], axis=0)
    for g in range(GQA_GROUP):
        qa_ref[g] = augmented(g, qa_rows)

    attend(1, qa_ref, [(ks_ref, vst_ref, i, causal_ref, None)], True, n_win_items * GQA_GROUP)

    def slc_step(pair, carry):
        c0 = 2 * pair
        attend(1, qa_ref, [(ks_ref, vst_ref, c0, None, None), (ks_ref, vst_ref, c0 + 1, None, c0 + 1 < i)], False, 0)
        return carry

    lax.fori_loop(0, (i + 1) // 2, slc_step, 0)
    emit(1, 1, True)


def _nsa_attn(qt, kc, vct, ks, vst, kw, vwt, gt, tq):
    bsz, _, L = qt.shape
    n_cmp_pad = kc.shape[2]
    n_slc = L // SLC_BLOCK
    group_rows = GQA_GROUP * HEAD_DIM
    n = jnp.arange(n_cmp_pad)[None, :]
    j = jnp.arange(n_slc)[:, None]
    overlap_t = ((n * CMP_STRIDE < (j + 1) * SLC_BLOCK) & (n * CMP_STRIDE + CMP_BLOCK - 1 >= j * SLC_BLOCK)).astype(F32)
    assert tq == ATT_TILE and WINDOW % tq == 0 and n_slc % 8 == 0
    n_slots = (WINDOW // tq + 2) * GQA_GROUP
    sub = jnp.arange(tq)[:, None]
    lane = jnp.arange(tq)[None, :]
    causal_bias = jnp.where(sub <= lane, 0.0, NEG_BIG).astype(F32)
    far_bias = jnp.where(sub > lane, 0.0, NEG_BIG).astype(F32)
    kern = functools.partial(_nsa_kernel, tq=tq, n_cmp_pad=n_cmp_pad, n_slc=n_slc)
    return pl.pallas_call(
        kern, grid=(bsz, N_KV_HEADS, L // tq),
        in_specs=[
            pl.BlockSpec((1, group_rows, tq), lambda b, h, i: (b, h, i)),
            pl.BlockSpec((1, 1, n_cmp_pad, HEAD_DIM), lambda b, h, i: (b, h, 0, 0)),
            pl.BlockSpec((1, HEAD_DIM, n_cmp_pad), lambda b, h, i: (b, h, 0)),
            pl.BlockSpec((1, 1, L, KEY_LANES), lambda b, h, i: (b, h, 0, 0)),
            pl.BlockSpec((1, VT_ROWS, L), lambda b, h, i: (b, h, 0)),
            pl.BlockSpec((1, 1, L, KEY_LANES), lambda b, h, i: (b, h, 0, 0)),
            pl.BlockSpec((1, VT_ROWS, L), lambda b, h, i: (b, h, 0)),
            pl.BlockSpec((1, GATE_ROWS, tq), lambda b, h, i: (b, h, i)),
            _full((n_slc, n_cmp_pad)),
            _full((tq, tq)), _full((tq, tq)),
        ],
        out_specs=pl.BlockSpec((1, group_rows, tq), lambda b, h, i: (b, h, i)),
        out_shape=jax.ShapeDtypeStruct((bsz, D_MODEL, L), F32),
        scratch_shapes=[pltpu.VMEM((GQA_GROUP, KEY_LANES, tq), BF16),
                        pltpu.VMEM((GQA_GROUP, KEY_LANES, tq), BF16),
                        pltpu.VMEM((2, GQA_GROUP, 1, tq), F32),
                        pltpu.VMEM((2, GQA_GROUP, VT_ROWS, tq), F32),
                        pltpu.VMEM((n_slots, tq, tq), F32),
                        pltpu.VMEM((n_slots, tq, tq), BF16),
                        pltpu.VMEM((n_slots, 1, tq), F32),
                        pltpu.VMEM((GQA_GROUP, 1, tq), F32),
                        pltpu.VMEM((GQA_GROUP, n_cmp_pad, tq), F32),
                        pltpu.VMEM((GQA_GROUP, n_cmp_pad, tq), BF16)],
        compiler_params=_cparams("parallel", "parallel", "parallel"), name="nsa_attn",
    )(qt, kc, vct, ks, vst, kw, vwt, gt, overlap_t, causal_bias, far_bias)


def _mix_kernel(h_ref, ys_ref, ot_ref, wgm_ref, wno_ref, wout_ref, g_ref, b_ref, h1_o):
    h = h_ref[0]
    o = ot_ref[0].T.astype(BF16)
    y_nsa = _dot(o, wno_ref[...])
    gm = _sigmoid(_dot(h.astype(BF16), wgm_ref[...]))
    mix = gm[:, :D_MODEL] * ys_ref[0] + gm[:, D_MODEL:] * y_nsa
    mixed = _dot(mix.astype(BF16), wout_ref[...])
    h1_o[0] = _layer_norm(DEEPNORM_ALPHA * h + mixed, g_ref[...], b_ref[...])


def _mix_ln1(h, y_ssm, o_t, wgm, wno, wout, g, b, tm):
    bsz, L, d = h.shape
    return pl.pallas_call(
        _mix_kernel, grid=(bsz, L // tm),
        in_specs=[
            pl.BlockSpec((1, tm, d), lambda b, i: (b, i, 0)),
            pl.BlockSpec((1, tm, d), lambda b, i: (b, i, 0)),
            pl.BlockSpec((1, d, tm), lambda b, i: (b, 0, i)),
            _full((d, 2 * d)), _full((d, d)), _full((d, d)), _full((1, d)), _full((1, d)),
        ],
        out_specs=pl.BlockSpec((1, tm, d), lambda b, i: (b, i, 0)),
        out_shape=jax.ShapeDtypeStruct((bsz, L, d), F32),
        compiler_params=_cparams("parallel", "parallel"), name="mix_ln1",
    )(h, y_ssm, o_t, wgm, wno, wout, g.reshape(1, d), b.reshape(1, d))


def _memkv_kernel(mem_ref, w_ref, k_o, v_o):
    kv = _dot(mem_ref[0].astype(BF16), w_ref[...])
    k_o[0] = kv[:, :D_MODEL].astype(BF16)
    v_o[0] = kv[:, D_MODEL:].astype(BF16)


def _mem_kv(mem, w_kv):
    bsz, m, d = mem.shape
    spec = pl.BlockSpec((1, m, d), lambda b: (b, 0, 0))
    return pl.pallas_call(
        _memkv_kernel, grid=(bsz,),
        in_specs=[spec, _full((d, 2 * d))],
        out_specs=[spec, spec],
        out_shape=[jax.ShapeDtypeStruct((bsz, m, d), BF16)] * 2,
        compiler_params=_cparams("parallel"), name="mem_kv",
    )(mem, w_kv)


def _xattn_kernel(h_ref, k_ref, v_ref, wq_ref, wo_ref, g_ref, b_ref, h2_o):
    h = h_ref[0]
    q = (_dot(h.astype(BF16), wq_ref[...]) * (X_HEAD_DIM ** -0.5)).astype(BF16)
    outs = []
    for hd in range(X_HEADS):
        cols = slice(hd * X_HEAD_DIM, (hd + 1) * X_HEAD_DIM)
        s = _dot_nt(q[:, cols], k_ref[0, :, cols])
        p = jnp.exp(s - jnp.max(s, axis=-1, keepdims=True))
        p = p / jnp.sum(p, axis=-1, keepdims=True)
        outs.append(_dot(p.astype(BF16), v_ref[0, :, cols]))
    o = jnp.concatenate(outs, axis=-1).astype(BF16)
    h2_o[0] = _layer_norm(DEEPNORM_ALPHA * h + _dot(o, wo_ref[...]), g_ref[...], b_ref[...])


def _xattn_ln2(h, k, v, wq, wo, g, b, tm):
    bsz, L, d = h.shape
    m = k.shape[1]
    row = pl.BlockSpec((1, tm, d), lambda b, i: (b, i, 0))
    kv = pl.BlockSpec((1, m, d), lambda b, i: (b, 0, 0))
    return pl.pallas_call(
        _xattn_kernel, grid=(bsz, L // tm),
        in_specs=[row, kv, kv, _full((d, d)), _full((d, d)), _full((1, d)), _full((1, d))],
        out_specs=row,
        out_shape=jax.ShapeDtypeStruct((bsz, L, d), F32),
        compiler_params=_cparams("parallel", "parallel"), name="xattn_ln2",
    )(h, k, v, wq, wo, g.reshape(1, d), b.reshape(1, d))


def _ffn_kernel(h_ref, win_ref, wout_ref, g_ref, b_ref, o_ref):
    h = h_ref[...]
    gu = _dot(h.astype(BF16), win_ref[...])
    gate = gu[:, :D_FF]
    act = gate * _sigmoid(gate) * gu[:, D_FF:]
    o_ref[...] = _layer_norm(DEEPNORM_ALPHA * h + _dot(act.astype(BF16), wout_ref[...]), g_ref[...], b_ref[...])


def _ffn_ln3(h, win, wout, g, b, tm):
    rows, d = h.shape
    row = pl.BlockSpec((tm, d), lambda i: (i, 0))
    return pl.pallas_call(
        _ffn_kernel, grid=(rows // tm,),
        in_specs=[row, _full((d, 2 * D_FF)), _full((D_FF, d)), _full((1, d)), _full((1, d))],
        out_specs=row,
        out_shape=jax.ShapeDtypeStruct((rows, d), F32),
        compiler_params=_cparams("parallel"), name="ffn_ln3",
    )(h, win, wout, g.reshape(1, d), b.reshape(1, d))


def _inproj_weights(w_in):
    d = D_MODEL
    c0 = SSM_WIDTH
    c1 = c0 + N_HEADS * HEAD_DIM
    c2 = c1 + 2 * KV_DIM
    c3 = c2 + 2 * KV_DIM
    c4 = c3 + 2 * KV_DIM
    c5 = c4 + 3 * N_HEADS
    per_head = lambda m: m.reshape(d, -1, HEAD_DIM).transpose(1, 0, 2)
    keys = lambda m: jnp.pad(per_head(m), ((0, 0), (0, 0), (0, KEY_LANES - HEAD_DIM)))
    values_t = lambda m: jnp.pad(m.T.reshape(N_KV_HEADS, HEAD_DIM, d),
                                 ((0, 0), (0, VT_ROWS - HEAD_DIM), (0, 0))).reshape(N_KV_HEADS * VT_ROWS, d)
    wg = w_in[:, c4:c5].T.reshape(N_KV_HEADS, 3 * GQA_GROUP, d)
    wg = jnp.pad(wg, ((0, 0), (0, GATE_ROWS - 3 * GQA_GROUP), (0, 0))).reshape(N_KV_HEADS * GATE_ROWS, d)
    return {
        "wu": w_in[:, :c0].astype(BF16),
        "wqt": w_in[:, c0:c1].T.astype(BF16),
        "wkc": per_head(w_in[:, c1:c2]).astype(BF16),
        "wks": keys(w_in[:, c2:c2 + KV_DIM]).astype(BF16),
        "wvst": values_t(w_in[:, c2 + KV_DIM:c3]).astype(BF16),
        "wkw": keys(w_in[:, c3:c3 + KV_DIM]).astype(BF16),
        "wvwt": values_t(w_in[:, c3 + KV_DIM:c4]).astype(BF16),
        "wgt": wg.astype(BF16),
        "wgm": w_in[:, c5:].astype(BF16),
    }


def _s5_weights(bb_re, bb_im, c_re, c_im, d_skip, w_glu, b_glu, w_o):
    per_slab = SSM_LANE_SLAB // SSM_GROUP
    eye = jnp.eye(per_slab, dtype=F32)

    def b_blocks(bb):
        bb = bb.reshape(SSM_SLABS, per_slab, SSM_STATE, SSM_GROUP)
        return jnp.einsum('jgnc,gh->jgchn', bb, eye).reshape(SSM_SLABS, SSM_LANE_SLAB, SSM_STATE_SLAB).astype(BF16)

    def c_blocks(c):
        c = c.astype(F32).reshape(SSM_SLABS, per_slab, SSM_GROUP, SSM_STATE)
        return jnp.einsum('jgcn,gh->jgnhc', c, eye).reshape(SSM_SLABS, SSM_STATE_SLAB, SSM_LANE_SLAB).astype(BF16)

    return {
        "wbre": b_blocks(bb_re), "wbim": b_blocks(bb_im), "cre": c_blocks(c_re), "cim": c_blocks(c_im),
        "d": d_skip.astype(F32).reshape(1, SSM_WIDTH), "wglu": w_glu.astype(BF16),
        "bglu": b_glu.astype(F32).reshape(1, SSM_WIDTH), "wo": w_o.astype(BF16),
    }


def _pick(total, want):
    return want if total % want == 0 else total


def kernel(x, mem, ln_emb_g, ln_emb_b, w_in, ssm_a_re, ssm_a_im, ssm_b_re, ssm_b_im, ssm_c_re, ssm_c_im, ssm_d,
           ssm_log_dt, ssm_w_glu, ssm_b_glu, ssm_w_out, cmp_pos, cmp_w1, cmp_b1, cmp_w2, nsa_w_out, w_out,
           ln1_g, ln1_b, xattn_w_q, xattn_w_kv, xattn_w_o, ln2_g, ln2_b, ffn_w_in, ffn_w_out, ln3_g, ln3_b):
    bsz, L, d = x.shape
    assert w_in.shape[0] == 1, "one layer: the trunk-entry LayerNorm is fused into its input projection"
    l = 0
    tile = _pick(L, 256)
    wi = _inproj_weights(w_in[l])
    h, u, qt, kvc, ks, kw, vst, vwt, gt = _ln_inproj(x, ln_emb_g, ln_emb_b, wi, tile)

    lb_re, lb_im, bb_re, bb_im = _zoh_prep(ssm_a_re[l], ssm_a_im[l], ssm_log_dt[l], ssm_b_re[l], ssm_b_im[l])
    ws = _s5_weights(bb_re, bb_im, ssm_c_re[l], ssm_c_im[l], ssm_d[l], ssm_w_glu[l], ssm_b_glu[l], ssm_w_out[l])
    y_ssm = _s5(u, lb_re, lb_im, ws, _pick(L, 32))

    kc, vct = _compress(kvc, cmp_pos[l].astype(F32), cmp_w1[l].astype(BF16),
                        cmp_b1[l].astype(F32).reshape(2, 1, CMP_HIDDEN),
                        cmp_w2[l, 0].astype(BF16), cmp_w2[l, 1].T.astype(BF16))
    o_t = _nsa_attn(qt, kc, vct, ks, vst, kw, vwt, gt, tile)

    h = _mix_ln1(h, y_ssm, o_t, wi["wgm"], nsa_w_out[l].astype(BF16), w_out[l].astype(BF16),
                 ln1_g[l], ln1_b[l], tile)
    mk, mv = _mem_kv(mem, xattn_w_kv[l].astype(BF16))
    h = _xattn_ln2(h, mk, mv, xattn_w_q[l].astype(BF16), xattn_w_o[l].astype(BF16), ln2_g[l], ln2_b[l], tile)
    h = _ffn_ln3(h.reshape(bsz * L, d), ffn_w_in[l].astype(BF16), ffn_w_out[l].astype(BF16),
                 ln3_g[l], ln3_b[l], tile)
    return h.reshape(bsz, L, d)
```

```python
import functools
import math

import jax
import jax.numpy as jnp
from jax import lax
from jax.experimental import pallas as pl
from jax.experimental.pallas import tpu as pltpu

F32 = jnp.float32
BF16 = jnp.bfloat16

D_MODEL = 1024
SSM_WIDTH = 512
SSM_GROUP = 16
SSM_GROUPS = SSM_WIDTH // SSM_GROUP
SSM_STATE = 64
SSM_STATES = SSM_GROUPS * SSM_STATE
SSM_EIG_CLIP = -1e-4
N_HEADS = 16
N_KV_HEADS = 4
HEAD_DIM = 64
GQA_GROUP = N_HEADS // N_KV_HEADS
KV_DIM = N_KV_HEADS * HEAD_DIM
CMP_BLOCK = 32
CMP_STRIDE = 16
CMP_HIDDEN = 256
SLC_BLOCK = 64
SLC_TOP_N = 8
WINDOW = 512
SEL_BIG = 1e9
X_HEADS = 4
X_HEAD_DIM = D_MODEL // X_HEADS
D_FF = 2816
DEEPNORM_ALPHA = 2.0 ** 0.25
LN_EPS = 1e-5
NEG_BIG = -1e30
GATE_ROWS = 16
LOG2E = 1.4426950408889634
ATT_TILE = 256
KEY_LANES = 128
ALIBI_LANE = HEAD_DIM
ONEHOT_LANE = HEAD_DIM + 8
VT_ROWS = 80
SLC_LEAD_CHUNKS = 3

V7X_VMEM_LIMIT_BYTES = 56 * 1024 * 1024
SSM_LANE_SLAB = 128
SSM_SLABS = SSM_WIDTH // SSM_LANE_SLAB
SSM_STATE_SLAB = SSM_STATES // SSM_SLABS


def _cparams(*sem):
    return pltpu.CompilerParams(dimension_semantics=sem, vmem_limit_bytes=V7X_VMEM_LIMIT_BYTES)


def _full(shape):
    zeros = (0,) * len(shape)
    return pl.BlockSpec(shape, lambda *_: zeros)


def _layer_norm(x, g, b):
    mu = jnp.mean(x, axis=-1, keepdims=True)
    xc = x - mu
    var = jnp.mean(xc * xc, axis=-1, keepdims=True)
    return xc * lax.rsqrt(var + LN_EPS) * g + b


def _gelu_tanh(x):
    return 0.5 * x * (1.0 + jnp.tanh(math.sqrt(2.0 / math.pi) * (x + 0.044715 * (x * x * x))))


def _sigmoid(x):
    return 1.0 / (1.0 + jnp.exp(-x))


def _dot(a, b):
    return jnp.dot(a, b, preferred_element_type=F32)


def _dot_nt(a, b):
    return lax.dot_general(a, b, (((1,), (1,)), ((), ())), preferred_element_type=F32)


def _zoh_kernel(a_re, a_im, log_dt, b_re, b_im, lb_re_o, lb_im_o, bb_re_o, bb_im_o):
    lam_re = jnp.minimum(a_re[...], SSM_EIG_CLIP)
    lam_im = a_im[...]
    dt = jnp.exp(log_dt[...])
    mag = jnp.exp(lam_re * dt)
    lb_re = mag * jnp.cos(lam_im * dt)
    lb_im = mag * jnp.sin(lam_im * dt)
    den = lam_re * lam_re + lam_im * lam_im
    nr = lb_re - 1.0
    f_re = (nr * lam_re + lb_im * lam_im) / den
    f_im = (lb_im * lam_re - nr * lam_im) / den
    br = b_re[...]
    bi = b_im[...]
    lb_re_o[...] = lb_re
    lb_im_o[...] = lb_im
    bb_re_o[...] = f_re * br - f_im * bi
    bb_im_o[...] = f_re * bi + f_im * br


def _zoh_prep(a_re, a_im, log_dt, b_re, b_im):
    gn = SSM_STATES
    col = lambda v: v.astype(F32).reshape(gn, 1)
    dt_col = jnp.broadcast_to(log_dt.astype(F32)[:, None], (SSM_GROUPS, SSM_STATE)).reshape(gn, 1)
    outs = pl.pallas_call(
        _zoh_kernel,
        out_shape=[jax.ShapeDtypeStruct((gn, 1), F32)] * 2 + [jax.ShapeDtypeStruct((gn, SSM_GROUP), F32)] * 2,
        name="zoh_prep",
    )(col(a_re), col(a_im), dt_col, b_re.astype(F32).reshape(gn, SSM_GROUP), b_im.astype(F32).reshape(gn, SSM_GROUP))
    lb_re, lb_im, bb_re, bb_im = outs
    shape_b = (SSM_GROUPS, SSM_STATE, SSM_GROUP)
    return lb_re.reshape(1, gn), lb_im.reshape(1, gn), bb_re.reshape(shape_b), bb_im.reshape(shape_b)


def _inproj_kernel(x_ref, g_ref, b_ref, wu_ref, wqt_ref, wk_ref, wvst_ref, wvwt_ref, wgt_ref,
                   h_o, u_o, qt_o, kvc_o, ks_o, kw_o, vst_o, vwt_o, gt_o):
    tl = x_ref.shape[1]
    h = _layer_norm(x_ref[0], g_ref[...], b_ref[...])
    h_o[0] = h
    hb = h.astype(BF16)
    u_o[0] = _dot(hb, wu_ref[...])
    qt_o[0] = (_dot_nt(wqt_ref[...], hb) * (HEAD_DIM ** -0.5 * LOG2E)).astype(BF16)

    k_all = _dot(hb, wk_ref[...])

    def head_tile(j):
        tile = k_all[:, (j // 2) * KEY_LANES:(j // 2 + 1) * KEY_LANES]
        return pltpu.roll(tile, HEAD_DIM, 1) if j % 2 else tile

    for j in range(2 * N_KV_HEADS):
        kvc_o[0, j] = head_tile(j)[:, :HEAD_DIM]

    lane = lax.broadcasted_iota(jnp.int32, (tl, KEY_LANES), 1)
    pos = pl.program_id(1) * tl + lax.broadcasted_iota(jnp.int32, (tl, KEY_LANES), 0)
    alibi = jnp.where((lane >= ALIBI_LANE) & (lane < ALIBI_LANE + 3), (pos % ATT_TILE).astype(F32), 0.0)
    slc_feat = alibi + jnp.where((lane >= ONEHOT_LANE) & (lane - ONEHOT_LANE == pos // SLC_BLOCK), 1.0, 0.0)
    is_key = lane < HEAD_DIM
    for j in range(N_KV_HEADS):
        ks_o[0, j] = jnp.where(is_key, head_tile(2 * N_KV_HEADS + j), slc_feat).astype(BF16)
        kw_o[0, j] = jnp.where(is_key, head_tile(3 * N_KV_HEADS + j), alibi).astype(BF16)
    row = lax.broadcasted_iota(jnp.int32, (N_KV_HEADS * VT_ROWS, tl), 0)
    ones_row = jnp.where(row % VT_ROWS == HEAD_DIM, 1.0, 0.0)
    vst_o[0] = (_dot_nt(wvst_ref[...], hb) + ones_row).astype(BF16)
    vwt_o[0] = (_dot_nt(wvwt_ref[...], hb) + ones_row).astype(BF16)
    gt_o[0] = _sigmoid(_dot_nt(wgt_ref[...], hb))


def _ln_inproj(x, ln_g, ln_b, w, tl):
    bsz, L, d = x.shape
    n_gate = N_KV_HEADS * GATE_ROWS
    grid = (bsz, L // tl)
    in_specs = [
        pl.BlockSpec((1, tl, d), lambda b, i: (b, i, 0)),
        _full((1, d)), _full((1, d)),
        _full((d, SSM_WIDTH)),
        _full((D_MODEL, d)),
        _full((d, 4 * KV_DIM)),
        _full((N_KV_HEADS * VT_ROWS, d)),
        _full((N_KV_HEADS * VT_ROWS, d)),
        _full((n_gate, d)),
    ]
    assert tl % ATT_TILE == 0 or ATT_TILE % tl == 0
    assert ONEHOT_LANE + L // SLC_BLOCK <= KEY_LANES
    out_shape = [
        jax.ShapeDtypeStruct((bsz, L, d), F32),
        jax.ShapeDtypeStruct((bsz, L, SSM_WIDTH), F32),
        jax.ShapeDtypeStruct((bsz, D_MODEL, L), BF16),
        jax.ShapeDtypeStruct((bsz, 2 * N_KV_HEADS, L, HEAD_DIM), F32),
        jax.ShapeDtypeStruct((bsz, N_KV_HEADS, L, KEY_LANES), BF16),
        jax.ShapeDtypeStruct((bsz, N_KV_HEADS, L, KEY_LANES), BF16),
        jax.ShapeDtypeStruct((bsz, N_KV_HEADS * VT_ROWS, L), BF16),
        jax.ShapeDtypeStruct((bsz, N_KV_HEADS * VT_ROWS, L), BF16),
        jax.ShapeDtypeStruct((bsz, n_gate, L), F32),
    ]
    out_specs = [
        pl.BlockSpec((1, tl, d), lambda b, i: (b, i, 0)),
        pl.BlockSpec((1, tl, SSM_WIDTH), lambda b, i: (b, i, 0)),
        pl.BlockSpec((1, D_MODEL, tl), lambda b, i: (b, 0, i)),
        pl.BlockSpec((1, 2 * N_KV_HEADS, tl, HEAD_DIM), lambda b, i: (b, 0, i, 0)),
        pl.BlockSpec((1, N_KV_HEADS, tl, KEY_LANES), lambda b, i: (b, 0, i, 0)),
        pl.BlockSpec((1, N_KV_HEADS, tl, KEY_LANES), lambda b, i: (b, 0, i, 0)),
        pl.BlockSpec((1, N_KV_HEADS * VT_ROWS, tl), lambda b, i: (b, 0, i)),
        pl.BlockSpec((1, N_KV_HEADS * VT_ROWS, tl), lambda b, i: (b, 0, i)),
        pl.BlockSpec((1, n_gate, tl), lambda b, i: (b, 0, i)),
    ]
    return pl.pallas_call(
        _inproj_kernel, grid=grid, in_specs=in_specs, out_specs=out_specs, out_shape=out_shape,
        compiler_params=_cparams("parallel", "parallel"), name="ln_inproj",
    )(x, ln_g.reshape(1, d), ln_b.reshape(1, d), w["wu"], w["wqt"], w["wk"],
      w["wvst"], w["wvwt"], w["wgt"])


def _s5_kernel(u_ref, lre_ref, lim_ref, wbre_ref, wbim_ref, cre_ref, cim_ref, d_ref, wglu_ref, bglu_ref, wo_ref,
               y_o, sre, sim, hre, him, *, bsz, steps, pitch):
    @pl.when(pl.program_id(0) == 0)
    def _():
        sre[...] = jnp.zeros_like(sre)
        sim[...] = jnp.zeros_like(sim)

    lanes = SSM_LANE_SLAB
    per_slab = SSM_STATE_SLAB // lanes
    u = u_ref[...].reshape(bsz * steps, SSM_WIDTH)
    ub = u.astype(BF16)
    for j in range(SSM_SLABS):
        uj = ub[:, j * lanes:(j + 1) * lanes]
        for w_ref, h_ref in ((wbre_ref, hre), (wbim_ref, him)):
            r = _dot(uj, w_ref[j])
            for b in range(bsz):
                for k in range(per_slab):
                    h_ref[j * per_slab + k, pl.ds(b, steps, stride=pitch), :] = (
                        r[b * steps:(b + 1) * steps, k * lanes:(k + 1) * lanes])

    for j in range(SSM_SLABS):
        slabs = [j * per_slab + k for k in range(per_slab)]
        lr = [jnp.broadcast_to(lre_ref[:, s * lanes:(s + 1) * lanes], (bsz, lanes)) for s in slabs]
        li = [jnp.broadcast_to(lim_ref[:, s * lanes:(s + 1) * lanes], (bsz, lanes)) for s in slabs]

        def step(t, carry, slabs=slabs, lr=lr, li=li):
            rows = pl.ds(pl.multiple_of(t * pitch, 8), bsz)
            new = []
            for k, s in enumerate(slabs):
                pr, pi = carry[2 * k], carry[2 * k + 1]
                nr = lr[k] * pr - li[k] * pi + hre[s, rows, :]
                ni = lr[k] * pi + li[k] * pr + him[s, rows, :]
                hre[s, rows, :] = nr
                him[s, rows, :] = ni
                new += [nr, ni]
            return tuple(new)

        init = []
        for s in slabs:
            init += [sre[:, s * lanes:(s + 1) * lanes], sim[:, s * lanes:(s + 1) * lanes]]
        fin = lax.fori_loop(0, steps, step, tuple(init), unroll=4)
        for k, s in enumerate(slabs):
            sre[:, s * lanes:(s + 1) * lanes] = fin[2 * k]
            sim[:, s * lanes:(s + 1) * lanes] = fin[2 * k + 1]

    def states(h_ref, j):
        return jnp.concatenate(
            [jnp.concatenate([h_ref[j * per_slab + k, pl.ds(b, steps, stride=pitch), :] for k in range(per_slab)],
                             axis=-1) for b in range(bsz)], axis=0)

    ys = []
    for j in range(SSM_SLABS):
        ys.append(_dot(states(hre, j).astype(BF16), cre_ref[j]) - _dot(states(him, j).astype(BF16), cim_ref[j]))
    y = jnp.concatenate(ys, axis=-1) + d_ref[...] * u
    g = _gelu_tanh(y)
    y2 = g * _sigmoid(_dot(g.astype(BF16), wglu_ref[...]) + bglu_ref[...])
    y_o[...] = _dot(y2.astype(BF16), wo_ref[...]).reshape(bsz, steps, D_MODEL)


def _s5(u, lb_re, lb_im, w, steps):
    bsz, L, _ = u.shape
    pitch = -(-bsz // 8) * 8
    pitch += 8 if (pitch // 8) % 2 == 0 else 0
    grid = (L // steps,)
    kern = functools.partial(_s5_kernel, bsz=bsz, steps=steps, pitch=pitch)
    n_slabs = SSM_STATES // SSM_LANE_SLAB
    in_specs = [
        pl.BlockSpec((bsz, steps, SSM_WIDTH), lambda c: (0, c, 0)),
        _full((1, SSM_STATES)), _full((1, SSM_STATES)),
        _full((SSM_SLABS, SSM_LANE_SLAB, SSM_STATE_SLAB)), _full((SSM_SLABS, SSM_LANE_SLAB, SSM_STATE_SLAB)),
        _full((SSM_SLABS, SSM_STATE_SLAB, SSM_LANE_SLAB)), _full((SSM_SLABS, SSM_STATE_SLAB, SSM_LANE_SLAB)),
        _full((1, SSM_WIDTH)),
        _full((SSM_WIDTH, SSM_WIDTH)), _full((1, SSM_WIDTH)),
        _full((SSM_WIDTH, D_MODEL)),
    ]
    return pl.pallas_call(
        kern, grid=grid, in_specs=in_specs,
        out_specs=pl.BlockSpec((bsz, steps, D_MODEL), lambda c: (0, c, 0)),
        out_shape=jax.ShapeDtypeStruct((bsz, L, D_MODEL), F32),
        scratch_shapes=[pltpu.VMEM((bsz, SSM_STATES), F32), pltpu.VMEM((bsz, SSM_STATES), F32),
                        pltpu.VMEM((n_slabs, steps * pitch, SSM_LANE_SLAB), F32),
                        pltpu.VMEM((n_slabs, steps * pitch, SSM_LANE_SLAB), F32)],
        compiler_params=_cparams("arbitrary"), name="s5",
    )(u, lb_re, lb_im, w["wbre"], w["wbim"], w["cre"], w["cim"], w["d"], w["wglu"], w["bglu"], w["wo"])


def _compress_kernel(kv_ref, pos_ref, w1_ref, b1_ref, w2k_ref, w2vt_ref, kc_o, vct_o, *, n_chunks):
    half = CMP_BLOCK // 2
    for z in range(2):
        for hh in range(N_KV_HEADS):
            j = z * N_KV_HEADS + hh
            first = jnp.zeros((n_chunks, CMP_HIDDEN), F32)
            second = jnp.zeros((n_chunks, CMP_HIDDEN), F32)
            for p in range(half):
                xp = kv_ref[0, j, pl.ds(p, n_chunks, stride=CMP_STRIDE), :]
                first += _dot((xp + pos_ref[z, p:p + 1, :]).astype(BF16), w1_ref[z, p * HEAD_DIM:(p + 1) * HEAD_DIM, :])
                q = half + p
                second += _dot((xp + pos_ref[z, q:q + 1, :]).astype(BF16), w1_ref[z, q * HEAD_DIM:(q + 1) * HEAD_DIM, :])
            pre = first + pltpu.roll(second, n_chunks - 1, 0) + b1_ref[z]
            hid = _gelu_tanh(pre).astype(BF16)
            if z == 0:
                kc_o[0, hh] = _dot(hid, w2k_ref[...]).astype(BF16)
            else:
                vct_o[0, hh * HEAD_DIM:(hh + 1) * HEAD_DIM, :] = _dot_nt(w2vt_ref[...], hid).astype(BF16)


def _compress(kvc, pos, w1, b1, w2k, w2vt):
    bsz, _, L, _ = kvc.shape
    n_chunks = L // CMP_STRIDE
    kern = functools.partial(_compress_kernel, n_chunks=n_chunks)
    return pl.pallas_call(
        kern, grid=(bsz,),
        in_specs=[
            pl.BlockSpec((1, 2 * N_KV_HEADS, L, HEAD_DIM), lambda b: (b, 0, 0, 0)),
            _full((2, CMP_BLOCK, HEAD_DIM)),
            _full((2, CMP_BLOCK * HEAD_DIM, CMP_HIDDEN)),
            _full((2, 1, CMP_HIDDEN)),
            _full((CMP_HIDDEN, HEAD_DIM)),
            _full((HEAD_DIM, CMP_HIDDEN)),
        ],
        out_specs=[
            pl.BlockSpec((1, N_KV_HEADS, n_chunks, HEAD_DIM), lambda b: (b, 0, 0, 0)),
            pl.BlockSpec((1, KV_DIM, n_chunks), lambda b: (b, 0, 0)),
        ],
        out_shape=[
            jax.ShapeDtypeStruct((bsz, N_KV_HEADS, n_chunks, HEAD_DIM), BF16),
            jax.ShapeDtypeStruct((bsz, KV_DIM, n_chunks), BF16),
        ],
        compiler_params=_cparams("parallel"), name="compress",
    )(kvc, pos, w1, b1, w2k, w2vt)


def _nsa_kernel(qt_ref, kc_ref, vct_ref, ks_ref, vst_ref, kw_ref, vwt_ref, gt_ref, ov_ref, causal_ref, far_ref, o_ref,
                qa_ref, qw_ref, m_ref, acc_ref, s_ref, mx_ref, al_ref, sc_ref, pc_ref, *, tq, n_cmp_pad, n_slc):
    hkv = pl.program_id(1)
    i = pl.program_id(2)
    t0 = i * tq
    tk = tq
    n_sel = min(SLC_TOP_N, n_slc)
    slopes = [jnp.exp2(jnp.full((1, tq), -0.5, F32) * (hkv * GQA_GROUP + g + 1).astype(F32)) * LOG2E
              for g in range(GQA_GROUP)]
    q_heads = [qt_ref[0, g * HEAD_DIM:(g + 1) * HEAD_DIM, :] for g in range(GQA_GROUP)]
    gate = lambda g, z: gt_ref[0, 3 * g + z:3 * g + z + 1, :]

    t_lane = t0 + lax.broadcasted_iota(jnp.int32, (1, tq), 1)
    row8 = lax.broadcasted_iota(jnp.int32, (ONEHOT_LANE - ALIBI_LANE, tq), 0)

    def augmented(g, tail_rows):
        hi = slopes[g].astype(BF16).astype(F32)
        mid = (slopes[g] - hi).astype(BF16).astype(F32)
        lo = slopes[g] - hi - mid
        parts = jnp.where(row8 == 0, hi, jnp.where(row8 == 1, mid, jnp.where(row8 == 2, lo, 0.0)))
        return jnp.concatenate([q_heads[g].astype(F32), parts, tail_rows], axis=0).astype(BF16)

    def attend(state, q_ref, items, first, slot0):
        starts = [pl.multiple_of((c if valid is None else jnp.maximum(c, 0)) * tk, tk) for _, _, c, _, valid in items]
        for g in range(GQA_GROUP):
            for n, (k_ref, _, _, bias_ref, _) in enumerate(items):
                s = _dot(k_ref[0, 0, pl.ds(starts[n], tk), :], q_ref[g])
                if bias_ref is not None:
                    s = s + bias_ref[...]
                s_ref[slot0 + n * GQA_GROUP + g] = s
                mx_ref[slot0 + n * GQA_GROUP + g] = jnp.max(s, axis=0, keepdims=True)
        for g in range(GQA_GROUP):
            shifts = []
            for _, _, c, _, valid in items:
                shift = slopes[g] * ((c - i) * tk).astype(F32)
                shifts.append(shift if valid is None else jnp.where(valid, shift, NEG_BIG))
            m_new = functools.reduce(jnp.maximum, [mx_ref[slot0 + n * GQA_GROUP + g] + shifts[n]
                                                   for n in range(len(items))])
            if not first:
                m_new = jnp.maximum(m_ref[state, g], m_new)
                al_ref[g] = jnp.exp2(m_ref[state, g] - m_new)
            pv = None
            for n, (_, vt_ref, _, _, _) in enumerate(items):
                slot = slot0 + n * GQA_GROUP + g
                p = jnp.exp2(s_ref[slot] - (m_new - shifts[n])).astype(BF16)
                part = _dot(vt_ref[0, :, pl.ds(starts[n], tk)], p)
                pv = part if pv is None else pv + part
            acc_ref[state, g] = pv if first else al_ref[g] * acc_ref[state, g] + pv
            m_ref[state, g] = m_new

    def emit(state, z, accumulate):
        for g in range(GQA_GROUP):
            rows = slice(g * HEAD_DIM, (g + 1) * HEAD_DIM)
            out = gate(g, z) * (acc_ref[state, g, :HEAD_DIM, :] / acc_ref[state, g, HEAD_DIM:HEAD_DIM + 1, :])
            o_ref[0, rows, :] = o_ref[0, rows, :] + out if accumulate else out

    n_idx = lax.broadcasted_iota(jnp.int32, (n_cmp_pad, tq), 0)
    dist_c = (t0 + lax.broadcasted_iota(jnp.int32, (n_cmp_pad, tq), 1)) - (n_idx * CMP_STRIDE + (CMP_BLOCK - 1))
    mask_c = dist_c >= 0
    dist_cf = dist_c.astype(F32)
    kc = kc_ref[0, 0]
    vct = vct_ref[0]
    for g in range(GQA_GROUP):
        s = _dot(kc, q_heads[g]) - slopes[g] * dist_cf
        sc_ref[g] = jnp.where(mask_c, s, NEG_BIG)
    p_sum = jnp.zeros((n_cmp_pad, tq), F32)
    for g in range(GQA_GROUP):
        s = sc_ref[g]
        p = jnp.where(mask_c, jnp.exp2(s - jnp.max(s, axis=0, keepdims=True)), 0.0)
        p = p / jnp.maximum(jnp.sum(p, axis=0, keepdims=True), 1e-30)
        pc_ref[g] = p.astype(BF16)
        p_sum = p_sum + p
    for g in range(GQA_GROUP):
        rows = slice(g * HEAD_DIM, (g + 1) * HEAD_DIM)
        o_ref[0, rows, :] = gate(g, 0) * _dot(vct, pc_ref[g])

    imp = lax.dot_general(ov_ref[...], p_sum, (((1,), (0,)), ((), ())), precision=lax.Precision.HIGHEST,
                          preferred_element_type=F32)
    blk = lax.broadcasted_iota(jnp.int32, (n_slc, tq), 0)
    cur = t_lane // SLC_BLOCK
    forced = (blk == 0) | (blk == cur) | (blk == cur - 1)
    future = blk * SLC_BLOCK > t_lane
    imp = jnp.where(forced, SEL_BIG, jnp.where(future, -SEL_BIG, imp))
    rank = jnp.zeros((n_slc, tq), jnp.int32)
    for j in range(n_slc):
        row = imp[j:j + 1, :]
        beats = (row > imp) | ((row == imp) & (blk > j))
        rank = rank + beats.astype(jnp.int32)
    sel_bias = jnp.where(rank < n_sel, 0.0, NEG_BIG)

    qa_rows = jnp.concatenate([sel_bias, jnp.zeros((KEY_LANES - ONEHOT_LANE - n_slc, tq), F32)], axis=0)
    for g in range(GQA_GROUP):
        qa_ref[g] = augmented(g, qa_rows)

    zero_rows = jnp.zeros((KEY_LANES - ONEHOT_LANE, tq), F32)
    for g in range(GQA_GROUP):
        qw_ref[g] = augmented(g, zero_rows)
    n_back = WINDOW // tk
    n_win_items = n_back + 1
    win_items = [(kw_ref, vwt_ref, i, causal_ref, None)]
    for back in range(1, n_back + 1):
        win_items.append((kw_ref, vwt_ref, i - back, far_ref if back == n_back else None, i >= back))
    attend(0, qw_ref, win_items, True, 0)
    emit(0, 2, True)

    slc_items = [(ks_ref, vst_ref, i, causal_ref, None)]
    for back in range(1, SLC_LEAD_CHUNKS):
        slc_items.append((ks_ref, vst_ref, i - back, None, i >= back))
    attend(1, qa_ref, slc_items, True, n_win_items * GQA_GROUP)
    n_rest = jnp.maximum(i - (SLC_LEAD_CHUNKS - 1), 0)

    def slc_step(pair, carry):
        c0 = 2 * pair
        attend(1, qa_ref, [(ks_ref, vst_ref, c0, None, None), (ks_ref, vst_ref, c0 + 1, None, c0 + 1 < n_rest)],
               False, 0)
        return carry

    lax.fori_loop(0, (n_rest + 1) // 2, slc_step, 0)
    emit(1, 1, True)


def _nsa_attn(qt, kc, vct, ks, vst, kw, vwt, gt, tq):
    bsz, _, L = qt.shape
    n_cmp_pad = kc.shape[2]
    n_slc = L // SLC_BLOCK
    group_rows = GQA_GROUP * HEAD_DIM
    n = jnp.arange(n_cmp_pad)[None, :]
    j = jnp.arange(n_slc)[:, None]
    overlap_t = ((n * CMP_STRIDE < (j + 1) * SLC_BLOCK) & (n * CMP_STRIDE + CMP_BLOCK - 1 >= j * SLC_BLOCK)).astype(F32)
    assert tq == ATT_TILE and WINDOW % tq == 0 and n_slc % 8 == 0
    n_slots = (WINDOW // tq + 1 + SLC_LEAD_CHUNKS) * GQA_GROUP
    sub = jnp.arange(tq)[:, None]
    lane = jnp.arange(tq)[None, :]
    causal_bias = jnp.where(sub <= lane, 0.0, NEG_BIG).astype(F32)
    far_bias = jnp.where(sub > lane, 0.0, NEG_BIG).astype(F32)
    kern = functools.partial(_nsa_kernel, tq=tq, n_cmp_pad=n_cmp_pad, n_slc=n_slc)
    return pl.pallas_call(
        kern, grid=(bsz, N_KV_HEADS, L // tq),
        in_specs=[
            pl.BlockSpec((1, group_rows, tq), lambda b, h, i: (b, h, i)),
            pl.BlockSpec((1, 1, n_cmp_pad, HEAD_DIM), lambda b, h, i: (b, h, 0, 0)),
            pl.BlockSpec((1, HEAD_DIM, n_cmp_pad), lambda b, h, i: (b, h, 0)),
            pl.BlockSpec((1, 1, L, KEY_LANES), lambda b, h, i: (b, h, 0, 0)),
            pl.BlockSpec((1, VT_ROWS, L), lambda b, h, i: (b, h, 0)),
            pl.BlockSpec((1, 1, L, KEY_LANES), lambda b, h, i: (b, h, 0, 0)),
            pl.BlockSpec((1, VT_ROWS, L), lambda b, h, i: (b, h, 0)),
            pl.BlockSpec((1, GATE_ROWS, tq), lambda b, h, i: (b, h, i)),
            _full((n_slc, n_cmp_pad)),
            _full((tq, tq)), _full((tq, tq)),
        ],
        out_specs=pl.BlockSpec((1, group_rows, tq), lambda b, h, i: (b, h, i)),
        out_shape=jax.ShapeDtypeStruct((bsz, D_MODEL, L), F32),
        scratch_shapes=[pltpu.VMEM((GQA_GROUP, KEY_LANES, tq), BF16),
                        pltpu.VMEM((GQA_GROUP, KEY_LANES, tq), BF16),
                        pltpu.VMEM((2, GQA_GROUP, 1, tq), F32),
                        pltpu.VMEM((2, GQA_GROUP, VT_ROWS, tq), F32),
                        pltpu.VMEM((n_slots, tq, tq), F32),
                        pltpu.VMEM((n_slots, 1, tq), F32),
                        pltpu.VMEM((GQA_GROUP, 1, tq), F32),
                        pltpu.VMEM((GQA_GROUP, n_cmp_pad, tq), F32),
                        pltpu.VMEM((GQA_GROUP, n_cmp_pad, tq), BF16)],
        compiler_params=_cparams("parallel", "parallel", "parallel"), name="nsa_attn",
    )(qt, kc, vct, ks, vst, kw, vwt, gt, overlap_t, causal_bias, far_bias)


def _mix_kernel(h_ref, ys_ref, ot_ref, wgm_ref, wno_ref, wout_ref, g_ref, b_ref, h1_o):
    h = h_ref[0]
    o = ot_ref[0].T.astype(BF16)
    y_nsa = _dot(o, wno_ref[...])
    gm = _sigmoid(_dot(h.astype(BF16), wgm_ref[...]))
    mix = gm[:, :D_MODEL] * ys_ref[0] + gm[:, D_MODEL:] * y_nsa
    mixed = _dot(mix.astype(BF16), wout_ref[...])
    h1_o[0] = _layer_norm(DEEPNORM_ALPHA * h + mixed, g_ref[...], b_ref[...])


def _mix_ln1(h, y_ssm, o_t, wgm, wno, wout, g, b, tm):
    bsz, L, d = h.shape
    return pl.pallas_call(
        _mix_kernel, grid=(bsz, L // tm),
        in_specs=[
            pl.BlockSpec((1, tm, d), lambda b, i: (b, i, 0)),
            pl.BlockSpec((1, tm, d), lambda b, i: (b, i, 0)),
            pl.BlockSpec((1, d, tm), lambda b, i: (b, 0, i)),
            _full((d, 2 * d)), _full((d, d)), _full((d, d)), _full((1, d)), _full((1, d)),
        ],
        out_specs=pl.BlockSpec((1, tm, d), lambda b, i: (b, i, 0)),
        out_shape=jax.ShapeDtypeStruct((bsz, L, d), F32),
        compiler_params=_cparams("parallel", "parallel"), name="mix_ln1",
    )(h, y_ssm, o_t, wgm, wno, wout, g.reshape(1, d), b.reshape(1, d))


def _memkv_kernel(mem_ref, w_ref, k_o, v_o):
    kv = _dot(mem_ref[0].astype(BF16), w_ref[...])
    k_o[0] = kv[:, :D_MODEL].astype(BF16)
    v_o[0] = kv[:, D_MODEL:].astype(BF16)


def _mem_kv(mem, w_kv):
    bsz, m, d = mem.shape
    spec = pl.BlockSpec((1, m, d), lambda b: (b, 0, 0))
    return pl.pallas_call(
        _memkv_kernel, grid=(bsz,),
        in_specs=[spec, _full((d, 2 * d))],
        out_specs=[spec, spec],
        out_shape=[jax.ShapeDtypeStruct((bsz, m, d), BF16)] * 2,
        compiler_params=_cparams("parallel"), name="mem_kv",
    )(mem, w_kv)


def _xattn_kernel(h_ref, k_ref, v_ref, wq_ref, wo_ref, g_ref, b_ref, h2_o):
    h = h_ref[0]
    q = (_dot(h.astype(BF16), wq_ref[...]) * (X_HEAD_DIM ** -0.5)).astype(BF16)
    outs = []
    for hd in range(X_HEADS):
        cols = slice(hd * X_HEAD_DIM, (hd + 1) * X_HEAD_DIM)
        s = _dot_nt(q[:, cols], k_ref[0, :, cols])
        p = jnp.exp(s - jnp.max(s, axis=-1, keepdims=True))
        p = p / jnp.sum(p, axis=-1, keepdims=True)
        outs.append(_dot(p.astype(BF16), v_ref[0, :, cols]))
    o = jnp.concatenate(outs, axis=-1).astype(BF16)
    h2_o[0] = _layer_norm(DEEPNORM_ALPHA * h + _dot(o, wo_ref[...]), g_ref[...], b_ref[...])


def _xattn_ln2(h, k, v, wq, wo, g, b, tm):
    bsz, L, d = h.shape
    m = k.shape[1]
    row = pl.BlockSpec((1, tm, d), lambda b, i: (b, i, 0))
    kv = pl.BlockSpec((1, m, d), lambda b, i: (b, 0, 0))
    return pl.pallas_call(
        _xattn_kernel, grid=(bsz, L // tm),
        in_specs=[row, kv, kv, _full((d, d)), _full((d, d)), _full((1, d)), _full((1, d))],
        out_specs=row,
        out_shape=jax.ShapeDtypeStruct((bsz, L, d), F32),
        compiler_params=_cparams("parallel", "parallel"), name="xattn_ln2",
    )(h, k, v, wq, wo, g.reshape(1, d), b.reshape(1, d))


def _ffn_kernel(h_ref, win_ref, wout_ref, g_ref, b_ref, o_ref):
    h = h_ref[...]
    gu = _dot(h.astype(BF16), win_ref[...])
    gate = gu[:, :D_FF]
    act = gate * _sigmoid(gate) * gu[:, D_FF:]
    o_ref[...] = _layer_norm(DEEPNORM_ALPHA * h + _dot(act.astype(BF16), wout_ref[...]), g_ref[...], b_ref[...])


def _ffn_ln3(h, win, wout, g, b, tm):
    rows, d = h.shape
    row = pl.BlockSpec((tm, d), lambda i: (i, 0))
    return pl.pallas_call(
        _ffn_kernel, grid=(rows // tm,),
        in_specs=[row, _full((d, 2 * D_FF)), _full((D_FF, d)), _full((1, d)), _full((1, d))],
        out_specs=row,
        out_shape=jax.ShapeDtypeStruct((rows, d), F32),
        compiler_params=_cparams("parallel"), name="ffn_ln3",
    )(h, win, wout, g.reshape(1, d), b.reshape(1, d))


def _inproj_weights(w_in):
    d = D_MODEL
    c0 = SSM_WIDTH
    c1 = c0 + N_HEADS * HEAD_DIM
    c2 = c1 + 2 * KV_DIM
    c3 = c2 + 2 * KV_DIM
    c4 = c3 + 2 * KV_DIM
    c5 = c4 + 3 * N_HEADS
    values_t = lambda m: jnp.pad(m.T.reshape(N_KV_HEADS, HEAD_DIM, d),
                                 ((0, 0), (0, VT_ROWS - HEAD_DIM), (0, 0))).reshape(N_KV_HEADS * VT_ROWS, d)
    wg = w_in[:, c4:c5].T.reshape(N_KV_HEADS, 3 * GQA_GROUP, d)
    wg = jnp.pad(wg, ((0, 0), (0, GATE_ROWS - 3 * GQA_GROUP), (0, 0))).reshape(N_KV_HEADS * GATE_ROWS, d)
    return {
        "wu": w_in[:, :c0].astype(BF16),
        "wqt": w_in[:, c0:c1].T.astype(BF16),
        "wk": jnp.concatenate([w_in[:, c1:c2], w_in[:, c2:c2 + KV_DIM], w_in[:, c3:c3 + KV_DIM]], axis=1).astype(BF16),
        "wvst": values_t(w_in[:, c2 + KV_DIM:c3]).astype(BF16),
        "wvwt": values_t(w_in[:, c3 + KV_DIM:c4]).astype(BF16),
        "wgt": wg.astype(BF16),
        "wgm": w_in[:, c5:].astype(BF16),
    }


def _s5_weights(bb_re, bb_im, c_re, c_im, d_skip, w_glu, b_glu, w_o):
    per_slab = SSM_LANE_SLAB // SSM_GROUP
    eye = jnp.eye(per_slab, dtype=F32)

    def b_blocks(bb):
        bb = bb.reshape(SSM_SLABS, per_slab, SSM_STATE, SSM_GROUP)
        return jnp.einsum('jgnc,gh->jgchn', bb, eye).reshape(SSM_SLABS, SSM_LANE_SLAB, SSM_STATE_SLAB).astype(BF16)

    def c_blocks(c):
        c = c.astype(F32).reshape(SSM_SLABS, per_slab, SSM_GROUP, SSM_STATE)
        return jnp.einsum('jgcn,gh->jgnhc', c, eye).reshape(SSM_SLABS, SSM_STATE_SLAB, SSM_LANE_SLAB).astype(BF16)

    return {
        "wbre": b_blocks(bb_re), "wbim": b_blocks(bb_im), "cre": c_blocks(c_re), "cim": c_blocks(c_im),
        "d": d_skip.astype(F32).reshape(1, SSM_WIDTH), "wglu": w_glu.astype(BF16),
        "bglu": b_glu.astype(F32).reshape(1, SSM_WIDTH), "wo": w_o.astype(BF16),
    }


def _pick(total, want):
    return want if total % want == 0 else total


def kernel(x, mem, ln_emb_g, ln_emb_b, w_in, ssm_a_re, ssm_a_im, ssm_b_re, ssm_b_im, ssm_c_re, ssm_c_im, ssm_d,
           ssm_log_dt, ssm_w_glu, ssm_b_glu, ssm_w_out, cmp_pos, cmp_w1, cmp_b1, cmp_w2, nsa_w_out, w_out,
           ln1_g, ln1_b, xattn_w_q, xattn_w_kv, xattn_w_o, ln2_g, ln2_b, ffn_w_in, ffn_w_out, ln3_g, ln3_b):
    bsz, L, d = x.shape
    assert w_in.shape[0] == 1, "one layer: the trunk-entry LayerNorm is fused into its input projection"
    l = 0
    tile = _pick(L, 256)
    wi = _inproj_weights(w_in[l])
    h, u, qt, kvc, ks, kw, vst, vwt, gt = _ln_inproj(x, ln_emb_g, ln_emb_b, wi, tile)

    lb_re, lb_im, bb_re, bb_im = _zoh_prep(ssm_a_re[l], ssm_a_im[l], ssm_log_dt[l], ssm_b_re[l], ssm_b_im[l])
    ws = _s5_weights(bb_re, bb_im, ssm_c_re[l], ssm_c_im[l], ssm_d[l], ssm_w_glu[l], ssm_b_glu[l], ssm_w_out[l])
    y_ssm = _s5(u, lb_re, lb_im, ws, _pick(L, 32))

    kc, vct = _compress(kvc, cmp_pos[l].astype(F32), cmp_w1[l].astype(BF16),
                        cmp_b1[l].astype(F32).reshape(2, 1, CMP_HIDDEN),
                        cmp_w2[l, 0].astype(BF16), cmp_w2[l, 1].T.astype(BF16))
    o_t = _nsa_attn(qt, kc, vct, ks, vst, kw, vwt, gt, tile)

    h = _mix_ln1(h, y_ssm, o_t, wi["wgm"], nsa_w_out[l].astype(BF16), w_out[l].astype(BF16),
                 ln1_g[l], ln1_b[l], tile)
    mk, mv = _mem_kv(mem, xattn_w_kv[l].astype(BF16))
    h = _xattn_ln2(h, mk, mv, xattn_w_q[l].astype(BF16), xattn_w_o[l].astype(BF16), ln2_g[l], ln2_b[l], tile)
    h = _ffn_ln3(h.reshape(bsz * L, d), ffn_w_in[l].astype(BF16), ffn_w_out[l].astype(BF16),
                 ln3_g[l], ln3_b[l], tile)
    return h.reshape(bsz, L, d)
```

```python
import functools
import math

import jax
import jax.numpy as jnp
from jax import lax
from jax.experimental import pallas as pl
from jax.experimental.pallas import tpu as pltpu

F32 = jnp.float32
BF16 = jnp.bfloat16

D_MODEL = 1024
SSM_WIDTH = 512
SSM_GROUP = 16
SSM_GROUPS = SSM_WIDTH // SSM_GROUP
SSM_STATE = 64
SSM_STATES = SSM_GROUPS * SSM_STATE
SSM_EIG_CLIP = -1e-4
N_HEADS = 16
N_KV_HEADS = 4
HEAD_DIM = 64
GQA_GROUP = N_HEADS // N_KV_HEADS
KV_DIM = N_KV_HEADS * HEAD_DIM
CMP_BLOCK = 32
CMP_STRIDE = 16
CMP_HIDDEN = 256
SLC_BLOCK = 64
SLC_TOP_N = 8
WINDOW = 512
SEL_BIG = 1e9
X_HEADS = 4
X_HEAD_DIM = D_MODEL // X_HEADS
D_FF = 2816
DEEPNORM_ALPHA = 2.0 ** 0.25
LN_EPS = 1e-5
NEG_BIG = -1e30
GATE_ROWS = 16
LOG2E = 1.4426950408889634
ATT_TILE = 256
KEY_LANES = 128
ALIBI_LANE = HEAD_DIM
ONEHOT_LANE = HEAD_DIM + 8
VT_ROWS = 80
SLC_LEAD_CHUNKS = 3
CHUNK_ROWS = 16

V7X_VMEM_LIMIT_BYTES = 56 * 1024 * 1024
SSM_LANE_SLAB = 128
SSM_SLABS = SSM_WIDTH // SSM_LANE_SLAB
SSM_STATE_SLAB = SSM_STATES // SSM_SLABS


def _cparams(*sem):
    return pltpu.CompilerParams(dimension_semantics=sem, vmem_limit_bytes=V7X_VMEM_LIMIT_BYTES)


def _full(shape):
    zeros = (0,) * len(shape)
    return pl.BlockSpec(shape, lambda *_: zeros)


def _layer_norm(x, g, b):
    mu = jnp.mean(x, axis=-1, keepdims=True)
    xc = x - mu
    var = jnp.mean(xc * xc, axis=-1, keepdims=True)
    return xc * lax.rsqrt(var + LN_EPS) * g + b


def _gelu_tanh(x):
    return 0.5 * x * (1.0 + jnp.tanh(math.sqrt(2.0 / math.pi) * (x + 0.044715 * (x * x * x))))


def _sigmoid(x):
    return 1.0 / (1.0 + jnp.exp(-x))


def _dot(a, b):
    return jnp.dot(a, b, preferred_element_type=F32)


def _dot_nt(a, b):
    return lax.dot_general(a, b, (((1,), (1,)), ((), ())), preferred_element_type=F32)


def _zoh_kernel(a_re, a_im, log_dt, b_re, b_im, lb_re_o, lb_im_o, bb_re_o, bb_im_o):
    lam_re = jnp.minimum(a_re[...], SSM_EIG_CLIP)
    lam_im = a_im[...]
    dt = jnp.exp(log_dt[...])
    mag = jnp.exp(lam_re * dt)
    lb_re = mag * jnp.cos(lam_im * dt)
    lb_im = mag * jnp.sin(lam_im * dt)
    den = lam_re * lam_re + lam_im * lam_im
    nr = lb_re - 1.0
    f_re = (nr * lam_re + lb_im * lam_im) / den
    f_im = (lb_im * lam_re - nr * lam_im) / den
    br = b_re[...]
    bi = b_im[...]
    lb_re_o[...] = lb_re
    lb_im_o[...] = lb_im
    bb_re_o[...] = f_re * br - f_im * bi
    bb_im_o[...] = f_re * bi + f_im * br


def _zoh_prep(a_re, a_im, log_dt, b_re, b_im):
    gn = SSM_STATES
    col = lambda v: v.astype(F32).reshape(gn, 1)
    dt_col = jnp.broadcast_to(log_dt.astype(F32)[:, None], (SSM_GROUPS, SSM_STATE)).reshape(gn, 1)
    outs = pl.pallas_call(
        _zoh_kernel,
        out_shape=[jax.ShapeDtypeStruct((gn, 1), F32)] * 2 + [jax.ShapeDtypeStruct((gn, SSM_GROUP), F32)] * 2,
        name="zoh_prep",
    )(col(a_re), col(a_im), dt_col, b_re.astype(F32).reshape(gn, SSM_GROUP), b_im.astype(F32).reshape(gn, SSM_GROUP))
    lb_re, lb_im, bb_re, bb_im = outs
    shape_b = (SSM_GROUPS, SSM_STATE, SSM_GROUP)
    return lb_re.reshape(1, gn), lb_im.reshape(1, gn), bb_re.reshape(shape_b), bb_im.reshape(shape_b)


def _inproj_kernel(x_ref, g_ref, b_ref, wu_ref, wqt_ref, wk_ref, wvst_ref, wvwt_ref, wgt_ref,
                   h_o, u_o, qt_o, kvc_o, ks_o, kw_o, vst_o, vwt_o, gt_o):
    tl = x_ref.shape[1]
    h = _layer_norm(x_ref[0], g_ref[...], b_ref[...])
    h_o[0] = h
    hb = h.astype(BF16)
    u_o[0] = _dot(hb, wu_ref[...])
    qt_o[0] = (_dot_nt(wqt_ref[...], hb) * (HEAD_DIM ** -0.5 * LOG2E)).astype(BF16)

    k_all = _dot(hb, wk_ref[...])

    def head_tile(j):
        tile = k_all[:, (j // 2) * KEY_LANES:(j // 2 + 1) * KEY_LANES]
        return pltpu.roll(tile, HEAD_DIM, 1) if j % 2 else tile

    for j in range(2 * N_KV_HEADS):
        kvc_o[0, j] = head_tile(j)[:, :HEAD_DIM]

    lane = lax.broadcasted_iota(jnp.int32, (tl, KEY_LANES), 1)
    pos = pl.program_id(1) * tl + lax.broadcasted_iota(jnp.int32, (tl, KEY_LANES), 0)
    alibi = jnp.where((lane >= ALIBI_LANE) & (lane < ALIBI_LANE + 3), (pos % ATT_TILE).astype(F32), 0.0)
    slc_feat = alibi + jnp.where((lane >= ONEHOT_LANE) & (lane - ONEHOT_LANE == pos // SLC_BLOCK), 1.0, 0.0)
    is_key = lane < HEAD_DIM
    for j in range(N_KV_HEADS):
        ks_o[0, j] = jnp.where(is_key, head_tile(2 * N_KV_HEADS + j), slc_feat).astype(BF16)
        kw_o[0, j] = jnp.where(is_key, head_tile(3 * N_KV_HEADS + j), alibi).astype(BF16)
    row = lax.broadcasted_iota(jnp.int32, (N_KV_HEADS * VT_ROWS, tl), 0)
    ones_row = jnp.where(row % VT_ROWS == HEAD_DIM, 1.0, 0.0)
    vst_o[0] = (_dot_nt(wvst_ref[...], hb) + ones_row).astype(BF16)
    vwt_o[0] = (_dot_nt(wvwt_ref[...], hb) + ones_row).astype(BF16)
    gt_o[0] = _sigmoid(_dot_nt(wgt_ref[...], hb))


def _ln_inproj(x, ln_g, ln_b, w, tl):
    bsz, L, d = x.shape
    n_gate = N_KV_HEADS * GATE_ROWS
    grid = (bsz, L // tl)
    in_specs = [
        pl.BlockSpec((1, tl, d), lambda b, i: (b, i, 0)),
        _full((1, d)), _full((1, d)),
        _full((d, SSM_WIDTH)),
        _full((D_MODEL, d)),
        _full((d, 4 * KV_DIM)),
        _full((N_KV_HEADS * VT_ROWS, d)),
        _full((N_KV_HEADS * VT_ROWS, d)),
        _full((n_gate, d)),
    ]
    assert tl % ATT_TILE == 0 or ATT_TILE % tl == 0
    assert ONEHOT_LANE + L // SLC_BLOCK <= KEY_LANES
    out_shape = [
        jax.ShapeDtypeStruct((bsz, L, d), F32),
        jax.ShapeDtypeStruct((bsz, L, SSM_WIDTH), F32),
        jax.ShapeDtypeStruct((bsz, D_MODEL, L), BF16),
        jax.ShapeDtypeStruct((bsz, 2 * N_KV_HEADS, L, HEAD_DIM), F32),
        jax.ShapeDtypeStruct((bsz, N_KV_HEADS, L, KEY_LANES), BF16),
        jax.ShapeDtypeStruct((bsz, N_KV_HEADS, L, KEY_LANES), BF16),
        jax.ShapeDtypeStruct((bsz, N_KV_HEADS * VT_ROWS, L), BF16),
        jax.ShapeDtypeStruct((bsz, N_KV_HEADS * VT_ROWS, L), BF16),
        jax.ShapeDtypeStruct((bsz, n_gate, L), F32),
    ]
    out_specs = [
        pl.BlockSpec((1, tl, d), lambda b, i: (b, i, 0)),
        pl.BlockSpec((1, tl, SSM_WIDTH), lambda b, i: (b, i, 0)),
        pl.BlockSpec((1, D_MODEL, tl), lambda b, i: (b, 0, i)),
        pl.BlockSpec((1, 2 * N_KV_HEADS, tl, HEAD_DIM), lambda b, i: (b, 0, i, 0)),
        pl.BlockSpec((1, N_KV_HEADS, tl, KEY_LANES), lambda b, i: (b, 0, i, 0)),
        pl.BlockSpec((1, N_KV_HEADS, tl, KEY_LANES), lambda b, i: (b, 0, i, 0)),
        pl.BlockSpec((1, N_KV_HEADS * VT_ROWS, tl), lambda b, i: (b, 0, i)),
        pl.BlockSpec((1, N_KV_HEADS * VT_ROWS, tl), lambda b, i: (b, 0, i)),
        pl.BlockSpec((1, n_gate, tl), lambda b, i: (b, 0, i)),
    ]
    return pl.pallas_call(
        _inproj_kernel, grid=grid, in_specs=in_specs, out_specs=out_specs, out_shape=out_shape,
        compiler_params=_cparams("parallel", "parallel"), name="ln_inproj",
    )(x, ln_g.reshape(1, d), ln_b.reshape(1, d), w["wu"], w["wqt"], w["wk"],
      w["wvst"], w["wvwt"], w["wgt"])


def _s5_kernel(u_ref, lre_ref, lim_ref, wbre_ref, wbim_ref, cre_ref, cim_ref, d_ref, wglu_ref, bglu_ref, wo_ref,
               y_o, sre, sim, hre, him, *, bsz, steps, pitch):
    @pl.when(pl.program_id(0) == 0)
    def _():
        sre[...] = jnp.zeros_like(sre)
        sim[...] = jnp.zeros_like(sim)

    lanes = SSM_LANE_SLAB
    per_slab = SSM_STATE_SLAB // lanes
    u = u_ref[...].reshape(bsz * steps, SSM_WIDTH)
    ub = u.astype(BF16)
    for j in range(SSM_SLABS):
        uj = ub[:, j * lanes:(j + 1) * lanes]
        for w_ref, h_ref in ((wbre_ref, hre), (wbim_ref, him)):
            r = _dot(uj, w_ref[j])
            for b in range(bsz):
                for k in range(per_slab):
                    h_ref[j * per_slab + k, pl.ds(b, steps, stride=pitch), :] = (
                        r[b * steps:(b + 1) * steps, k * lanes:(k + 1) * lanes])

    for j in range(SSM_SLABS):
        slabs = [j * per_slab + k for k in range(per_slab)]
        lr = [jnp.broadcast_to(lre_ref[:, s * lanes:(s + 1) * lanes], (bsz, lanes)) for s in slabs]
        li = [jnp.broadcast_to(lim_ref[:, s * lanes:(s + 1) * lanes], (bsz, lanes)) for s in slabs]

        def step(t, carry, slabs=slabs, lr=lr, li=li):
            rows = pl.ds(pl.multiple_of(t * pitch, 8), bsz)
            new = []
            for k, s in enumerate(slabs):
                pr, pi = carry[2 * k], carry[2 * k + 1]
                nr = lr[k] * pr - li[k] * pi + hre[s, rows, :]
                ni = lr[k] * pi + li[k] * pr + him[s, rows, :]
                hre[s, rows, :] = nr
                him[s, rows, :] = ni
                new += [nr, ni]
            return tuple(new)

        init = []
        for s in slabs:
            init += [sre[:, s * lanes:(s + 1) * lanes], sim[:, s * lanes:(s + 1) * lanes]]
        fin = lax.fori_loop(0, steps, step, tuple(init), unroll=4)
        for k, s in enumerate(slabs):
            sre[:, s * lanes:(s + 1) * lanes] = fin[2 * k]
            sim[:, s * lanes:(s + 1) * lanes] = fin[2 * k + 1]

    def states(h_ref, j):
        return jnp.concatenate(
            [jnp.concatenate([h_ref[j * per_slab + k, pl.ds(b, steps, stride=pitch), :] for k in range(per_slab)],
                             axis=-1) for b in range(bsz)], axis=0)

    ys = []
    for j in range(SSM_SLABS):
        ys.append(_dot(states(hre, j).astype(BF16), cre_ref[j]) - _dot(states(him, j).astype(BF16), cim_ref[j]))
    y = jnp.concatenate(ys, axis=-1) + d_ref[...] * u
    g = _gelu_tanh(y)
    y2 = g * _sigmoid(_dot(g.astype(BF16), wglu_ref[...]) + bglu_ref[...])
    y_o[...] = _dot(y2.astype(BF16), wo_ref[...]).reshape(bsz, steps, D_MODEL)


def _s5(u, lb_re, lb_im, w, steps):
    bsz, L, _ = u.shape
    pitch = -(-bsz // 8) * 8
    pitch += 8 if (pitch // 8) % 2 == 0 else 0
    grid = (L // steps,)
    kern = functools.partial(_s5_kernel, bsz=bsz, steps=steps, pitch=pitch)
    n_slabs = SSM_STATES // SSM_LANE_SLAB
    in_specs = [
        pl.BlockSpec((bsz, steps, SSM_WIDTH), lambda c: (0, c, 0)),
        _full((1, SSM_STATES)), _full((1, SSM_STATES)),
        _full((SSM_SLABS, SSM_LANE_SLAB, SSM_STATE_SLAB)), _full((SSM_SLABS, SSM_LANE_SLAB, SSM_STATE_SLAB)),
        _full((SSM_SLABS, SSM_STATE_SLAB, SSM_LANE_SLAB)), _full((SSM_SLABS, SSM_STATE_SLAB, SSM_LANE_SLAB)),
        _full((1, SSM_WIDTH)),
        _full((SSM_WIDTH, SSM_WIDTH)), _full((1, SSM_WIDTH)),
        _full((SSM_WIDTH, D_MODEL)),
    ]
    return pl.pallas_call(
        kern, grid=grid, in_specs=in_specs,
        out_specs=pl.BlockSpec((bsz, steps, D_MODEL), lambda c: (0, c, 0)),
        out_shape=jax.ShapeDtypeStruct((bsz, L, D_MODEL), F32),
        scratch_shapes=[pltpu.VMEM((bsz, SSM_STATES), F32), pltpu.VMEM((bsz, SSM_STATES), F32),
                        pltpu.VMEM((n_slabs, steps * pitch, SSM_LANE_SLAB), F32),
                        pltpu.VMEM((n_slabs, steps * pitch, SSM_LANE_SLAB), F32)],
        compiler_params=_cparams("arbitrary"), name="s5",
    )(u, lb_re, lb_im, w["wbre"], w["wbim"], w["cre"], w["cim"], w["d"], w["wglu"], w["bglu"], w["wo"])


def _compress_kernel(kv_ref, pos_ref, w1_ref, b1_ref, w2k_ref, w2vt_ref, kc_o, vct_o, *, n_chunks):
    half = CMP_BLOCK // 2
    for z in range(2):
        for hh in range(N_KV_HEADS):
            j = z * N_KV_HEADS + hh
            first = jnp.zeros((n_chunks, CMP_HIDDEN), F32)
            second = jnp.zeros((n_chunks, CMP_HIDDEN), F32)
            for p in range(half):
                xp = kv_ref[0, j, pl.ds(p, n_chunks, stride=CMP_STRIDE), :]
                first += _dot((xp + pos_ref[z, p:p + 1, :]).astype(BF16), w1_ref[z, p * HEAD_DIM:(p + 1) * HEAD_DIM, :])
                q = half + p
                second += _dot((xp + pos_ref[z, q:q + 1, :]).astype(BF16), w1_ref[z, q * HEAD_DIM:(q + 1) * HEAD_DIM, :])
            pre = first + pltpu.roll(second, n_chunks - 1, 0) + b1_ref[z]
            hid = _gelu_tanh(pre).astype(BF16)
            if z == 0:
                kc_o[0, hh] = _dot(hid, w2k_ref[...]).astype(BF16)
            else:
                vct_o[0, hh * HEAD_DIM:(hh + 1) * HEAD_DIM, :] = _dot_nt(w2vt_ref[...], hid).astype(BF16)


def _compress(kvc, pos, w1, b1, w2k, w2vt):
    bsz, _, L, _ = kvc.shape
    n_chunks = L // CMP_STRIDE
    kern = functools.partial(_compress_kernel, n_chunks=n_chunks)
    return pl.pallas_call(
        kern, grid=(bsz,),
        in_specs=[
            pl.BlockSpec((1, 2 * N_KV_HEADS, L, HEAD_DIM), lambda b: (b, 0, 0, 0)),
            _full((2, CMP_BLOCK, HEAD_DIM)),
            _full((2, CMP_BLOCK * HEAD_DIM, CMP_HIDDEN)),
            _full((2, 1, CMP_HIDDEN)),
            _full((CMP_HIDDEN, HEAD_DIM)),
            _full((HEAD_DIM, CMP_HIDDEN)),
        ],
        out_specs=[
            pl.BlockSpec((1, N_KV_HEADS, n_chunks, HEAD_DIM), lambda b: (b, 0, 0, 0)),
            pl.BlockSpec((1, KV_DIM, n_chunks), lambda b: (b, 0, 0)),
        ],
        out_shape=[
            jax.ShapeDtypeStruct((bsz, N_KV_HEADS, n_chunks, HEAD_DIM), BF16),
            jax.ShapeDtypeStruct((bsz, KV_DIM, n_chunks), BF16),
        ],
        compiler_params=_cparams("parallel"), name="compress",
    )(kvc, pos, w1, b1, w2k, w2vt)


def _nsa_kernel(qt_ref, kc_ref, vct_ref, ks_ref, vst_ref, kw_ref, vwt_ref, gt_ref, ov_ref, causal_ref, far_ref, chunk_ref,
                o_ref, qa_ref, qw_ref, m_ref, acc_ref, s_ref, mx_ref, al_ref, sc_ref, pc_ref, need_ref,
                *, tq, n_cmp_pad, n_slc):
    hkv = pl.program_id(1)
    i = pl.program_id(2)
    t0 = i * tq
    tk = tq
    n_sel = min(SLC_TOP_N, n_slc)
    slopes = [jnp.exp2(jnp.full((1, tq), -0.5, F32) * (hkv * GQA_GROUP + g + 1).astype(F32)) * LOG2E
              for g in range(GQA_GROUP)]
    q_heads = [qt_ref[0, g * HEAD_DIM:(g + 1) * HEAD_DIM, :] for g in range(GQA_GROUP)]
    gate = lambda g, z: gt_ref[0, 3 * g + z:3 * g + z + 1, :]

    t_lane = t0 + lax.broadcasted_iota(jnp.int32, (1, tq), 1)
    row8 = lax.broadcasted_iota(jnp.int32, (ONEHOT_LANE - ALIBI_LANE, tq), 0)

    def augmented(g, tail_rows):
        hi = slopes[g].astype(BF16).astype(F32)
        mid = (slopes[g] - hi).astype(BF16).astype(F32)
        lo = slopes[g] - hi - mid
        parts = jnp.where(row8 == 0, hi, jnp.where(row8 == 1, mid, jnp.where(row8 == 2, lo, 0.0)))
        return jnp.concatenate([q_heads[g].astype(F32), parts, tail_rows], axis=0).astype(BF16)

    def attend(state, q_ref, items, first, slot0):
        starts = [pl.multiple_of((c if valid is None else jnp.maximum(c, 0)) * tk, tk) for _, _, c, _, valid in items]
        for g in range(GQA_GROUP):
            for n, (k_ref, _, _, bias_ref, _) in enumerate(items):
                s = _dot(k_ref[0, 0, pl.ds(starts[n], tk), :], q_ref[g])
                if bias_ref is not None:
                    s = s + bias_ref[...]
                s_ref[slot0 + n * GQA_GROUP + g] = s
                mx_ref[slot0 + n * GQA_GROUP + g] = jnp.max(s, axis=0, keepdims=True)
        for g in range(GQA_GROUP):
            shifts = []
            for _, _, c, _, valid in items:
                shift = slopes[g] * ((c - i) * tk).astype(F32)
                shifts.append(shift if valid is None else jnp.where(valid, shift, NEG_BIG))
            m_new = functools.reduce(jnp.maximum, [mx_ref[slot0 + n * GQA_GROUP + g] + shifts[n]
                                                   for n in range(len(items))])
            if not first:
                m_new = jnp.maximum(m_ref[state, g], m_new)
                al_ref[g] = jnp.exp2(m_ref[state, g] - m_new)
            pv = None
            for n, (_, vt_ref, _, _, _) in enumerate(items):
                slot = slot0 + n * GQA_GROUP + g
                p = jnp.exp2(s_ref[slot] - (m_new - shifts[n])).astype(BF16)
                part = _dot(vt_ref[0, :, pl.ds(starts[n], tk)], p)
                pv = part if pv is None else pv + part
            acc_ref[state, g] = pv if first else al_ref[g] * acc_ref[state, g] + pv
            m_ref[state, g] = m_new

    def emit(state, z, accumulate):
        for g in range(GQA_GROUP):
            rows = slice(g * HEAD_DIM, (g + 1) * HEAD_DIM)
            out = gate(g, z) * (acc_ref[state, g, :HEAD_DIM, :] / acc_ref[state, g, HEAD_DIM:HEAD_DIM + 1, :])
            o_ref[0, rows, :] = o_ref[0, rows, :] + out if accumulate else out

    n_idx = lax.broadcasted_iota(jnp.int32, (n_cmp_pad, tq), 0)
    dist_c = (t0 + lax.broadcasted_iota(jnp.int32, (n_cmp_pad, tq), 1)) - (n_idx * CMP_STRIDE + (CMP_BLOCK - 1))
    mask_c = dist_c >= 0
    dist_cf = dist_c.astype(F32)
    kc = kc_ref[0, 0]
    vct = vct_ref[0]
    for g in range(GQA_GROUP):
        s = _dot(kc, q_heads[g]) - slopes[g] * dist_cf
        sc_ref[g] = jnp.where(mask_c, s, NEG_BIG)
    p_sum = jnp.zeros((n_cmp_pad, tq), F32)
    for g in range(GQA_GROUP):
        s = sc_ref[g]
        p = jnp.where(mask_c, jnp.exp2(s - jnp.max(s, axis=0, keepdims=True)), 0.0)
        p = p / jnp.maximum(jnp.sum(p, axis=0, keepdims=True), 1e-30)
        pc_ref[g] = p.astype(BF16)
        p_sum = p_sum + p
    for g in range(GQA_GROUP):
        rows = slice(g * HEAD_DIM, (g + 1) * HEAD_DIM)
        o_ref[0, rows, :] = gate(g, 0) * _dot(vct, pc_ref[g])

    imp = lax.dot_general(ov_ref[...], p_sum, (((1,), (0,)), ((), ())), precision=lax.Precision.HIGHEST,
                          preferred_element_type=F32)
    blk = lax.broadcasted_iota(jnp.int32, (n_slc, tq), 0)
    cur = t_lane // SLC_BLOCK
    forced = (blk == 0) | (blk == cur) | (blk == cur - 1)
    future = blk * SLC_BLOCK > t_lane
    imp = jnp.where(forced, SEL_BIG, jnp.where(future, -SEL_BIG, imp))
    rank = jnp.zeros((n_slc, tq), jnp.int32)
    for j in range(n_slc):
        row = imp[j:j + 1, :]
        beats = (row > imp) | ((row == imp) & (blk > j))
        rank = rank + beats.astype(jnp.int32)
    sel_bias = jnp.where(rank < n_sel, 0.0, NEG_BIG)

    qa_rows = jnp.concatenate([sel_bias, jnp.zeros((KEY_LANES - ONEHOT_LANE - n_slc, tq), F32)], axis=0)
    for g in range(GQA_GROUP):
        qa_ref[g] = augmented(g, qa_rows)

    zero_rows = jnp.zeros((KEY_LANES - ONEHOT_LANE, tq), F32)
    for g in range(GQA_GROUP):
        qw_ref[g] = augmented(g, zero_rows)
    n_back = WINDOW // tk
    n_win_items = n_back + 1
    win_items = [(kw_ref, vwt_ref, i, causal_ref, None)]
    for back in range(1, n_back + 1):
        win_items.append((kw_ref, vwt_ref, i - back, far_ref if back == n_back else None, i >= back))
    attend(0, qw_ref, win_items, True, 0)
    emit(0, 2, True)

    slc_items = [(ks_ref, vst_ref, i, causal_ref, None)]
    for back in range(1, SLC_LEAD_CHUNKS):
        slc_items.append((ks_ref, vst_ref, i - back, None, i >= back))
    slc_items.append((ks_ref, vst_ref, 0, None, i >= SLC_LEAD_CHUNKS))
    attend(1, qa_ref, slc_items, True, n_win_items * GQA_GROUP)

    picked = (rank < n_sel).astype(BF16)
    per_chunk = jnp.sum(_dot(chunk_ref[...], picked), axis=1, keepdims=True)
    for c in range(n_slc * SLC_BLOCK // tk):
        need_ref[c] = (per_chunk[c, 0] > 0.0).astype(jnp.int32)
    last = i - SLC_LEAD_CHUNKS

    def slc_step(pair, carry):
        c0 = 1 + 2 * pair
        c1 = jnp.minimum(c0 + 1, last)
        use0 = need_ref[c0] > 0
        use1 = (c0 + 1 <= last) & (need_ref[c1] > 0)

        @pl.when(use0 | use1)
        def _():
            attend(1, qa_ref, [(ks_ref, vst_ref, c0, None, use0), (ks_ref, vst_ref, c1, None, use1)], False, 0)
        return carry

    lax.fori_loop(0, jnp.maximum(last + 1, 0) // 2, slc_step, 0)
    emit(1, 1, True)


def _nsa_attn(qt, kc, vct, ks, vst, kw, vwt, gt, tq):
    bsz, _, L = qt.shape
    n_cmp_pad = kc.shape[2]
    n_slc = L // SLC_BLOCK
    group_rows = GQA_GROUP * HEAD_DIM
    n = jnp.arange(n_cmp_pad)[None, :]
    j = jnp.arange(n_slc)[:, None]
    overlap_t = ((n * CMP_STRIDE < (j + 1) * SLC_BLOCK) & (n * CMP_STRIDE + CMP_BLOCK - 1 >= j * SLC_BLOCK)).astype(F32)
    assert tq == ATT_TILE and WINDOW % tq == 0 and n_slc % 8 == 0
    n_slots = (WINDOW // tq + 1 + SLC_LEAD_CHUNKS + 1) * GQA_GROUP
    sub = jnp.arange(tq)[:, None]
    lane = jnp.arange(tq)[None, :]
    causal_bias = jnp.where(sub <= lane, 0.0, NEG_BIG).astype(F32)
    far_bias = jnp.where(sub > lane, 0.0, NEG_BIG).astype(F32)
    assert L // tq <= CHUNK_ROWS
    chunk_of_block = (jnp.arange(CHUNK_ROWS)[:, None] == jnp.arange(n_slc)[None, :] * SLC_BLOCK // tq).astype(BF16)
    kern = functools.partial(_nsa_kernel, tq=tq, n_cmp_pad=n_cmp_pad, n_slc=n_slc)
    return pl.pallas_call(
        kern, grid=(bsz, N_KV_HEADS, L // tq),
        in_specs=[
            pl.BlockSpec((1, group_rows, tq), lambda b, h, i: (b, h, i)),
            pl.BlockSpec((1, 1, n_cmp_pad, HEAD_DIM), lambda b, h, i: (b, h, 0, 0)),
            pl.BlockSpec((1, HEAD_DIM, n_cmp_pad), lambda b, h, i: (b, h, 0)),
            pl.BlockSpec((1, 1, L, KEY_LANES), lambda b, h, i: (b, h, 0, 0)),
            pl.BlockSpec((1, VT_ROWS, L), lambda b, h, i: (b, h, 0)),
            pl.BlockSpec((1, 1, L, KEY_LANES), lambda b, h, i: (b, h, 0, 0)),
            pl.BlockSpec((1, VT_ROWS, L), lambda b, h, i: (b, h, 0)),
            pl.BlockSpec((1, GATE_ROWS, tq), lambda b, h, i: (b, h, i)),
            _full((n_slc, n_cmp_pad)),
            _full((tq, tq)), _full((tq, tq)),
            _full((CHUNK_ROWS, n_slc)),
        ],
        out_specs=pl.BlockSpec((1, group_rows, tq), lambda b, h, i: (b, h, i)),
        out_shape=jax.ShapeDtypeStruct((bsz, D_MODEL, L), F32),
        scratch_shapes=[pltpu.VMEM((GQA_GROUP, KEY_LANES, tq), BF16),
                        pltpu.VMEM((GQA_GROUP, KEY_LANES, tq), BF16),
                        pltpu.VMEM((2, GQA_GROUP, 1, tq), F32),
                        pltpu.VMEM((2, GQA_GROUP, VT_ROWS, tq), F32),
                        pltpu.VMEM((n_slots, tq, tq), F32),
                        pltpu.VMEM((n_slots, 1, tq), F32),
                        pltpu.VMEM((GQA_GROUP, 1, tq), F32),
                        pltpu.VMEM((GQA_GROUP, n_cmp_pad, tq), F32),
                        pltpu.VMEM((GQA_GROUP, n_cmp_pad, tq), BF16),
                        pltpu.SMEM((CHUNK_ROWS,), jnp.int32)],
        compiler_params=_cparams("parallel", "parallel", "parallel"), name="nsa_attn",
    )(qt, kc, vct, ks, vst, kw, vwt, gt, overlap_t, causal_bias, far_bias, chunk_of_block)


def _mix_kernel(h_ref, ys_ref, ot_ref, wgm_ref, wno_ref, wout_ref, g_ref, b_ref, h1_o):
    h = h_ref[0]
    o = ot_ref[0].T.astype(BF16)
    y_nsa = _dot(o, wno_ref[...])
    gm = _sigmoid(_dot(h.astype(BF16), wgm_ref[...]))
    mix = gm[:, :D_MODEL] * ys_ref[0] + gm[:, D_MODEL:] * y_nsa
    mixed = _dot(mix.astype(BF16), wout_ref[...])
    h1_o[0] = _layer_norm(DEEPNORM_ALPHA * h + mixed, g_ref[...], b_ref[...])


def _mix_ln1(h, y_ssm, o_t, wgm, wno, wout, g, b, tm):
    bsz, L, d = h.shape
    return pl.pallas_call(
        _mix_kernel, grid=(bsz, L // tm),
        in_specs=[
            pl.BlockSpec((1, tm, d), lambda b, i: (b, i, 0)),
            pl.BlockSpec((1, tm, d), lambda b, i: (b, i, 0)),
            pl.BlockSpec((1, d, tm), lambda b, i: (b, 0, i)),
            _full((d, 2 * d)), _full((d, d)), _full((d, d)), _full((1, d)), _full((1, d)),
        ],
        out_specs=pl.BlockSpec((1, tm, d), lambda b, i: (b, i, 0)),
        out_shape=jax.ShapeDtypeStruct((bsz, L, d), F32),
        compiler_params=_cparams("parallel", "parallel"), name="mix_ln1",
    )(h, y_ssm, o_t, wgm, wno, wout, g.reshape(1, d), b.reshape(1, d))


def _memkv_kernel(mem_ref, w_ref, k_o, v_o):
    kv = _dot(mem_ref[0].astype(BF16), w_ref[...])
    k_o[0] = kv[:, :D_MODEL].astype(BF16)
    v_o[0] = kv[:, D_MODEL:].astype(BF16)


def _mem_kv(mem, w_kv):
    bsz, m, d = mem.shape
    spec = pl.BlockSpec((1, m, d), lambda b: (b, 0, 0))
    return pl.pallas_call(
        _memkv_kernel, grid=(bsz,),
        in_specs=[spec, _full((d, 2 * d))],
        out_specs=[spec, spec],
        out_shape=[jax.ShapeDtypeStruct((bsz, m, d), BF16)] * 2,
        compiler_params=_cparams("parallel"), name="mem_kv",
    )(mem, w_kv)


def _xattn_kernel(h_ref, k_ref, v_ref, wq_ref, wo_ref, g_ref, b_ref, h2_o):
    h = h_ref[0]
    q = (_dot(h.astype(BF16), wq_ref[...]) * (X_HEAD_DIM ** -0.5)).astype(BF16)
    outs = []
    for hd in range(X_HEADS):
        cols = slice(hd * X_HEAD_DIM, (hd + 1) * X_HEAD_DIM)
        s = _dot_nt(q[:, cols], k_ref[0, :, cols])
        p = jnp.exp(s - jnp.max(s, axis=-1, keepdims=True))
        p = p / jnp.sum(p, axis=-1, keepdims=True)
        outs.append(_dot(p.astype(BF16), v_ref[0, :, cols]))
    o = jnp.concatenate(outs, axis=-1).astype(BF16)
    h2_o[0] = _layer_norm(DEEPNORM_ALPHA * h + _dot(o, wo_ref[...]), g_ref[...], b_ref[...])


def _xattn_ln2(h, k, v, wq, wo, g, b, tm):
    bsz, L, d = h.shape
    m = k.shape[1]
    row = pl.BlockSpec((1, tm, d), lambda b, i: (b, i, 0))
    kv = pl.BlockSpec((1, m, d), lambda b, i: (b, 0, 0))
    return pl.pallas_call(
        _xattn_kernel, grid=(bsz, L // tm),
        in_specs=[row, kv, kv, _full((d, d)), _full((d, d)), _full((1, d)), _full((1, d))],
        out_specs=row,
        out_shape=jax.ShapeDtypeStruct((bsz, L, d), F32),
        compiler_params=_cparams("parallel", "parallel"), name="xattn_ln2",
    )(h, k, v, wq, wo, g.reshape(1, d), b.reshape(1, d))


def _ffn_kernel(h_ref, win_ref, wout_ref, g_ref, b_ref, o_ref):
    h = h_ref[...]
    gu = _dot(h.astype(BF16), win_ref[...])
    gate = gu[:, :D_FF]
    act = gate * _sigmoid(gate) * gu[:, D_FF:]
    o_ref[...] = _layer_norm(DEEPNORM_ALPHA * h + _dot(act.astype(BF16), wout_ref[...]), g_ref[...], b_ref[...])


def _ffn_ln3(h, win, wout, g, b, tm):
    rows, d = h.shape
    row = pl.BlockSpec((tm, d), lambda i: (i, 0))
    return pl.pallas_call(
        _ffn_kernel, grid=(rows // tm,),
        in_specs=[row, _full((d, 2 * D_FF)), _full((D_FF, d)), _full((1, d)), _full((1, d))],
        out_specs=row,
        out_shape=jax.ShapeDtypeStruct((rows, d), F32),
        compiler_params=_cparams("parallel"), name="ffn_ln3",
    )(h, win, wout, g.reshape(1, d), b.reshape(1, d))


def _inproj_weights(w_in):
    d = D_MODEL
    c0 = SSM_WIDTH
    c1 = c0 + N_HEADS * HEAD_DIM
    c2 = c1 + 2 * KV_DIM
    c3 = c2 + 2 * KV_DIM
    c4 = c3 + 2 * KV_DIM
    c5 = c4 + 3 * N_HEADS
    values_t = lambda m: jnp.pad(m.T.reshape(N_KV_HEADS, HEAD_DIM, d),
                                 ((0, 0), (0, VT_ROWS - HEAD_DIM), (0, 0))).reshape(N_KV_HEADS * VT_ROWS, d)
    wg = w_in[:, c4:c5].T.reshape(N_KV_HEADS, 3 * GQA_GROUP, d)
    wg = jnp.pad(wg, ((0, 0), (0, GATE_ROWS - 3 * GQA_GROUP), (0, 0))).reshape(N_KV_HEADS * GATE_ROWS, d)
    return {
        "wu": w_in[:, :c0].astype(BF16),
        "wqt": w_in[:, c0:c1].T.astype(BF16),
        "wk": jnp.concatenate([w_in[:, c1:c2], w_in[:, c2:c2 + KV_DIM], w_in[:, c3:c3 + KV_DIM]], axis=1).astype(BF16),
        "wvst": values_t(w_in[:, c2 + KV_DIM:c3]).astype(BF16),
        "wvwt": values_t(w_in[:, c3 + KV_DIM:c4]).astype(BF16),
        "wgt": wg.astype(BF16),
        "wgm": w_in[:, c5:].astype(BF16),
    }


def _s5_weights(bb_re, bb_im, c_re, c_im, d_skip, w_glu, b_glu, w_o):
    per_slab = SSM_LANE_SLAB // SSM_GROUP
    eye = jnp.eye(per_slab, dtype=F32)

    def b_blocks(bb):
        bb = bb.reshape(SSM_SLABS, per_slab, SSM_STATE, SSM_GROUP)
        return jnp.einsum('jgnc,gh->jgchn', bb, eye).reshape(SSM_SLABS, SSM_LANE_SLAB, SSM_STATE_SLAB).astype(BF16)

    def c_blocks(c):
        c = c.astype(F32).reshape(SSM_SLABS, per_slab, SSM_GROUP, SSM_STATE)
        return jnp.einsum('jgcn,gh->jgnhc', c, eye).reshape(SSM_SLABS, SSM_STATE_SLAB, SSM_LANE_SLAB).astype(BF16)

    return {
        "wbre": b_blocks(bb_re), "wbim": b_blocks(bb_im), "cre": c_blocks(c_re), "cim": c_blocks(c_im),
        "d": d_skip.astype(F32).reshape(1, SSM_WIDTH), "wglu": w_glu.astype(BF16),
        "bglu": b_glu.astype(F32).reshape(1, SSM_WIDTH), "wo": w_o.astype(BF16),
    }


def _pick(total, want):
    return want if total % want == 0 else total


def kernel(x, mem, ln_emb_g, ln_emb_b, w_in, ssm_a_re, ssm_a_im, ssm_b_re, ssm_b_im, ssm_c_re, ssm_c_im, ssm_d,
           ssm_log_dt, ssm_w_glu, ssm_b_glu, ssm_w_out, cmp_pos, cmp_w1, cmp_b1, cmp_w2, nsa_w_out, w_out,
           ln1_g, ln1_b, xattn_w_q, xattn_w_kv, xattn_w_o, ln2_g, ln2_b, ffn_w_in, ffn_w_out, ln3_g, ln3_b):
    bsz, L, d = x.shape
    assert w_in.shape[0] == 1, "one layer: the trunk-entry LayerNorm is fused into its input projection"
    l = 0
    tile = _pick(L, 256)
    wi = _inproj_weights(w_in[l])
    h, u, qt, kvc, ks, kw, vst, vwt, gt = _ln_inproj(x, ln_emb_g, ln_emb_b, wi, tile)

    lb_re, lb_im, bb_re, bb_im = _zoh_prep(ssm_a_re[l], ssm_a_im[l], ssm_log_dt[l], ssm_b_re[l], ssm_b_im[l])
    ws = _s5_weights(bb_re, bb_im, ssm_c_re[l], ssm_c_im[l], ssm_d[l], ssm_w_glu[l], ssm_b_glu[l], ssm_w_out[l])
    y_ssm = _s5(u, lb_re, lb_im, ws, _pick(L, 32))

    kc, vct = _compress(kvc, cmp_pos[l].astype(F32), cmp_w1[l].astype(BF16),
                        cmp_b1[l].astype(F32).reshape(2, 1, CMP_HIDDEN),
                        cmp_w2[l, 0].astype(BF16), cmp_w2[l, 1].T.astype(BF16))
    o_t = _nsa_attn(qt, kc, vct, ks, vst, kw, vwt, gt, tile)

    h = _mix_ln1(h, y_ssm, o_t, wi["wgm"], nsa_w_out[l].astype(BF16), w_out[l].astype(BF16),
                 ln1_g[l], ln1_b[l], tile)
    mk, mv = _mem_kv(mem, xattn_w_kv[l].astype(BF16))
    h = _xattn_ln2(h, mk, mv, xattn_w_q[l].astype(BF16), xattn_w_o[l].astype(BF16), ln2_g[l], ln2_b[l], tile)
    h = _ffn_ln3(h.reshape(bsz * L, d), ffn_w_in[l].astype(BF16), ffn_w_out[l].astype(BF16),
                 ln3_g[l], ln3_b[l], tile)
    return h.reshape(bsz, L, d)
```

```python
import functools
import math

import jax
import jax.numpy as jnp
from jax import lax
from jax.experimental import pallas as pl
from jax.experimental.pallas import tpu as pltpu

F32 = jnp.float32
BF16 = jnp.bfloat16

D_MODEL = 1024
SSM_WIDTH = 512
SSM_GROUP = 16
SSM_GROUPS = SSM_WIDTH // SSM_GROUP
SSM_STATE = 64
SSM_STATES = SSM_GROUPS * SSM_STATE
SSM_EIG_CLIP = -1e-4
N_HEADS = 16
N_KV_HEADS = 4
HEAD_DIM = 64
GQA_GROUP = N_HEADS // N_KV_HEADS
KV_DIM = N_KV_HEADS * HEAD_DIM
CMP_BLOCK = 32
CMP_STRIDE = 16
CMP_HIDDEN = 256
SLC_BLOCK = 64
SLC_TOP_N = 8
WINDOW = 512
SEL_BIG = 1e9
X_HEADS = 4
X_HEAD_DIM = D_MODEL // X_HEADS
D_FF = 2816
DEEPNORM_ALPHA = 2.0 ** 0.25
LN_EPS = 1e-5
NEG_BIG = -1e30
GATE_ROWS = 16
LOG2E = 1.4426950408889634
ATT_TILE = 256
ROW_TILE = 512
KEY_LANES = 128
ALIBI_LANE = HEAD_DIM
ONEHOT_LANE = HEAD_DIM + 8
VT_ROWS = 80
SLC_LEAD_CHUNKS = 3
CHUNK_ROWS = 16

V7X_VMEM_LIMIT_BYTES = 56 * 1024 * 1024
SSM_LANE_SLAB = 128
SSM_SLABS = SSM_WIDTH // SSM_LANE_SLAB
SSM_STATE_SLAB = SSM_STATES // SSM_SLABS


def _cparams(*sem):
    return pltpu.CompilerParams(dimension_semantics=sem, vmem_limit_bytes=V7X_VMEM_LIMIT_BYTES)


def _full(shape):
    zeros = (0,) * len(shape)
    return pl.BlockSpec(shape, lambda *_: zeros, pipeline_mode=pl.Buffered(1))


def _layer_norm(x, g, b):
    mu = jnp.mean(x, axis=-1, keepdims=True)
    xc = x - mu
    var = jnp.mean(xc * xc, axis=-1, keepdims=True)
    return xc * lax.rsqrt(var + LN_EPS) * g + b


def _gelu_tanh(x):
    return 0.5 * x * (1.0 + jnp.tanh(math.sqrt(2.0 / math.pi) * (x + 0.044715 * (x * x * x))))


def _sigmoid(x):
    return 1.0 / (1.0 + jnp.exp(-x))


def _dot(a, b):
    return jnp.dot(a, b, preferred_element_type=F32)


def _dot_nt(a, b):
    return lax.dot_general(a, b, (((1,), (1,)), ((), ())), preferred_element_type=F32)


def _zoh_kernel(a_re, a_im, log_dt, b_re, b_im, lb_re_o, lb_im_o, bb_re_o, bb_im_o):
    lam_re = jnp.minimum(a_re[...], SSM_EIG_CLIP)
    lam_im = a_im[...]
    dt = jnp.exp(log_dt[...])
    mag = jnp.exp(lam_re * dt)
    lb_re = mag * jnp.cos(lam_im * dt)
    lb_im = mag * jnp.sin(lam_im * dt)
    den = lam_re * lam_re + lam_im * lam_im
    nr = lb_re - 1.0
    f_re = (nr * lam_re + lb_im * lam_im) / den
    f_im = (lb_im * lam_re - nr * lam_im) / den
    br = b_re[...]
    bi = b_im[...]
    lb_re_o[...] = lb_re
    lb_im_o[...] = lb_im
    bb_re_o[...] = f_re * br - f_im * bi
    bb_im_o[...] = f_re * bi + f_im * br


def _zoh_prep(a_re, a_im, log_dt, b_re, b_im):
    gn = SSM_STATES
    col = lambda v: v.astype(F32).reshape(gn, 1)
    dt_col = jnp.broadcast_to(log_dt.astype(F32)[:, None], (SSM_GROUPS, SSM_STATE)).reshape(gn, 1)
    outs = pl.pallas_call(
        _zoh_kernel,
        out_shape=[jax.ShapeDtypeStruct((gn, 1), F32)] * 2 + [jax.ShapeDtypeStruct((gn, SSM_GROUP), F32)] * 2,
        name="zoh_prep",
    )(col(a_re), col(a_im), dt_col, b_re.astype(F32).reshape(gn, SSM_GROUP), b_im.astype(F32).reshape(gn, SSM_GROUP))
    lb_re, lb_im, bb_re, bb_im = outs
    shape_b = (SSM_GROUPS, SSM_STATE, SSM_GROUP)
    return lb_re.reshape(1, gn), lb_im.reshape(1, gn), bb_re.reshape(shape_b), bb_im.reshape(shape_b)


def _inproj_kernel(x_ref, g_ref, b_ref, wu_ref, wqt_ref, wk_ref, wvst_ref, wvwt_ref, wgt_ref,
                   h_o, u_o, qt_o, kvc_o, ks_o, kw_o, vst_o, vwt_o, gt_o):
    tl = x_ref.shape[1]
    h = _layer_norm(x_ref[0], g_ref[...], b_ref[...])
    h_o[0] = h
    hb = h.astype(BF16)
    u_o[0] = _dot(hb, wu_ref[...])
    qt_o[0] = (_dot_nt(wqt_ref[...], hb) * (HEAD_DIM ** -0.5 * LOG2E)).astype(BF16)

    k_all = _dot(hb, wk_ref[...])

    def head_tile(j):
        tile = k_all[:, (j // 2) * KEY_LANES:(j // 2 + 1) * KEY_LANES]
        return pltpu.roll(tile, HEAD_DIM, 1) if j % 2 else tile

    for j in range(2 * N_KV_HEADS):
        kvc_o[0, j] = head_tile(j)[:, :HEAD_DIM]

    lane = lax.broadcasted_iota(jnp.int32, (tl, KEY_LANES), 1)
    pos = pl.program_id(1) * tl + lax.broadcasted_iota(jnp.int32, (tl, KEY_LANES), 0)
    alibi = jnp.where((lane >= ALIBI_LANE) & (lane < ALIBI_LANE + 3), (pos % ATT_TILE).astype(F32), 0.0)
    slc_feat = alibi + jnp.where((lane >= ONEHOT_LANE) & (lane - ONEHOT_LANE == pos // SLC_BLOCK), 1.0, 0.0)
    is_key = lane < HEAD_DIM
    for j in range(N_KV_HEADS):
        ks_o[0, j] = jnp.where(is_key, head_tile(2 * N_KV_HEADS + j), slc_feat).astype(BF16)
        kw_o[0, j] = jnp.where(is_key, head_tile(3 * N_KV_HEADS + j), alibi).astype(BF16)
    row = lax.broadcasted_iota(jnp.int32, (N_KV_HEADS * VT_ROWS, tl), 0)
    ones_row = jnp.where(row % VT_ROWS == HEAD_DIM, 1.0, 0.0)
    vst_o[0] = (_dot_nt(wvst_ref[...], hb) + ones_row).astype(BF16)
    vwt_o[0] = (_dot_nt(wvwt_ref[...], hb) + ones_row).astype(BF16)
    gt_o[0] = _sigmoid(_dot_nt(wgt_ref[...], hb))


def _ln_inproj(x, ln_g, ln_b, w, tl):
    bsz, L, d = x.shape
    n_gate = N_KV_HEADS * GATE_ROWS
    grid = (bsz, L // tl)
    in_specs = [
        pl.BlockSpec((1, tl, d), lambda b, i: (b, i, 0)),
        _full((1, d)), _full((1, d)),
        _full((d, SSM_WIDTH)),
        _full((D_MODEL, d)),
        _full((d, 4 * KV_DIM)),
        _full((N_KV_HEADS * VT_ROWS, d)),
        _full((N_KV_HEADS * VT_ROWS, d)),
        _full((n_gate, d)),
    ]
    assert tl % ATT_TILE == 0 or ATT_TILE % tl == 0
    assert ONEHOT_LANE + L // SLC_BLOCK <= KEY_LANES
    out_shape = [
        jax.ShapeDtypeStruct((bsz, L, d), F32),
        jax.ShapeDtypeStruct((bsz, L, SSM_WIDTH), F32),
        jax.ShapeDtypeStruct((bsz, D_MODEL, L), BF16),
        jax.ShapeDtypeStruct((bsz, 2 * N_KV_HEADS, L, HEAD_DIM), F32),
        jax.ShapeDtypeStruct((bsz, N_KV_HEADS, L, KEY_LANES), BF16),
        jax.ShapeDtypeStruct((bsz, N_KV_HEADS, L, KEY_LANES), BF16),
        jax.ShapeDtypeStruct((bsz, N_KV_HEADS * VT_ROWS, L), BF16),
        jax.ShapeDtypeStruct((bsz, N_KV_HEADS * VT_ROWS, L), BF16),
        jax.ShapeDtypeStruct((bsz, n_gate, L), F32),
    ]
    out_specs = [
        pl.BlockSpec((1, tl, d), lambda b, i: (b, i, 0)),
        pl.BlockSpec((1, tl, SSM_WIDTH), lambda b, i: (b, i, 0)),
        pl.BlockSpec((1, D_MODEL, tl), lambda b, i: (b, 0, i)),
        pl.BlockSpec((1, 2 * N_KV_HEADS, tl, HEAD_DIM), lambda b, i: (b, 0, i, 0)),
        pl.BlockSpec((1, N_KV_HEADS, tl, KEY_LANES), lambda b, i: (b, 0, i, 0)),
        pl.BlockSpec((1, N_KV_HEADS, tl, KEY_LANES), lambda b, i: (b, 0, i, 0)),
        pl.BlockSpec((1, N_KV_HEADS * VT_ROWS, tl), lambda b, i: (b, 0, i)),
        pl.BlockSpec((1, N_KV_HEADS * VT_ROWS, tl), lambda b, i: (b, 0, i)),
        pl.BlockSpec((1, n_gate, tl), lambda b, i: (b, 0, i)),
    ]
    return pl.pallas_call(
        _inproj_kernel, grid=grid, in_specs=in_specs, out_specs=out_specs, out_shape=out_shape,
        compiler_params=_cparams("parallel", "parallel"), name="ln_inproj",
    )(x, ln_g.reshape(1, d), ln_b.reshape(1, d), w["wu"], w["wqt"], w["wk"],
      w["wvst"], w["wvwt"], w["wgt"])


def _s5_kernel(u_ref, lre_ref, lim_ref, wbre_ref, wbim_ref, cre_ref, cim_ref, d_ref, wglu_ref, bglu_ref, wo_ref,
               y_o, sre, sim, hre, him, *, bsz, steps, pitch):
    @pl.when(pl.program_id(0) == 0)
    def _():
        sre[...] = jnp.zeros_like(sre)
        sim[...] = jnp.zeros_like(sim)

    lanes = SSM_LANE_SLAB
    per_slab = SSM_STATE_SLAB // lanes
    u = u_ref[...].reshape(bsz * steps, SSM_WIDTH)
    ub = u.astype(BF16)
    for j in range(SSM_SLABS):
        uj = ub[:, j * lanes:(j + 1) * lanes]
        for w_ref, h_ref in ((wbre_ref, hre), (wbim_ref, him)):
            r = _dot(uj, w_ref[j])
            for b in range(bsz):
                for k in range(per_slab):
                    h_ref[j * per_slab + k, pl.ds(b, steps, stride=pitch), :] = (
                        r[b * steps:(b + 1) * steps, k * lanes:(k + 1) * lanes])

    for j in range(SSM_SLABS):
        slabs = [j * per_slab + k for k in range(per_slab)]
        lr = [jnp.broadcast_to(lre_ref[:, s * lanes:(s + 1) * lanes], (bsz, lanes)) for s in slabs]
        li = [jnp.broadcast_to(lim_ref[:, s * lanes:(s + 1) * lanes], (bsz, lanes)) for s in slabs]

        def step(t, carry, slabs=slabs, lr=lr, li=li):
            rows = pl.ds(pl.multiple_of(t * pitch, 8), bsz)
            new = []
            for k, s in enumerate(slabs):
                pr, pi = carry[2 * k], carry[2 * k + 1]
                nr = lr[k] * pr - li[k] * pi + hre[s, rows, :]
                ni = lr[k] * pi + li[k] * pr + him[s, rows, :]
                hre[s, rows, :] = nr
                him[s, rows, :] = ni
                new += [nr, ni]
            return tuple(new)

        init = []
        for s in slabs:
            init += [sre[:, s * lanes:(s + 1) * lanes], sim[:, s * lanes:(s + 1) * lanes]]
        fin = lax.fori_loop(0, steps, step, tuple(init), unroll=4)
        for k, s in enumerate(slabs):
            sre[:, s * lanes:(s + 1) * lanes] = fin[2 * k]
            sim[:, s * lanes:(s + 1) * lanes] = fin[2 * k + 1]

    def states(h_ref, j):
        return jnp.concatenate(
            [jnp.concatenate([h_ref[j * per_slab + k, pl.ds(b, steps, stride=pitch), :] for k in range(per_slab)],
                             axis=-1) for b in range(bsz)], axis=0)

    ys = []
    for j in range(SSM_SLABS):
        ys.append(_dot(states(hre, j).astype(BF16), cre_ref[j]) - _dot(states(him, j).astype(BF16), cim_ref[j]))
    y = jnp.concatenate(ys, axis=-1) + d_ref[...] * u
    g = _gelu_tanh(y)
    y2 = g * _sigmoid(_dot(g.astype(BF16), wglu_ref[...]) + bglu_ref[...])
    y_o[...] = _dot(y2.astype(BF16), wo_ref[...]).reshape(bsz, steps, D_MODEL)


def _s5(u, lb_re, lb_im, w, steps):
    bsz, L, _ = u.shape
    pitch = -(-bsz // 8) * 8
    pitch += 8 if (pitch // 8) % 2 == 0 else 0
    grid = (L // steps,)
    kern = functools.partial(_s5_kernel, bsz=bsz, steps=steps, pitch=pitch)
    n_slabs = SSM_STATES // SSM_LANE_SLAB
    in_specs = [
        pl.BlockSpec((bsz, steps, SSM_WIDTH), lambda c: (0, c, 0)),
        _full((1, SSM_STATES)), _full((1, SSM_STATES)),
        _full((SSM_SLABS, SSM_LANE_SLAB, SSM_STATE_SLAB)), _full((SSM_SLABS, SSM_LANE_SLAB, SSM_STATE_SLAB)),
        _full((SSM_SLABS, SSM_STATE_SLAB, SSM_LANE_SLAB)), _full((SSM_SLABS, SSM_STATE_SLAB, SSM_LANE_SLAB)),
        _full((1, SSM_WIDTH)),
        _full((SSM_WIDTH, SSM_WIDTH)), _full((1, SSM_WIDTH)),
        _full((SSM_WIDTH, D_MODEL)),
    ]
    return pl.pallas_call(
        kern, grid=grid, in_specs=in_specs,
        out_specs=pl.BlockSpec((bsz, steps, D_MODEL), lambda c: (0, c, 0)),
        out_shape=jax.ShapeDtypeStruct((bsz, L, D_MODEL), F32),
        scratch_shapes=[pltpu.VMEM((bsz, SSM_STATES), F32), pltpu.VMEM((bsz, SSM_STATES), F32),
                        pltpu.VMEM((n_slabs, steps * pitch, SSM_LANE_SLAB), F32),
                        pltpu.VMEM((n_slabs, steps * pitch, SSM_LANE_SLAB), F32)],
        compiler_params=_cparams("arbitrary"), name="s5",
    )(u, lb_re, lb_im, w["wbre"], w["wbim"], w["cre"], w["cim"], w["d"], w["wglu"], w["bglu"], w["wo"])


def _compress_kernel(kv_ref, pos_ref, w1_ref, b1_ref, w2k_ref, w2vt_ref, kc_o, vct_o, *, n_chunks):
    half = CMP_BLOCK // 2
    for z in range(2):
        for hh in range(N_KV_HEADS):
            j = z * N_KV_HEADS + hh
            first = jnp.zeros((n_chunks, CMP_HIDDEN), F32)
            second = jnp.zeros((n_chunks, CMP_HIDDEN), F32)
            for p in range(half):
                xp = kv_ref[0, j, pl.ds(p, n_chunks, stride=CMP_STRIDE), :]
                first += _dot((xp + pos_ref[z, p:p + 1, :]).astype(BF16), w1_ref[z, p * HEAD_DIM:(p + 1) * HEAD_DIM, :])
                q = half + p
                second += _dot((xp + pos_ref[z, q:q + 1, :]).astype(BF16), w1_ref[z, q * HEAD_DIM:(q + 1) * HEAD_DIM, :])
            pre = first + pltpu.roll(second, n_chunks - 1, 0) + b1_ref[z]
            hid = _gelu_tanh(pre).astype(BF16)
            if z == 0:
                kc_o[0, hh] = _dot(hid, w2k_ref[...]).astype(BF16)
            else:
                vct_o[0, hh * HEAD_DIM:(hh + 1) * HEAD_DIM, :] = _dot_nt(w2vt_ref[...], hid).astype(BF16)


def _compress(kvc, pos, w1, b1, w2k, w2vt):
    bsz, _, L, _ = kvc.shape
    n_chunks = L // CMP_STRIDE
    kern = functools.partial(_compress_kernel, n_chunks=n_chunks)
    return pl.pallas_call(
        kern, grid=(bsz,),
        in_specs=[
            pl.BlockSpec((1, 2 * N_KV_HEADS, L, HEAD_DIM), lambda b: (b, 0, 0, 0)),
            _full((2, CMP_BLOCK, HEAD_DIM)),
            _full((2, CMP_BLOCK * HEAD_DIM, CMP_HIDDEN)),
            _full((2, 1, CMP_HIDDEN)),
            _full((CMP_HIDDEN, HEAD_DIM)),
            _full((HEAD_DIM, CMP_HIDDEN)),
        ],
        out_specs=[
            pl.BlockSpec((1, N_KV_HEADS, n_chunks, HEAD_DIM), lambda b: (b, 0, 0, 0)),
            pl.BlockSpec((1, KV_DIM, n_chunks), lambda b: (b, 0, 0)),
        ],
        out_shape=[
            jax.ShapeDtypeStruct((bsz, N_KV_HEADS, n_chunks, HEAD_DIM), BF16),
            jax.ShapeDtypeStruct((bsz, KV_DIM, n_chunks), BF16),
        ],
        compiler_params=_cparams("parallel"), name="compress",
    )(kvc, pos, w1, b1, w2k, w2vt)


def _nsa_kernel(*refs, tq, n_cmp_pad, n_slc):
    def one_head(hkv, carry):
        _nsa_head(hkv, *refs, tq=tq, n_cmp_pad=n_cmp_pad, n_slc=n_slc)
        return carry

    lax.fori_loop(0, N_KV_HEADS, one_head, 0)


def _nsa_head(hkv, qt_ref, kc_ref, vct_ref, ks_ref, vst_ref, kw_ref, vwt_ref, gt_ref, ov_ref, causal_ref, far_ref,
              chunk_ref, o_ref, qa_ref, qw_ref, m_ref, acc_ref, s_ref, mx_ref, al_ref, sc_ref, pc_ref, need_ref,
              *, tq, n_cmp_pad, n_slc):
    i = pl.program_id(1)
    t0 = i * tq
    tk = tq
    n_sel = min(SLC_TOP_N, n_slc)
    slopes = [jnp.exp2(jnp.full((1, tq), -0.5, F32) * (hkv * GQA_GROUP + g + 1).astype(F32)) * LOG2E
              for g in range(GQA_GROUP)]
    head_rows = lambda g: pl.ds(pl.multiple_of((hkv * GQA_GROUP + g) * HEAD_DIM, HEAD_DIM), HEAD_DIM)
    vt_rows = pl.ds(pl.multiple_of(hkv * VT_ROWS, 16), VT_ROWS)
    q_heads = [qt_ref[0, head_rows(g), :] for g in range(GQA_GROUP)]
    gate = lambda g, z: gt_ref[0, pl.ds(hkv * GATE_ROWS + 3 * g + z, 1), :]

    t_lane = t0 + lax.broadcasted_iota(jnp.int32, (1, tq), 1)
    row8 = lax.broadcasted_iota(jnp.int32, (ONEHOT_LANE - ALIBI_LANE, tq), 0)

    def augmented(g, tail_rows):
        hi = slopes[g].astype(BF16).astype(F32)
        mid = (slopes[g] - hi).astype(BF16).astype(F32)
        lo = slopes[g] - hi - mid
        parts = jnp.where(row8 == 0, hi, jnp.where(row8 == 1, mid, jnp.where(row8 == 2, lo, 0.0)))
        return jnp.concatenate([q_heads[g].astype(F32), parts, tail_rows], axis=0).astype(BF16)

    def attend(state, q_ref, items, first, slot0):
        starts = [pl.multiple_of((c if valid is None else jnp.maximum(c, 0)) * tk, tk) for _, _, c, _, valid in items]
        for g in range(GQA_GROUP):
            for n, (k_ref, _, _, bias_ref, _) in enumerate(items):
                s = _dot(k_ref[0, hkv, pl.ds(starts[n], tk), :], q_ref[g])
                if bias_ref is not None:
                    s = s + bias_ref[...]
                s_ref[slot0 + n * GQA_GROUP + g] = s
                mx_ref[slot0 + n * GQA_GROUP + g] = jnp.max(s, axis=0, keepdims=True)
        for g in range(GQA_GROUP):
            shifts = []
            for _, _, c, _, valid in items:
                shift = slopes[g] * ((c - i) * tk).astype(F32)
                shifts.append(shift if valid is None else jnp.where(valid, shift, NEG_BIG))
            m_new = functools.reduce(jnp.maximum, [mx_ref[slot0 + n * GQA_GROUP + g] + shifts[n]
                                                   for n in range(len(items))])
            if not first:
                m_new = jnp.maximum(m_ref[state, g], m_new)
                al_ref[g] = jnp.exp2(m_ref[state, g] - m_new)
            pv = None
            for n, (_, vt_ref, _, _, _) in enumerate(items):
                slot = slot0 + n * GQA_GROUP + g
                p = jnp.exp2(s_ref[slot] - (m_new - shifts[n])).astype(BF16)
                part = _dot(vt_ref[0, vt_rows, pl.ds(starts[n], tk)], p)
                pv = part if pv is None else pv + part
            acc_ref[state, g] = pv if first else al_ref[g] * acc_ref[state, g] + pv
            m_ref[state, g] = m_new

    def emit(state, z, accumulate):
        for g in range(GQA_GROUP):
            out = gate(g, z) * (acc_ref[state, g, :HEAD_DIM, :] / acc_ref[state, g, HEAD_DIM:HEAD_DIM + 1, :])
            o_ref[0, head_rows(g), :] = o_ref[0, head_rows(g), :] + out if accumulate else out

    n_idx = lax.broadcasted_iota(jnp.int32, (n_cmp_pad, tq), 0)
    dist_c = (t0 + lax.broadcasted_iota(jnp.int32, (n_cmp_pad, tq), 1)) - (n_idx * CMP_STRIDE + (CMP_BLOCK - 1))
    mask_c = dist_c >= 0
    dist_cf = dist_c.astype(F32)
    kc = kc_ref[0, hkv]
    vct = vct_ref[0, pl.ds(pl.multiple_of(hkv * HEAD_DIM, HEAD_DIM), HEAD_DIM), :]
    for g in range(GQA_GROUP):
        s = _dot(kc, q_heads[g]) - slopes[g] * dist_cf
        sc_ref[g] = jnp.where(mask_c, s, NEG_BIG)
    p_sum = jnp.zeros((n_cmp_pad, tq), F32)
    for g in range(GQA_GROUP):
        s = sc_ref[g]
        p = jnp.where(mask_c, jnp.exp2(s - jnp.max(s, axis=0, keepdims=True)), 0.0)
        p = p / jnp.maximum(jnp.sum(p, axis=0, keepdims=True), 1e-30)
        pc_ref[g] = p.astype(BF16)
        p_sum = p_sum + p
    for g in range(GQA_GROUP):
        o_ref[0, head_rows(g), :] = gate(g, 0) * _dot(vct, pc_ref[g])

    imp = lax.dot_general(ov_ref[...], p_sum, (((1,), (0,)), ((), ())), precision=lax.Precision.HIGHEST,
                          preferred_element_type=F32)
    blk = lax.broadcasted_iota(jnp.int32, (n_slc, tq), 0)
    cur = t_lane // SLC_BLOCK
    forced = (blk == 0) | (blk == cur) | (blk == cur - 1)
    future = blk * SLC_BLOCK > t_lane
    imp = jnp.where(forced, SEL_BIG, jnp.where(future, -SEL_BIG, imp))
    rank = jnp.zeros((n_slc, tq), jnp.int32)
    for j in range(n_slc):
        row = imp[j:j + 1, :]
        beats = (row > imp) | ((row == imp) & (blk > j))
        rank = rank + beats.astype(jnp.int32)
    sel_bias = jnp.where(rank < n_sel, 0.0, NEG_BIG)

    qa_rows = jnp.concatenate([sel_bias, jnp.zeros((KEY_LANES - ONEHOT_LANE - n_slc, tq), F32)], axis=0)
    for g in range(GQA_GROUP):
        qa_ref[g] = augmented(g, qa_rows)

    zero_rows = jnp.zeros((KEY_LANES - ONEHOT_LANE, tq), F32)
    for g in range(GQA_GROUP):
        qw_ref[g] = augmented(g, zero_rows)
    n_back = WINDOW // tk
    n_win_items = n_back + 1
    win_items = [(kw_ref, vwt_ref, i, causal_ref, None)]
    for back in range(1, n_back + 1):
        win_items.append((kw_ref, vwt_ref, i - back, far_ref if back == n_back else None, i >= back))
    attend(0, qw_ref, win_items, True, 0)
    emit(0, 2, True)

    slc_items = [(ks_ref, vst_ref, i, causal_ref, None)]
    for back in range(1, SLC_LEAD_CHUNKS):
        slc_items.append((ks_ref, vst_ref, i - back, None, i >= back))
    slc_items.append((ks_ref, vst_ref, 0, None, i >= SLC_LEAD_CHUNKS))
    attend(1, qa_ref, slc_items, True, n_win_items * GQA_GROUP)

    picked = (rank < n_sel).astype(BF16)
    per_chunk = jnp.sum(_dot(chunk_ref[...], picked), axis=1, keepdims=True)
    for c in range(n_slc * SLC_BLOCK // tk):
        need_ref[c] = (per_chunk[c, 0] > 0.0).astype(jnp.int32)
    last = i - SLC_LEAD_CHUNKS

    def slc_step(pair, carry):
        c0 = 1 + 2 * pair
        c1 = jnp.minimum(c0 + 1, last)
        use0 = need_ref[c0] > 0
        use1 = (c0 + 1 <= last) & (need_ref[c1] > 0)

        @pl.when(use0 | use1)
        def _():
            attend(1, qa_ref, [(ks_ref, vst_ref, c0, None, use0), (ks_ref, vst_ref, c1, None, use1)], False, 0)
        return carry

    lax.fori_loop(0, jnp.maximum(last + 1, 0) // 2, slc_step, 0)
    emit(1, 1, True)


def _nsa_attn(qt, kc, vct, ks, vst, kw, vwt, gt, tq):
    bsz, _, L = qt.shape
    n_cmp_pad = kc.shape[2]
    n_slc = L // SLC_BLOCK
    n = jnp.arange(n_cmp_pad)[None, :]
    j = jnp.arange(n_slc)[:, None]
    overlap_t = ((n * CMP_STRIDE < (j + 1) * SLC_BLOCK) & (n * CMP_STRIDE + CMP_BLOCK - 1 >= j * SLC_BLOCK)).astype(F32)
    assert tq == ATT_TILE and WINDOW % tq == 0 and n_slc % 8 == 0
    n_slots = (WINDOW // tq + 1 + SLC_LEAD_CHUNKS + 1) * GQA_GROUP
    sub = jnp.arange(tq)[:, None]
    lane = jnp.arange(tq)[None, :]
    causal_bias = jnp.where(sub <= lane, 0.0, NEG_BIG).astype(F32)
    far_bias = jnp.where(sub > lane, 0.0, NEG_BIG).astype(F32)
    assert L // tq <= CHUNK_ROWS
    chunk_of_block = (jnp.arange(CHUNK_ROWS)[:, None] == jnp.arange(n_slc)[None, :] * SLC_BLOCK // tq).astype(BF16)
    kern = functools.partial(_nsa_kernel, tq=tq, n_cmp_pad=n_cmp_pad, n_slc=n_slc)
    return pl.pallas_call(
        kern, grid=(bsz, L // tq),
        in_specs=[
            pl.BlockSpec((1, D_MODEL, tq), lambda b, i: (b, 0, i)),
            pl.BlockSpec((1, N_KV_HEADS, n_cmp_pad, HEAD_DIM), lambda b, i: (b, 0, 0, 0)),
            pl.BlockSpec((1, KV_DIM, n_cmp_pad), lambda b, i: (b, 0, 0)),
            pl.BlockSpec((1, N_KV_HEADS, L, KEY_LANES), lambda b, i: (b, 0, 0, 0)),
            pl.BlockSpec((1, N_KV_HEADS * VT_ROWS, L), lambda b, i: (b, 0, 0)),
            pl.BlockSpec((1, N_KV_HEADS, L, KEY_LANES), lambda b, i: (b, 0, 0, 0)),
            pl.BlockSpec((1, N_KV_HEADS * VT_ROWS, L), lambda b, i: (b, 0, 0)),
            pl.BlockSpec((1, N_KV_HEADS * GATE_ROWS, tq), lambda b, i: (b, 0, i)),
            _full((n_slc, n_cmp_pad)),
            _full((tq, tq)), _full((tq, tq)),
            _full((CHUNK_ROWS, n_slc)),
        ],
        out_specs=pl.BlockSpec((1, D_MODEL, tq), lambda b, i: (b, 0, i)),
        out_shape=jax.ShapeDtypeStruct((bsz, D_MODEL, L), F32),
        scratch_shapes=[pltpu.VMEM((GQA_GROUP, KEY_LANES, tq), BF16),
                        pltpu.VMEM((GQA_GROUP, KEY_LANES, tq), BF16),
                        pltpu.VMEM((2, GQA_GROUP, 1, tq), F32),
                        pltpu.VMEM((2, GQA_GROUP, VT_ROWS, tq), F32),
                        pltpu.VMEM((n_slots, tq, tq), F32),
                        pltpu.VMEM((n_slots, 1, tq), F32),
                        pltpu.VMEM((GQA_GROUP, 1, tq), F32),
                        pltpu.VMEM((GQA_GROUP, n_cmp_pad, tq), F32),
                        pltpu.VMEM((GQA_GROUP, n_cmp_pad, tq), BF16),
                        pltpu.SMEM((CHUNK_ROWS,), jnp.int32)],
        compiler_params=_cparams("parallel", "parallel"), name="nsa_attn",
    )(qt, kc, vct, ks, vst, kw, vwt, gt, overlap_t, causal_bias, far_bias, chunk_of_block)


def _mix_kernel(h_ref, ys_ref, ot_ref, wgm_ref, wno_ref, wout_ref, g_ref, b_ref, h1_o):
    h = h_ref[0]
    o = ot_ref[0].T.astype(BF16)
    y_nsa = _dot(o, wno_ref[...])
    gm = _sigmoid(_dot(h.astype(BF16), wgm_ref[...]))
    mix = gm[:, :D_MODEL] * ys_ref[0] + gm[:, D_MODEL:] * y_nsa
    mixed = _dot(mix.astype(BF16), wout_ref[...])
    h1_o[0] = _layer_norm(DEEPNORM_ALPHA * h + mixed, g_ref[...], b_ref[...])


def _mix_ln1(h, y_ssm, o_t, wgm, wno, wout, g, b, tm):
    bsz, L, d = h.shape
    return pl.pallas_call(
        _mix_kernel, grid=(bsz, L // tm),
        in_specs=[
            pl.BlockSpec((1, tm, d), lambda b, i: (b, i, 0)),
            pl.BlockSpec((1, tm, d), lambda b, i: (b, i, 0)),
            pl.BlockSpec((1, d, tm), lambda b, i: (b, 0, i)),
            _full((d, 2 * d)), _full((d, d)), _full((d, d)), _full((1, d)), _full((1, d)),
        ],
        out_specs=pl.BlockSpec((1, tm, d), lambda b, i: (b, i, 0)),
        out_shape=jax.ShapeDtypeStruct((bsz, L, d), F32),
        compiler_params=_cparams("parallel", "parallel"), name="mix_ln1",
    )(h, y_ssm, o_t, wgm, wno, wout, g.reshape(1, d), b.reshape(1, d))


def _memkv_kernel(mem_ref, w_ref, k_o, v_o):
    kv = _dot(mem_ref[0].astype(BF16), w_ref[...])
    k_o[0] = kv[:, :D_MODEL].astype(BF16)
    v_o[0] = kv[:, D_MODEL:].astype(BF16)


def _mem_kv(mem, w_kv):
    bsz, m, d = mem.shape
    spec = pl.BlockSpec((1, m, d), lambda b: (b, 0, 0))
    return pl.pallas_call(
        _memkv_kernel, grid=(bsz,),
        in_specs=[spec, _full((d, 2 * d))],
        out_specs=[spec, spec],
        out_shape=[jax.ShapeDtypeStruct((bsz, m, d), BF16)] * 2,
        compiler_params=_cparams("parallel"), name="mem_kv",
    )(mem, w_kv)


def _xattn_kernel(h_ref, k_ref, v_ref, wq_ref, wo_ref, g_ref, b_ref, h2_o):
    h = h_ref[0]
    q = (_dot(h.astype(BF16), wq_ref[...]) * (X_HEAD_DIM ** -0.5)).astype(BF16)
    outs = []
    for hd in range(X_HEADS):
        cols = slice(hd * X_HEAD_DIM, (hd + 1) * X_HEAD_DIM)
        s = _dot_nt(q[:, cols], k_ref[0, :, cols])
        p = jnp.exp(s - jnp.max(s, axis=-1, keepdims=True))
        p = p / jnp.sum(p, axis=-1, keepdims=True)
        outs.append(_dot(p.astype(BF16), v_ref[0, :, cols]))
    o = jnp.concatenate(outs, axis=-1).astype(BF16)
    h2_o[0] = _layer_norm(DEEPNORM_ALPHA * h + _dot(o, wo_ref[...]), g_ref[...], b_ref[...])


def _xattn_ln2(h, k, v, wq, wo, g, b, tm):
    bsz, L, d = h.shape
    m = k.shape[1]
    row = pl.BlockSpec((1, tm, d), lambda b, i: (b, i, 0))
    kv = pl.BlockSpec((1, m, d), lambda b, i: (b, 0, 0))
    return pl.pallas_call(
        _xattn_kernel, grid=(bsz, L // tm),
        in_specs=[row, kv, kv, _full((d, d)), _full((d, d)), _full((1, d)), _full((1, d))],
        out_specs=row,
        out_shape=jax.ShapeDtypeStruct((bsz, L, d), F32),
        compiler_params=_cparams("parallel", "parallel"), name="xattn_ln2",
    )(h, k, v, wq, wo, g.reshape(1, d), b.reshape(1, d))


def _ffn_kernel(h_ref, win_ref, wout_ref, g_ref, b_ref, o_ref):
    h = h_ref[...]
    gu = _dot(h.astype(BF16), win_ref[...])
    gate = gu[:, :D_FF]
    act = gate * _sigmoid(gate) * gu[:, D_FF:]
    o_ref[...] = _layer_norm(DEEPNORM_ALPHA * h + _dot(act.astype(BF16), wout_ref[...]), g_ref[...], b_ref[...])


def _ffn_ln3(h, win, wout, g, b, tm):
    rows, d = h.shape
    row = pl.BlockSpec((tm, d), lambda i: (i, 0))
    return pl.pallas_call(
        _ffn_kernel, grid=(rows // tm,),
        in_specs=[row, _full((d, 2 * D_FF)), _full((D_FF, d)), _full((1, d)), _full((1, d))],
        out_specs=row,
        out_shape=jax.ShapeDtypeStruct((rows, d), F32),
        compiler_params=_cparams("parallel"), name="ffn_ln3",
    )(h, win, wout, g.reshape(1, d), b.reshape(1, d))


def _inproj_weights(w_in):
    d = D_MODEL
    c0 = SSM_WIDTH
    c1 = c0 + N_HEADS * HEAD_DIM
    c2 = c1 + 2 * KV_DIM
    c3 = c2 + 2 * KV_DIM
    c4 = c3 + 2 * KV_DIM
    c5 = c4 + 3 * N_HEADS
    values_t = lambda m: jnp.pad(m.T.reshape(N_KV_HEADS, HEAD_DIM, d),
                                 ((0, 0), (0, VT_ROWS - HEAD_DIM), (0, 0))).reshape(N_KV_HEADS * VT_ROWS, d)
    wg = w_in[:, c4:c5].T.reshape(N_KV_HEADS, 3 * GQA_GROUP, d)
    wg = jnp.pad(wg, ((0, 0), (0, GATE_ROWS - 3 * GQA_GROUP), (0, 0))).reshape(N_KV_HEADS * GATE_ROWS, d)
    return {
        "wu": w_in[:, :c0].astype(BF16),
        "wqt": w_in[:, c0:c1].T.astype(BF16),
        "wk": jnp.concatenate([w_in[:, c1:c2], w_in[:, c2:c2 + KV_DIM], w_in[:, c3:c3 + KV_DIM]], axis=1).astype(BF16),
        "wvst": values_t(w_in[:, c2 + KV_DIM:c3]).astype(BF16),
        "wvwt": values_t(w_in[:, c3 + KV_DIM:c4]).astype(BF16),
        "wgt": wg.astype(BF16),
        "wgm": w_in[:, c5:].astype(BF16),
    }


def _s5_weights(bb_re, bb_im, c_re, c_im, d_skip, w_glu, b_glu, w_o):
    per_slab = SSM_LANE_SLAB // SSM_GROUP
    eye = jnp.eye(per_slab, dtype=F32)

    def b_blocks(bb):
        bb = bb.reshape(SSM_SLABS, per_slab, SSM_STATE, SSM_GROUP)
        return jnp.einsum('jgnc,gh->jgchn', bb, eye).reshape(SSM_SLABS, SSM_LANE_SLAB, SSM_STATE_SLAB).astype(BF16)

    def c_blocks(c):
        c = c.astype(F32).reshape(SSM_SLABS, per_slab, SSM_GROUP, SSM_STATE)
        return jnp.einsum('jgcn,gh->jgnhc', c, eye).reshape(SSM_SLABS, SSM_STATE_SLAB, SSM_LANE_SLAB).astype(BF16)

    return {
        "wbre": b_blocks(bb_re), "wbim": b_blocks(bb_im), "cre": c_blocks(c_re), "cim": c_blocks(c_im),
        "d": d_skip.astype(F32).reshape(1, SSM_WIDTH), "wglu": w_glu.astype(BF16),
        "bglu": b_glu.astype(F32).reshape(1, SSM_WIDTH), "wo": w_o.astype(BF16),
    }


def _pick(total, want):
    return want if total % want == 0 else total


def kernel(x, mem, ln_emb_g, ln_emb_b, w_in, ssm_a_re, ssm_a_im, ssm_b_re, ssm_b_im, ssm_c_re, ssm_c_im, ssm_d,
           ssm_log_dt, ssm_w_glu, ssm_b_glu, ssm_w_out, cmp_pos, cmp_w1, cmp_b1, cmp_w2, nsa_w_out, w_out,
           ln1_g, ln1_b, xattn_w_q, xattn_w_kv, xattn_w_o, ln2_g, ln2_b, ffn_w_in, ffn_w_out, ln3_g, ln3_b):
    bsz, L, d = x.shape
    assert w_in.shape[0] == 1, "one layer: the trunk-entry LayerNorm is fused into its input projection"
    l = 0
    att_tile = _pick(L, ATT_TILE)
    row_tile = _pick(L, ROW_TILE)
    wi = _inproj_weights(w_in[l])
    h, u, qt, kvc, ks, kw, vst, vwt, gt = _ln_inproj(x, ln_emb_g, ln_emb_b, wi, row_tile)

    lb_re, lb_im, bb_re, bb_im = _zoh_prep(ssm_a_re[l], ssm_a_im[l], ssm_log_dt[l], ssm_b_re[l], ssm_b_im[l])
    ws = _s5_weights(bb_re, bb_im, ssm_c_re[l], ssm_c_im[l], ssm_d[l], ssm_w_glu[l], ssm_b_glu[l], ssm_w_out[l])
    y_ssm = _s5(u, lb_re, lb_im, ws, _pick(L, 32))

    kc, vct = _compress(kvc, cmp_pos[l].astype(F32), cmp_w1[l].astype(BF16),
                        cmp_b1[l].astype(F32).reshape(2, 1, CMP_HIDDEN),
                        cmp_w2[l, 0].astype(BF16), cmp_w2[l, 1].T.astype(BF16))
    o_t = _nsa_attn(qt, kc, vct, ks, vst, kw, vwt, gt, att_tile)

    h = _mix_ln1(h, y_ssm, o_t, wi["wgm"], nsa_w_out[l].astype(BF16), w_out[l].astype(BF16),
                 ln1_g[l], ln1_b[l], row_tile)
    mk, mv = _mem_kv(mem, xattn_w_kv[l].astype(BF16))
    h = _xattn_ln2(h, mk, mv, xattn_w_q[l].astype(BF16), xattn_w_o[l].astype(BF16), ln2_g[l], ln2_b[l], row_tile)
    h = _ffn_ln3(h.reshape(bsz * L, d), ffn_w_in[l].astype(BF16), ffn_w_out[l].astype(BF16),
                 ln3_g[l], ln3_b[l], row_tile)
    return h.reshape(bsz, L, d)
```

```python
import functools
import math

import jax
import jax.numpy as jnp
from jax import lax
from jax.experimental import pallas as pl
from jax.experimental.pallas import tpu as pltpu

F32 = jnp.float32
BF16 = jnp.bfloat16

D_MODEL = 1024
SSM_WIDTH = 512
SSM_GROUP = 16
SSM_GROUPS = SSM_WIDTH // SSM_GROUP
SSM_STATE = 64
SSM_STATES = SSM_GROUPS * SSM_STATE
SSM_EIG_CLIP = -1e-4
N_HEADS = 16
N_KV_HEADS = 4
HEAD_DIM = 64
GQA_GROUP = N_HEADS // N_KV_HEADS
KV_DIM = N_KV_HEADS * HEAD_DIM
CMP_BLOCK = 32
CMP_STRIDE = 16
CMP_HIDDEN = 256
SLC_BLOCK = 64
SLC_TOP_N = 8
WINDOW = 512
SEL_BIG = 1e9
X_HEADS = 4
X_HEAD_DIM = D_MODEL // X_HEADS
D_FF = 2816
DEEPNORM_ALPHA = 2.0 ** 0.25
LN_EPS = 1e-5
NEG_BIG = -1e30
GATE_ROWS = 16
LOG2E = 1.4426950408889634
ATT_TILE = 256
ROW_TILE = 512
MIX_ROW_TILE = 256
KEY_LANES = 128
ALIBI_LANE = HEAD_DIM
ONEHOT_LANE = HEAD_DIM + 8
VT_ROWS = 80
SLC_LEAD_CHUNKS = 3
CHUNK_ROWS = 16

V7X_VMEM_LIMIT_BYTES = 56 * 1024 * 1024
SSM_LANE_SLAB = 128
SSM_SLABS = SSM_WIDTH // SSM_LANE_SLAB
SSM_STATE_SLAB = SSM_STATES // SSM_SLABS


def _cparams(*sem):
    return pltpu.CompilerParams(dimension_semantics=sem, vmem_limit_bytes=V7X_VMEM_LIMIT_BYTES)


def _full(shape):
    zeros = (0,) * len(shape)
    return pl.BlockSpec(shape, lambda *_: zeros, pipeline_mode=pl.Buffered(1))


def _layer_norm(x, g, b):
    mu = jnp.mean(x, axis=-1, keepdims=True)
    xc = x - mu
    var = jnp.mean(xc * xc, axis=-1, keepdims=True)
    return xc * lax.rsqrt(var + LN_EPS) * g + b


def _gelu_tanh(x):
    return 0.5 * x * (1.0 + jnp.tanh(math.sqrt(2.0 / math.pi) * (x + 0.044715 * (x * x * x))))


def _sigmoid(x):
    return 1.0 / (1.0 + jnp.exp(-x))


def _dot(a, b):
    return jnp.dot(a, b, preferred_element_type=F32)


def _dot_nt(a, b):
    return lax.dot_general(a, b, (((1,), (1,)), ((), ())), preferred_element_type=F32)


def _zoh_kernel(a_re, a_im, log_dt, b_re, b_im, lb_re_o, lb_im_o, bb_re_o, bb_im_o):
    lam_re = jnp.minimum(a_re[...], SSM_EIG_CLIP)
    lam_im = a_im[...]
    dt = jnp.exp(log_dt[...])
    mag = jnp.exp(lam_re * dt)
    lb_re = mag * jnp.cos(lam_im * dt)
    lb_im = mag * jnp.sin(lam_im * dt)
    den = lam_re * lam_re + lam_im * lam_im
    nr = lb_re - 1.0
    f_re = (nr * lam_re + lb_im * lam_im) / den
    f_im = (lb_im * lam_re - nr * lam_im) / den
    br = b_re[...]
    bi = b_im[...]
    lb_re_o[...] = lb_re
    lb_im_o[...] = lb_im
    bb_re_o[...] = f_re * br - f_im * bi
    bb_im_o[...] = f_re * bi + f_im * br


def _zoh_prep(a_re, a_im, log_dt, b_re, b_im):
    gn = SSM_STATES
    col = lambda v: v.astype(F32).reshape(gn, 1)
    dt_col = jnp.broadcast_to(log_dt.astype(F32)[:, None], (SSM_GROUPS, SSM_STATE)).reshape(gn, 1)
    outs = pl.pallas_call(
        _zoh_kernel,
        out_shape=[jax.ShapeDtypeStruct((gn, 1), F32)] * 2 + [jax.ShapeDtypeStruct((gn, SSM_GROUP), F32)] * 2,
        name="zoh_prep",
    )(col(a_re), col(a_im), dt_col, b_re.astype(F32).reshape(gn, SSM_GROUP), b_im.astype(F32).reshape(gn, SSM_GROUP))
    lb_re, lb_im, bb_re, bb_im = outs
    shape_b = (SSM_GROUPS, SSM_STATE, SSM_GROUP)
    return lb_re.reshape(1, gn), lb_im.reshape(1, gn), bb_re.reshape(shape_b), bb_im.reshape(shape_b)


def _inproj_kernel(x_ref, g_ref, b_ref, wu_ref, wqt_ref, wk_ref, wvst_ref, wvwt_ref, wgt_ref,
                   h_o, u_o, qt_o, kvc_o, ks_o, kw_o, vst_o, vwt_o, gt_o):
    tl = x_ref.shape[1]
    h = _layer_norm(x_ref[0], g_ref[...], b_ref[...])
    h_o[0] = h
    hb = h.astype(BF16)
    u_o[0] = _dot(hb, wu_ref[...])
    qt_o[0] = (_dot_nt(wqt_ref[...], hb) * (HEAD_DIM ** -0.5 * LOG2E)).astype(BF16)

    k_all = _dot(hb, wk_ref[...])

    def head_tile(j):
        tile = k_all[:, (j // 2) * KEY_LANES:(j // 2 + 1) * KEY_LANES]
        return pltpu.roll(tile, HEAD_DIM, 1) if j % 2 else tile

    for j in range(2 * N_KV_HEADS):
        kvc_o[0, j] = head_tile(j)[:, :HEAD_DIM]

    lane = lax.broadcasted_iota(jnp.int32, (tl, KEY_LANES), 1)
    pos = pl.program_id(1) * tl + lax.broadcasted_iota(jnp.int32, (tl, KEY_LANES), 0)
    alibi = jnp.where((lane >= ALIBI_LANE) & (lane < ALIBI_LANE + 3), (pos % ATT_TILE).astype(F32), 0.0)
    slc_feat = alibi + jnp.where((lane >= ONEHOT_LANE) & (lane - ONEHOT_LANE == pos // SLC_BLOCK), 1.0, 0.0)
    is_key = lane < HEAD_DIM
    for j in range(N_KV_HEADS):
        ks_o[0, j] = jnp.where(is_key, head_tile(2 * N_KV_HEADS + j), slc_feat).astype(BF16)
        kw_o[0, j] = jnp.where(is_key, head_tile(3 * N_KV_HEADS + j), alibi).astype(BF16)
    row = lax.broadcasted_iota(jnp.int32, (N_KV_HEADS * VT_ROWS, tl), 0)
    ones_row = jnp.where(row % VT_ROWS == HEAD_DIM, 1.0, 0.0)
    vst_o[0] = (_dot_nt(wvst_ref[...], hb) + ones_row).astype(BF16)
    vwt_o[0] = (_dot_nt(wvwt_ref[...], hb) + ones_row).astype(BF16)
    gt_o[0] = _sigmoid(_dot_nt(wgt_ref[...], hb))


def _ln_inproj(x, ln_g, ln_b, w, tl):
    bsz, L, d = x.shape
    n_gate = N_KV_HEADS * GATE_ROWS
    grid = (bsz, L // tl)
    in_specs = [
        pl.BlockSpec((1, tl, d), lambda b, i: (b, i, 0)),
        _full((1, d)), _full((1, d)),
        _full((d, SSM_WIDTH)),
        _full((D_MODEL, d)),
        _full((d, 4 * KV_DIM)),
        _full((N_KV_HEADS * VT_ROWS, d)),
        _full((N_KV_HEADS * VT_ROWS, d)),
        _full((n_gate, d)),
    ]
    assert tl % ATT_TILE == 0 or ATT_TILE % tl == 0
    assert ONEHOT_LANE + L // SLC_BLOCK <= KEY_LANES
    out_shape = [
        jax.ShapeDtypeStruct((bsz, L, d), F32),
        jax.ShapeDtypeStruct((bsz, L, SSM_WIDTH), F32),
        jax.ShapeDtypeStruct((bsz, D_MODEL, L), BF16),
        jax.ShapeDtypeStruct((bsz, 2 * N_KV_HEADS, L, HEAD_DIM), F32),
        jax.ShapeDtypeStruct((bsz, N_KV_HEADS, L, KEY_LANES), BF16),
        jax.ShapeDtypeStruct((bsz, N_KV_HEADS, L, KEY_LANES), BF16),
        jax.ShapeDtypeStruct((bsz, N_KV_HEADS * VT_ROWS, L), BF16),
        jax.ShapeDtypeStruct((bsz, N_KV_HEADS * VT_ROWS, L), BF16),
        jax.ShapeDtypeStruct((bsz, n_gate, L), F32),
    ]
    out_specs = [
        pl.BlockSpec((1, tl, d), lambda b, i: (b, i, 0)),
        pl.BlockSpec((1, tl, SSM_WIDTH), lambda b, i: (b, i, 0)),
        pl.BlockSpec((1, D_MODEL, tl), lambda b, i: (b, 0, i)),
        pl.BlockSpec((1, 2 * N_KV_HEADS, tl, HEAD_DIM), lambda b, i: (b, 0, i, 0)),
        pl.BlockSpec((1, N_KV_HEADS, tl, KEY_LANES), lambda b, i: (b, 0, i, 0)),
        pl.BlockSpec((1, N_KV_HEADS, tl, KEY_LANES), lambda b, i: (b, 0, i, 0)),
        pl.BlockSpec((1, N_KV_HEADS * VT_ROWS, tl), lambda b, i: (b, 0, i)),
        pl.BlockSpec((1, N_KV_HEADS * VT_ROWS, tl), lambda b, i: (b, 0, i)),
        pl.BlockSpec((1, n_gate, tl), lambda b, i: (b, 0, i)),
    ]
    return pl.pallas_call(
        _inproj_kernel, grid=grid, in_specs=in_specs, out_specs=out_specs, out_shape=out_shape,
        compiler_params=_cparams("parallel", "parallel"), name="ln_inproj",
    )(x, ln_g.reshape(1, d), ln_b.reshape(1, d), w["wu"], w["wqt"], w["wk"],
      w["wvst"], w["wvwt"], w["wgt"])


def _s5_kernel(u_ref, lre_ref, lim_ref, wbre_ref, wbim_ref, cre_ref, cim_ref, d_ref, wglu_ref, bglu_ref, wo_ref,
               y_o, sre, sim, hre, him, *, bsz, steps, pitch):
    @pl.when(pl.program_id(0) == 0)
    def _():
        sre[...] = jnp.zeros_like(sre)
        sim[...] = jnp.zeros_like(sim)

    lanes = SSM_LANE_SLAB
    per_slab = SSM_STATE_SLAB // lanes
    u = u_ref[...].reshape(bsz * steps, SSM_WIDTH)
    ub = u.astype(BF16)

    def project_in(j):
        uj = ub[:, j * lanes:(j + 1) * lanes]
        for w_ref, h_ref in ((wbre_ref, hre), (wbim_ref, him)):
            r = _dot(uj, w_ref[j])
            for b in range(bsz):
                for k in range(per_slab):
                    h_ref[j * per_slab + k, pl.ds(b, steps, stride=pitch), :] = (
                        r[b * steps:(b + 1) * steps, k * lanes:(k + 1) * lanes])

    def recur(j):
        for s in range(j * per_slab, (j + 1) * per_slab):
            cols = slice(s * lanes, (s + 1) * lanes)
            lr = jnp.broadcast_to(lre_ref[:, cols], (bsz, lanes))
            li = jnp.broadcast_to(lim_ref[:, cols], (bsz, lanes))
            pr, pi = sre[:, cols], sim[:, cols]
            for t in range(steps):
                rows = pl.ds(t * pitch, bsz)
                pr, pi = (lr * pr - li * pi + hre[s, rows, :], lr * pi + li * pr + him[s, rows, :])
                hre[s, rows, :] = pr
                him[s, rows, :] = pi
            sre[:, cols] = pr
            sim[:, cols] = pi

    def states(h_ref, j):
        return jnp.concatenate(
            [jnp.concatenate([h_ref[j * per_slab + k, pl.ds(b, steps, stride=pitch), :] for k in range(per_slab)],
                             axis=-1) for b in range(bsz)], axis=0)

    ys = [None] * SSM_SLABS

    def project_out(j):
        ys[j] = _dot(states(hre, j).astype(BF16), cre_ref[j]) - _dot(states(him, j).astype(BF16), cim_ref[j])

    for j in range(SSM_SLABS + 2):
        if j < SSM_SLABS:
            project_in(j)
        if 0 <= j - 1 < SSM_SLABS:
            recur(j - 1)
        if 0 <= j - 2 < SSM_SLABS:
            project_out(j - 2)
    y = jnp.concatenate(ys, axis=-1) + d_ref[...] * u
    g = _gelu_tanh(y)
    y2 = g * _sigmoid(_dot(g.astype(BF16), wglu_ref[...]) + bglu_ref[...])
    y_o[...] = _dot(y2.astype(BF16), wo_ref[...]).reshape(bsz, steps, D_MODEL)


def _s5(u, lb_re, lb_im, w, steps):
    bsz, L, _ = u.shape
    pitch = -(-bsz // 4) * 4
    pitch += 4 if (pitch // 4) % 2 == 0 else 0
    grid = (L // steps,)
    kern = functools.partial(_s5_kernel, bsz=bsz, steps=steps, pitch=pitch)
    n_slabs = SSM_STATES // SSM_LANE_SLAB
    in_specs = [
        pl.BlockSpec((bsz, steps, SSM_WIDTH), lambda c: (0, c, 0)),
        _full((1, SSM_STATES)), _full((1, SSM_STATES)),
        _full((SSM_SLABS, SSM_LANE_SLAB, SSM_STATE_SLAB)), _full((SSM_SLABS, SSM_LANE_SLAB, SSM_STATE_SLAB)),
        _full((SSM_SLABS, SSM_STATE_SLAB, SSM_LANE_SLAB)), _full((SSM_SLABS, SSM_STATE_SLAB, SSM_LANE_SLAB)),
        _full((1, SSM_WIDTH)),
        _full((SSM_WIDTH, SSM_WIDTH)), _full((1, SSM_WIDTH)),
        _full((SSM_WIDTH, D_MODEL)),
    ]
    return pl.pallas_call(
        kern, grid=grid, in_specs=in_specs,
        out_specs=pl.BlockSpec((bsz, steps, D_MODEL), lambda c: (0, c, 0)),
        out_shape=jax.ShapeDtypeStruct((bsz, L, D_MODEL), F32),
        scratch_shapes=[pltpu.VMEM((bsz, SSM_STATES), F32), pltpu.VMEM((bsz, SSM_STATES), F32),
                        pltpu.VMEM((n_slabs, steps * pitch, SSM_LANE_SLAB), F32),
                        pltpu.VMEM((n_slabs, steps * pitch, SSM_LANE_SLAB), F32)],
        compiler_params=_cparams("arbitrary"), name="s5",
    )(u, lb_re, lb_im, w["wbre"], w["wbim"], w["cre"], w["cim"], w["d"], w["wglu"], w["bglu"], w["wo"])


def _compress_kernel(kv_ref, pos_ref, w1_ref, b1_ref, w2k_ref, w2vt_ref, kc_o, vct_o, *, n_chunks):
    half = CMP_BLOCK // 2
    for z in range(2):
        for hh in range(N_KV_HEADS):
            j = z * N_KV_HEADS + hh
            first = jnp.zeros((n_chunks, CMP_HIDDEN), F32)
            second = jnp.zeros((n_chunks, CMP_HIDDEN), F32)
            for p in range(half):
                xp = kv_ref[0, j, pl.ds(p, n_chunks, stride=CMP_STRIDE), :]
                first += _dot((xp + pos_ref[z, p:p + 1, :]).astype(BF16), w1_ref[z, p * HEAD_DIM:(p + 1) * HEAD_DIM, :])
                q = half + p
                second += _dot((xp + pos_ref[z, q:q + 1, :]).astype(BF16), w1_ref[z, q * HEAD_DIM:(q + 1) * HEAD_DIM, :])
            pre = first + pltpu.roll(second, n_chunks - 1, 0) + b1_ref[z]
            hid = _gelu_tanh(pre).astype(BF16)
            if z == 0:
                kc_o[0, hh] = _dot(hid, w2k_ref[...]).astype(BF16)
            else:
                vct_o[0, hh * HEAD_DIM:(hh + 1) * HEAD_DIM, :] = _dot_nt(w2vt_ref[...], hid).astype(BF16)


def _compress(kvc, pos, w1, b1, w2k, w2vt):
    bsz, _, L, _ = kvc.shape
    n_chunks = L // CMP_STRIDE
    kern = functools.partial(_compress_kernel, n_chunks=n_chunks)
    return pl.pallas_call(
        kern, grid=(bsz,),
        in_specs=[
            pl.BlockSpec((1, 2 * N_KV_HEADS, L, HEAD_DIM), lambda b: (b, 0, 0, 0)),
            _full((2, CMP_BLOCK, HEAD_DIM)),
            _full((2, CMP_BLOCK * HEAD_DIM, CMP_HIDDEN)),
            _full((2, 1, CMP_HIDDEN)),
            _full((CMP_HIDDEN, HEAD_DIM)),
            _full((HEAD_DIM, CMP_HIDDEN)),
        ],
        out_specs=[
            pl.BlockSpec((1, N_KV_HEADS, n_chunks, HEAD_DIM), lambda b: (b, 0, 0, 0)),
            pl.BlockSpec((1, KV_DIM, n_chunks), lambda b: (b, 0, 0)),
        ],
        out_shape=[
            jax.ShapeDtypeStruct((bsz, N_KV_HEADS, n_chunks, HEAD_DIM), BF16),
            jax.ShapeDtypeStruct((bsz, KV_DIM, n_chunks), BF16),
        ],
        compiler_params=_cparams("parallel"), name="compress",
    )(kvc, pos, w1, b1, w2k, w2vt)


def _nsa_kernel(*refs, tq, n_cmp_pad, n_slc):
    def one_head(hkv, carry):
        _nsa_head(hkv, *refs, tq=tq, n_cmp_pad=n_cmp_pad, n_slc=n_slc)
        return carry

    lax.fori_loop(0, N_KV_HEADS, one_head, 0)


def _nsa_head(hkv, qt_ref, kc_ref, vct_ref, ks_ref, vst_ref, kw_ref, vwt_ref, gt_ref, ov_ref, causal_ref, far_ref,
              chunk_ref, o_ref, qa_ref, qw_ref, m_ref, acc_ref, s_ref, mx_ref, al_ref, sc_ref, pc_ref, need_ref,
              *, tq, n_cmp_pad, n_slc):
    i = pl.program_id(1)
    t0 = i * tq
    tk = tq
    n_sel = min(SLC_TOP_N, n_slc)
    slopes = [jnp.exp2(jnp.full((1, tq), -0.5, F32) * jnp.asarray(hkv * GQA_GROUP + g + 1, F32)) * LOG2E
              for g in range(GQA_GROUP)]
    head_rows = lambda g: pl.ds(pl.multiple_of((hkv * GQA_GROUP + g) * HEAD_DIM, HEAD_DIM), HEAD_DIM)
    vt_rows = pl.ds(pl.multiple_of(hkv * VT_ROWS, 16), VT_ROWS)
    q_heads = [qt_ref[0, head_rows(g), :] for g in range(GQA_GROUP)]
    gate = lambda g, z: gt_ref[0, pl.ds(hkv * GATE_ROWS + 3 * g + z, 1), :]

    t_lane = t0 + lax.broadcasted_iota(jnp.int32, (1, tq), 1)
    row8 = lax.broadcasted_iota(jnp.int32, (ONEHOT_LANE - ALIBI_LANE, tq), 0)

    def augmented(g, tail_rows):
        hi = slopes[g].astype(BF16).astype(F32)
        mid = (slopes[g] - hi).astype(BF16).astype(F32)
        lo = slopes[g] - hi - mid
        parts = jnp.where(row8 == 0, hi, jnp.where(row8 == 1, mid, jnp.where(row8 == 2, lo, 0.0)))
        return jnp.concatenate([q_heads[g].astype(F32), parts, tail_rows], axis=0).astype(BF16)

    def attend(state, q_ref, items, first, slot0):
        starts = [pl.multiple_of((c if valid is None else jnp.maximum(c, 0)) * tk, tk) for _, _, c, _, valid in items]
        for g in range(GQA_GROUP):
            for n, (k_ref, _, _, bias_ref, _) in enumerate(items):
                s = _dot(k_ref[0, hkv, pl.ds(starts[n], tk), :], q_ref[g])
                if bias_ref is not None:
                    s = s + bias_ref[...]
                s_ref[slot0 + n * GQA_GROUP + g] = s
                mx_ref[slot0 + n * GQA_GROUP + g] = jnp.max(s, axis=0, keepdims=True)
        for g in range(GQA_GROUP):
            shifts = []
            for _, _, c, _, valid in items:
                shift = slopes[g] * jnp.asarray((c - i) * tk, F32)
                shifts.append(shift if valid is None else jnp.where(valid, shift, NEG_BIG))
            m_new = functools.reduce(jnp.maximum, [mx_ref[slot0 + n * GQA_GROUP + g] + shifts[n]
                                                   for n in range(len(items))])
            if not first:
                m_new = jnp.maximum(m_ref[state, g], m_new)
                al_ref[g] = jnp.exp2(m_ref[state, g] - m_new)
            pv = None
            for n, (_, vt_ref, _, _, _) in enumerate(items):
                slot = slot0 + n * GQA_GROUP + g
                p = jnp.exp2(s_ref[slot] - (m_new - shifts[n])).astype(BF16)
                part = _dot(vt_ref[0, vt_rows, pl.ds(starts[n], tk)], p)
                pv = part if pv is None else pv + part
            acc_ref[state, g] = pv if first else al_ref[g] * acc_ref[state, g] + pv
            m_ref[state, g] = m_new

    def emit(state, z, accumulate):
        for g in range(GQA_GROUP):
            out = gate(g, z) * (acc_ref[state, g, :HEAD_DIM, :] / acc_ref[state, g, HEAD_DIM:HEAD_DIM + 1, :])
            o_ref[0, head_rows(g), :] = o_ref[0, head_rows(g), :] + out if accumulate else out

    n_idx = lax.broadcasted_iota(jnp.int32, (n_cmp_pad, tq), 0)
    dist_c = (t0 + lax.broadcasted_iota(jnp.int32, (n_cmp_pad, tq), 1)) - (n_idx * CMP_STRIDE + (CMP_BLOCK - 1))
    mask_c = dist_c >= 0
    dist_cf = dist_c.astype(F32)
    kc = kc_ref[0, hkv]
    vct = vct_ref[0, pl.ds(pl.multiple_of(hkv * HEAD_DIM, HEAD_DIM), HEAD_DIM), :]
    for g in range(GQA_GROUP):
        s = _dot(kc, q_heads[g]) - slopes[g] * dist_cf
        sc_ref[g] = jnp.where(mask_c, s, NEG_BIG)
    p_sum = jnp.zeros((n_cmp_pad, tq), F32)
    for g in range(GQA_GROUP):
        s = sc_ref[g]
        p = jnp.where(mask_c, jnp.exp2(s - jnp.max(s, axis=0, keepdims=True)), 0.0)
        p = p / jnp.maximum(jnp.sum(p, axis=0, keepdims=True), 1e-30)
        pc_ref[g] = p.astype(BF16)
        p_sum = p_sum + p
    for g in range(GQA_GROUP):
        o_ref[0, head_rows(g), :] = gate(g, 0) * _dot(vct, pc_ref[g])

    imp = lax.dot_general(ov_ref[...], p_sum, (((1,), (0,)), ((), ())), precision=lax.Precision.HIGHEST,
                          preferred_element_type=F32)
    blk = lax.broadcasted_iota(jnp.int32, (n_slc, tq), 0)
    cur = t_lane // SLC_BLOCK
    forced = (blk == 0) | (blk == cur) | (blk == cur - 1)
    future = blk * SLC_BLOCK > t_lane
    imp = jnp.where(forced, SEL_BIG, jnp.where(future, -SEL_BIG, imp))
    rank = jnp.zeros((n_slc, tq), jnp.int32)
    for j in range(n_slc):
        row = imp[j:j + 1, :]
        beats = (row > imp) | ((row == imp) & (blk > j))
        rank = rank + beats.astype(jnp.int32)
    sel_bias = jnp.where(rank < n_sel, 0.0, NEG_BIG)

    qa_rows = jnp.concatenate([sel_bias, jnp.zeros((KEY_LANES - ONEHOT_LANE - n_slc, tq), F32)], axis=0)
    for g in range(GQA_GROUP):
        qa_ref[g] = augmented(g, qa_rows)

    zero_rows = jnp.zeros((KEY_LANES - ONEHOT_LANE, tq), F32)
    for g in range(GQA_GROUP):
        qw_ref[g] = augmented(g, zero_rows)
    n_back = WINDOW // tk
    n_win_items = n_back + 1
    win_items = [(kw_ref, vwt_ref, i, causal_ref, None)]
    for back in range(1, n_back + 1):
        win_items.append((kw_ref, vwt_ref, i - back, far_ref if back == n_back else None, i >= back))
    attend(0, qw_ref, win_items, True, 0)
    emit(0, 2, True)

    slc_items = [(ks_ref, vst_ref, i, causal_ref, None)]
    for back in range(1, SLC_LEAD_CHUNKS):
        slc_items.append((ks_ref, vst_ref, i - back, None, i >= back))
    slc_items.append((ks_ref, vst_ref, 0, None, i >= SLC_LEAD_CHUNKS))
    attend(1, qa_ref, slc_items, True, n_win_items * GQA_GROUP)

    picked = (rank < n_sel).astype(BF16)
    per_chunk = jnp.sum(_dot(chunk_ref[...], picked), axis=1, keepdims=True)
    for c in range(n_slc * SLC_BLOCK // tk):
        need_ref[c] = (per_chunk[c, 0] > 0.0).astype(jnp.int32)
    last = i - SLC_LEAD_CHUNKS

    def slc_step(pair, carry):
        c0 = 1 + 2 * pair
        c1 = jnp.minimum(c0 + 1, last)
        use0 = need_ref[c0] > 0
        use1 = (c0 + 1 <= last) & (need_ref[c1] > 0)

        @pl.when(use0 | use1)
        def _():
            attend(1, qa_ref, [(ks_ref, vst_ref, c0, None, use0), (ks_ref, vst_ref, c1, None, use1)], False, 0)
        return carry

    lax.fori_loop(0, jnp.maximum(last + 1, 0) // 2, slc_step, 0)
    emit(1, 1, True)


def _nsa_attn(qt, kc, vct, ks, vst, kw, vwt, gt, tq):
    bsz, _, L = qt.shape
    n_cmp_pad = kc.shape[2]
    n_slc = L // SLC_BLOCK
    n = jnp.arange(n_cmp_pad)[None, :]
    j = jnp.arange(n_slc)[:, None]
    overlap_t = ((n * CMP_STRIDE < (j + 1) * SLC_BLOCK) & (n * CMP_STRIDE + CMP_BLOCK - 1 >= j * SLC_BLOCK)).astype(F32)
    assert tq == ATT_TILE and WINDOW % tq == 0 and n_slc % 8 == 0
    n_slots = (WINDOW // tq + 1 + SLC_LEAD_CHUNKS + 1) * GQA_GROUP
    sub = jnp.arange(tq)[:, None]
    lane = jnp.arange(tq)[None, :]
    causal_bias = jnp.where(sub <= lane, 0.0, NEG_BIG).astype(F32)
    far_bias = jnp.where(sub > lane, 0.0, NEG_BIG).astype(F32)
    assert L // tq <= CHUNK_ROWS
    chunk_of_block = (jnp.arange(CHUNK_ROWS)[:, None] == jnp.arange(n_slc)[None, :] * SLC_BLOCK // tq).astype(BF16)
    kern = functools.partial(_nsa_kernel, tq=tq, n_cmp_pad=n_cmp_pad, n_slc=n_slc)
    return pl.pallas_call(
        kern, grid=(bsz, L // tq),
        in_specs=[
            pl.BlockSpec((1, D_MODEL, tq), lambda b, i: (b, 0, i)),
            pl.BlockSpec((1, N_KV_HEADS, n_cmp_pad, HEAD_DIM), lambda b, i: (b, 0, 0, 0)),
            pl.BlockSpec((1, KV_DIM, n_cmp_pad), lambda b, i: (b, 0, 0)),
            pl.BlockSpec((1, N_KV_HEADS, L, KEY_LANES), lambda b, i: (b, 0, 0, 0)),
            pl.BlockSpec((1, N_KV_HEADS * VT_ROWS, L), lambda b, i: (b, 0, 0)),
            pl.BlockSpec((1, N_KV_HEADS, L, KEY_LANES), lambda b, i: (b, 0, 0, 0)),
            pl.BlockSpec((1, N_KV_HEADS * VT_ROWS, L), lambda b, i: (b, 0, 0)),
            pl.BlockSpec((1, N_KV_HEADS * GATE_ROWS, tq), lambda b, i: (b, 0, i)),
            _full((n_slc, n_cmp_pad)),
            _full((tq, tq)), _full((tq, tq)),
            _full((CHUNK_ROWS, n_slc)),
        ],
        out_specs=pl.BlockSpec((1, D_MODEL, tq), lambda b, i: (b, 0, i)),
        out_shape=jax.ShapeDtypeStruct((bsz, D_MODEL, L), F32),
        scratch_shapes=[pltpu.VMEM((GQA_GROUP, KEY_LANES, tq), BF16),
                        pltpu.VMEM((GQA_GROUP, KEY_LANES, tq), BF16),
                        pltpu.VMEM((2, GQA_GROUP, 1, tq), F32),
                        pltpu.VMEM((2, GQA_GROUP, VT_ROWS, tq), F32),
                        pltpu.VMEM((n_slots, tq, tq), F32),
                        pltpu.VMEM((n_slots, 1, tq), F32),
                        pltpu.VMEM((GQA_GROUP, 1, tq), F32),
                        pltpu.VMEM((GQA_GROUP, n_cmp_pad, tq), F32),
                        pltpu.VMEM((GQA_GROUP, n_cmp_pad, tq), BF16),
                        pltpu.SMEM((CHUNK_ROWS,), jnp.int32)],
        compiler_params=_cparams("parallel", "parallel"), name="nsa_attn",
    )(qt, kc, vct, ks, vst, kw, vwt, gt, overlap_t, causal_bias, far_bias, chunk_of_block)


def _mix_kernel(h_ref, ys_ref, ot_ref, wgm_ref, wno_ref, wout_ref, g_ref, b_ref, h1_o):
    h = h_ref[0]
    o = ot_ref[0].T.astype(BF16)
    y_nsa = _dot(o, wno_ref[...])
    gm = _sigmoid(_dot(h.astype(BF16), wgm_ref[...]))
    mix = gm[:, :D_MODEL] * ys_ref[0] + gm[:, D_MODEL:] * y_nsa
    mixed = _dot(mix.astype(BF16), wout_ref[...])
    h1_o[0] = _layer_norm(DEEPNORM_ALPHA * h + mixed, g_ref[...], b_ref[...])


def _mix_ln1(h, y_ssm, o_t, wgm, wno, wout, g, b, tm):
    bsz, L, d = h.shape
    return pl.pallas_call(
        _mix_kernel, grid=(bsz, L // tm),
        in_specs=[
            pl.BlockSpec((1, tm, d), lambda b, i: (b, i, 0)),
            pl.BlockSpec((1, tm, d), lambda b, i: (b, i, 0)),
            pl.BlockSpec((1, d, tm), lambda b, i: (b, 0, i)),
            _full((d, 2 * d)), _full((d, d)), _full((d, d)), _full((1, d)), _full((1, d)),
        ],
        out_specs=pl.BlockSpec((1, tm, d), lambda b, i: (b, i, 0)),
        out_shape=jax.ShapeDtypeStruct((bsz, L, d), F32),
        compiler_params=_cparams("parallel", "parallel"), name="mix_ln1",
    )(h, y_ssm, o_t, wgm, wno, wout, g.reshape(1, d), b.reshape(1, d))


def _memkv_kernel(mem_ref, w_ref, k_o, v_o):
    kv = _dot(mem_ref[0].astype(BF16), w_ref[...])
    k_o[0] = kv[:, :D_MODEL].astype(BF16)
    v_o[0] = kv[:, D_MODEL:].astype(BF16)


def _mem_kv(mem, w_kv):
    bsz, m, d = mem.shape
    spec = pl.BlockSpec((1, m, d), lambda b: (b, 0, 0))
    return pl.pallas_call(
        _memkv_kernel, grid=(bsz,),
        in_specs=[spec, _full((d, 2 * d))],
        out_specs=[spec, spec],
        out_shape=[jax.ShapeDtypeStruct((bsz, m, d), BF16)] * 2,
        compiler_params=_cparams("parallel"), name="mem_kv",
    )(mem, w_kv)


def _xattn_kernel(h_ref, k_ref, v_ref, wq_ref, wo_ref, g_ref, b_ref, h2_o):
    h = h_ref[0]
    q = (_dot(h.astype(BF16), wq_ref[...]) * (X_HEAD_DIM ** -0.5)).astype(BF16)
    outs = []
    for hd in range(X_HEADS):
        cols = slice(hd * X_HEAD_DIM, (hd + 1) * X_HEAD_DIM)
        s = _dot_nt(q[:, cols], k_ref[0, :, cols])
        p = jnp.exp(s - jnp.max(s, axis=-1, keepdims=True))
        p = p / jnp.sum(p, axis=-1, keepdims=True)
        outs.append(_dot(p.astype(BF16), v_ref[0, :, cols]))
    o = jnp.concatenate(outs, axis=-1).astype(BF16)
    h2_o[0] = _layer_norm(DEEPNORM_ALPHA * h + _dot(o, wo_ref[...]), g_ref[...], b_ref[...])


def _xattn_ln2(h, k, v, wq, wo, g, b, tm):
    bsz, L, d = h.shape
    m = k.shape[1]
    row = pl.BlockSpec((1, tm, d), lambda b, i: (b, i, 0))
    kv = pl.BlockSpec((1, m, d), lambda b, i: (b, 0, 0))
    return pl.pallas_call(
        _xattn_kernel, grid=(bsz, L // tm),
        in_specs=[row, kv, kv, _full((d, d)), _full((d, d)), _full((1, d)), _full((1, d))],
        out_specs=row,
        out_shape=jax.ShapeDtypeStruct((bsz, L, d), F32),
        compiler_params=_cparams("parallel", "parallel"), name="xattn_ln2",
    )(h, k, v, wq, wo, g.reshape(1, d), b.reshape(1, d))


def _ffn_kernel(h_ref, win_ref, wout_ref, g_ref, b_ref, o_ref):
    h = h_ref[...]
    gu = _dot(h.astype(BF16), win_ref[...])
    gate = gu[:, :D_FF]
    act = gate * _sigmoid(gate) * gu[:, D_FF:]
    o_ref[...] = _layer_norm(DEEPNORM_ALPHA * h + _dot(act.astype(BF16), wout_ref[...]), g_ref[...], b_ref[...])


def _ffn_ln3(h, win, wout, g, b, tm):
    rows, d = h.shape
    row = pl.BlockSpec((tm, d), lambda i: (i, 0))
    return pl.pallas_call(
        _ffn_kernel, grid=(rows // tm,),
        in_specs=[row, _full((d, 2 * D_FF)), _full((D_FF, d)), _full((1, d)), _full((1, d))],
        out_specs=row,
        out_shape=jax.ShapeDtypeStruct((rows, d), F32),
        compiler_params=_cparams("parallel"), name="ffn_ln3",
    )(h, win, wout, g.reshape(1, d), b.reshape(1, d))


def _inproj_weights(w_in):
    d = D_MODEL
    c0 = SSM_WIDTH
    c1 = c0 + N_HEADS * HEAD_DIM
    c2 = c1 + 2 * KV_DIM
    c3 = c2 + 2 * KV_DIM
    c4 = c3 + 2 * KV_DIM
    c5 = c4 + 3 * N_HEADS
    values_t = lambda m: jnp.pad(m.T.reshape(N_KV_HEADS, HEAD_DIM, d),
                                 ((0, 0), (0, VT_ROWS - HEAD_DIM), (0, 0))).reshape(N_KV_HEADS * VT_ROWS, d)
    wg = w_in[:, c4:c5].T.reshape(N_KV_HEADS, 3 * GQA_GROUP, d)
    wg = jnp.pad(wg, ((0, 0), (0, GATE_ROWS - 3 * GQA_GROUP), (0, 0))).reshape(N_KV_HEADS * GATE_ROWS, d)
    return {
        "wu": w_in[:, :c0].astype(BF16),
        "wqt": w_in[:, c0:c1].T.astype(BF16),
        "wk": jnp.concatenate([w_in[:, c1:c2], w_in[:, c2:c2 + KV_DIM], w_in[:, c3:c3 + KV_DIM]], axis=1).astype(BF16),
        "wvst": values_t(w_in[:, c2 + KV_DIM:c3]).astype(BF16),
        "wvwt": values_t(w_in[:, c3 + KV_DIM:c4]).astype(BF16),
        "wgt": wg.astype(BF16),
        "wgm": w_in[:, c5:].astype(BF16),
    }


def _s5_weights(bb_re, bb_im, c_re, c_im, d_skip, w_glu, b_glu, w_o):
    per_slab = SSM_LANE_SLAB // SSM_GROUP
    eye = jnp.eye(per_slab, dtype=F32)

    def b_blocks(bb):
        bb = bb.reshape(SSM_SLABS, per_slab, SSM_STATE, SSM_GROUP)
        return jnp.einsum('jgnc,gh->jgchn', bb, eye).reshape(SSM_SLABS, SSM_LANE_SLAB, SSM_STATE_SLAB).astype(BF16)

    def c_blocks(c):
        c = c.astype(F32).reshape(SSM_SLABS, per_slab, SSM_GROUP, SSM_STATE)
        return jnp.einsum('jgcn,gh->jgnhc', c, eye).reshape(SSM_SLABS, SSM_STATE_SLAB, SSM_LANE_SLAB).astype(BF16)

    return {
        "wbre": b_blocks(bb_re), "wbim": b_blocks(bb_im), "cre": c_blocks(c_re), "cim": c_blocks(c_im),
        "d": d_skip.astype(F32).reshape(1, SSM_WIDTH), "wglu": w_glu.astype(BF16),
        "bglu": b_glu.astype(F32).reshape(1, SSM_WIDTH), "wo": w_o.astype(BF16),
    }


def _pick(total, want):
    return want if total % want == 0 else total


def kernel(x, mem, ln_emb_g, ln_emb_b, w_in, ssm_a_re, ssm_a_im, ssm_b_re, ssm_b_im, ssm_c_re, ssm_c_im, ssm_d,
           ssm_log_dt, ssm_w_glu, ssm_b_glu, ssm_w_out, cmp_pos, cmp_w1, cmp_b1, cmp_w2, nsa_w_out, w_out,
           ln1_g, ln1_b, xattn_w_q, xattn_w_kv, xattn_w_o, ln2_g, ln2_b, ffn_w_in, ffn_w_out, ln3_g, ln3_b):
    bsz, L, d = x.shape
    assert w_in.shape[0] == 1, "one layer: the trunk-entry LayerNorm is fused into its input projection"
    l = 0
    att_tile = _pick(L, ATT_TILE)
    row_tile = _pick(L, ROW_TILE)
    wi = _inproj_weights(w_in[l])
    h, u, qt, kvc, ks, kw, vst, vwt, gt = _ln_inproj(x, ln_emb_g, ln_emb_b, wi, row_tile)

    lb_re, lb_im, bb_re, bb_im = _zoh_prep(ssm_a_re[l], ssm_a_im[l], ssm_log_dt[l], ssm_b_re[l], ssm_b_im[l])
    ws = _s5_weights(bb_re, bb_im, ssm_c_re[l], ssm_c_im[l], ssm_d[l], ssm_w_glu[l], ssm_b_glu[l], ssm_w_out[l])
    y_ssm = _s5(u, lb_re, lb_im, ws, _pick(L, 32))

    kc, vct = _compress(kvc, cmp_pos[l].astype(F32), cmp_w1[l].astype(BF16),
                        cmp_b1[l].astype(F32).reshape(2, 1, CMP_HIDDEN),
                        cmp_w2[l, 0].astype(BF16), cmp_w2[l, 1].T.astype(BF16))
    o_t = _nsa_attn(qt, kc, vct, ks, vst, kw, vwt, gt, att_tile)

    h = _mix_ln1(h, y_ssm, o_t, wi["wgm"], nsa_w_out[l].astype(BF16), w_out[l].astype(BF16),
                 ln1_g[l], ln1_b[l], _pick(L, MIX_ROW_TILE))
    mk, mv = _mem_kv(mem, xattn_w_kv[l].astype(BF16))
    h = _xattn_ln2(h, mk, mv, xattn_w_q[l].astype(BF16), xattn_w_o[l].astype(BF16), ln2_g[l], ln2_b[l], row_tile)
    h = _ffn_ln3(h.reshape(bsz * L, d), ffn_w_in[l].astype(BF16), ffn_w_out[l].astype(BF16),
                 ln3_g[l], ln3_b[l], row_tile)
    return h.reshape(bsz, L, d)
```

```python
import functools
import math

import jax
import jax.numpy as jnp
from jax import lax
from jax.experimental import pallas as pl
from jax.experimental.pallas import tpu as pltpu

F32 = jnp.float32
BF16 = jnp.bfloat16

D_MODEL = 1024
SSM_WIDTH = 512
SSM_GROUP = 16
SSM_GROUPS = SSM_WIDTH // SSM_GROUP
SSM_STATE = 64
SSM_STATES = SSM_GROUPS * SSM_STATE
SSM_EIG_CLIP = -1e-4
N_HEADS = 16
N_KV_HEADS = 4
HEAD_DIM = 64
GQA_GROUP = N_HEADS // N_KV_HEADS
KV_DIM = N_KV_HEADS * HEAD_DIM
CMP_BLOCK = 32
CMP_STRIDE = 16
CMP_HIDDEN = 256
SLC_BLOCK = 64
SLC_TOP_N = 8
WINDOW = 512
SEL_BIG = 1e9
X_HEADS = 4
X_HEAD_DIM = D_MODEL // X_HEADS
D_FF = 2816
DEEPNORM_ALPHA = 2.0 ** 0.25
LN_EPS = 1e-5
NEG_BIG = -1e30
GATE_ROWS = 16
LOG2E = 1.4426950408889634
ATT_TILE = 256
ROW_TILE = 512
MIX_ROW_TILE = 256
KEY_LANES = 128
ALIBI_LANE = HEAD_DIM
ONEHOT_LANE = HEAD_DIM + 8
VT_ROWS = 80
SLC_LEAD_CHUNKS = 3
HEADS_IN_FLIGHT = 2
HEAD_PHASE_LAG = 2
CHUNK_ROWS = 16

V7X_VMEM_LIMIT_BYTES = 56 * 1024 * 1024
SSM_LANE_SLAB = 128
SSM_SLABS = SSM_WIDTH // SSM_LANE_SLAB
SSM_STATE_SLAB = SSM_STATES // SSM_SLABS


def _cparams(*sem):
    return pltpu.CompilerParams(dimension_semantics=sem, vmem_limit_bytes=V7X_VMEM_LIMIT_BYTES)


def _full(shape):
    zeros = (0,) * len(shape)
    return pl.BlockSpec(shape, lambda *_: zeros, pipeline_mode=pl.Buffered(1))


def _layer_norm(x, g, b):
    mu = jnp.mean(x, axis=-1, keepdims=True)
    xc = x - mu
    var = jnp.mean(xc * xc, axis=-1, keepdims=True)
    return xc * lax.rsqrt(var + LN_EPS) * g + b


def _gelu_tanh(x):
    return 0.5 * x * (1.0 + jnp.tanh(math.sqrt(2.0 / math.pi) * (x + 0.044715 * (x * x * x))))


def _sigmoid(x):
    return 1.0 / (1.0 + jnp.exp(-x))


def _dot(a, b):
    return jnp.dot(a, b, preferred_element_type=F32)


def _dot_nt(a, b):
    return lax.dot_general(a, b, (((1,), (1,)), ((), ())), preferred_element_type=F32)


def _zoh_kernel(a_re, a_im, log_dt, b_re, b_im, lb_re_o, lb_im_o, bb_re_o, bb_im_o):
    lam_re = jnp.minimum(a_re[...], SSM_EIG_CLIP)
    lam_im = a_im[...]
    dt = jnp.exp(log_dt[...])
    mag = jnp.exp(lam_re * dt)
    lb_re = mag * jnp.cos(lam_im * dt)
    lb_im = mag * jnp.sin(lam_im * dt)
    den = lam_re * lam_re + lam_im * lam_im
    nr = lb_re - 1.0
    f_re = (nr * lam_re + lb_im * lam_im) / den
    f_im = (lb_im * lam_re - nr * lam_im) / den
    br = b_re[...]
    bi = b_im[...]
    lb_re_o[...] = lb_re
    lb_im_o[...] = lb_im
    bb_re_o[...] = f_re * br - f_im * bi
    bb_im_o[...] = f_re * bi + f_im * br


def _zoh_prep(a_re, a_im, log_dt, b_re, b_im):
    gn = SSM_STATES
    col = lambda v: v.astype(F32).reshape(gn, 1)
    dt_col = jnp.broadcast_to(log_dt.astype(F32)[:, None], (SSM_GROUPS, SSM_STATE)).reshape(gn, 1)
    outs = pl.pallas_call(
        _zoh_kernel,
        out_shape=[jax.ShapeDtypeStruct((gn, 1), F32)] * 2 + [jax.ShapeDtypeStruct((gn, SSM_GROUP), F32)] * 2,
        name="zoh_prep",
    )(col(a_re), col(a_im), dt_col, b_re.astype(F32).reshape(gn, SSM_GROUP), b_im.astype(F32).reshape(gn, SSM_GROUP))
    lb_re, lb_im, bb_re, bb_im = outs
    shape_b = (SSM_GROUPS, SSM_STATE, SSM_GROUP)
    return lb_re.reshape(1, gn), lb_im.reshape(1, gn), bb_re.reshape(shape_b), bb_im.reshape(shape_b)


def _inproj_kernel(x_ref, g_ref, b_ref, wu_ref, wqt_ref, wk_ref, wvst_ref, wvwt_ref, wgt_ref,
                   h_o, u_o, qt_o, kvc_o, ks_o, kw_o, vst_o, vwt_o, gt_o):
    tl = x_ref.shape[1]
    h = _layer_norm(x_ref[0], g_ref[...], b_ref[...])
    h_o[0] = h
    hb = h.astype(BF16)
    u_o[0] = _dot(hb, wu_ref[...])
    qt_o[0] = (_dot_nt(wqt_ref[...], hb) * (HEAD_DIM ** -0.5 * LOG2E)).astype(BF16)

    k_all = _dot(hb, wk_ref[...])

    def head_tile(j):
        tile = k_all[:, (j // 2) * KEY_LANES:(j // 2 + 1) * KEY_LANES]
        return pltpu.roll(tile, HEAD_DIM, 1) if j % 2 else tile

    for j in range(2 * N_KV_HEADS):
        kvc_o[0, j] = head_tile(j)[:, :HEAD_DIM]

    lane = lax.broadcasted_iota(jnp.int32, (tl, KEY_LANES), 1)
    pos = pl.program_id(1) * tl + lax.broadcasted_iota(jnp.int32, (tl, KEY_LANES), 0)
    alibi = jnp.where((lane >= ALIBI_LANE) & (lane < ALIBI_LANE + 3), (pos % ATT_TILE).astype(F32), 0.0)
    slc_feat = alibi + jnp.where((lane >= ONEHOT_LANE) & (lane - ONEHOT_LANE == pos // SLC_BLOCK), 1.0, 0.0)
    is_key = lane < HEAD_DIM
    for j in range(N_KV_HEADS):
        ks_o[0, j] = jnp.where(is_key, head_tile(2 * N_KV_HEADS + j), slc_feat).astype(BF16)
        kw_o[0, j] = jnp.where(is_key, head_tile(3 * N_KV_HEADS + j), alibi).astype(BF16)
    row = lax.broadcasted_iota(jnp.int32, (N_KV_HEADS * VT_ROWS, tl), 0)
    ones_row = jnp.where(row % VT_ROWS == HEAD_DIM, 1.0, 0.0)
    vst_o[0] = (_dot_nt(wvst_ref[...], hb) + ones_row).astype(BF16)
    vwt_o[0] = (_dot_nt(wvwt_ref[...], hb) + ones_row).astype(BF16)
    gt_o[0] = _sigmoid(_dot_nt(wgt_ref[...], hb))


def _ln_inproj(x, ln_g, ln_b, w, tl):
    bsz, L, d = x.shape
    n_gate = N_KV_HEADS * GATE_ROWS
    grid = (bsz, L // tl)
    in_specs = [
        pl.BlockSpec((1, tl, d), lambda b, i: (b, i, 0)),
        _full((1, d)), _full((1, d)),
        _full((d, SSM_WIDTH)),
        _full((D_MODEL, d)),
        _full((d, 4 * KV_DIM)),
        _full((N_KV_HEADS * VT_ROWS, d)),
        _full((N_KV_HEADS * VT_ROWS, d)),
        _full((n_gate, d)),
    ]
    assert tl % ATT_TILE == 0 or ATT_TILE % tl == 0
    assert ONEHOT_LANE + L // SLC_BLOCK <= KEY_LANES
    out_shape = [
        jax.ShapeDtypeStruct((bsz, L, d), F32),
        jax.ShapeDtypeStruct((bsz, L, SSM_WIDTH), F32),
        jax.ShapeDtypeStruct((bsz, D_MODEL, L), BF16),
        jax.ShapeDtypeStruct((bsz, 2 * N_KV_HEADS, L, HEAD_DIM), F32),
        jax.ShapeDtypeStruct((bsz, N_KV_HEADS, L, KEY_LANES), BF16),
        jax.ShapeDtypeStruct((bsz, N_KV_HEADS, L, KEY_LANES), BF16),
        jax.ShapeDtypeStruct((bsz, N_KV_HEADS * VT_ROWS, L), BF16),
        jax.ShapeDtypeStruct((bsz, N_KV_HEADS * VT_ROWS, L), BF16),
        jax.ShapeDtypeStruct((bsz, n_gate, L), F32),
    ]
    out_specs = [
        pl.BlockSpec((1, tl, d), lambda b, i: (b, i, 0)),
        pl.BlockSpec((1, tl, SSM_WIDTH), lambda b, i: (b, i, 0)),
        pl.BlockSpec((1, D_MODEL, tl), lambda b, i: (b, 0, i)),
        pl.BlockSpec((1, 2 * N_KV_HEADS, tl, HEAD_DIM), lambda b, i: (b, 0, i, 0)),
        pl.BlockSpec((1, N_KV_HEADS, tl, KEY_LANES), lambda b, i: (b, 0, i, 0)),
        pl.BlockSpec((1, N_KV_HEADS, tl, KEY_LANES), lambda b, i: (b, 0, i, 0)),
        pl.BlockSpec((1, N_KV_HEADS * VT_ROWS, tl), lambda b, i: (b, 0, i)),
        pl.BlockSpec((1, N_KV_HEADS * VT_ROWS, tl), lambda b, i: (b, 0, i)),
        pl.BlockSpec((1, n_gate, tl), lambda b, i: (b, 0, i)),
    ]
    return pl.pallas_call(
        _inproj_kernel, grid=grid, in_specs=in_specs, out_specs=out_specs, out_shape=out_shape,
        compiler_params=_cparams("parallel", "parallel"), name="ln_inproj",
    )(x, ln_g.reshape(1, d), ln_b.reshape(1, d), w["wu"], w["wqt"], w["wk"],
      w["wvst"], w["wvwt"], w["wgt"])


def _s5_kernel(u_ref, lre_ref, lim_ref, wbre_ref, wbim_ref, cre_ref, cim_ref, d_ref, wglu_ref, bglu_ref, wo_ref,
               y_o, sre, sim, hre, him, *, bsz, steps, pitch):
    @pl.when(pl.program_id(0) == 0)
    def _():
        sre[...] = jnp.zeros_like(sre)
        sim[...] = jnp.zeros_like(sim)

    lanes = SSM_LANE_SLAB
    per_slab = SSM_STATE_SLAB // lanes
    u = u_ref[...].reshape(bsz * steps, SSM_WIDTH)
    ub = u.astype(BF16)

    def project_in(j):
        uj = ub[:, j * lanes:(j + 1) * lanes]
        for w_ref, h_ref in ((wbre_ref, hre), (wbim_ref, him)):
            r = _dot(uj, w_ref[j])
            for b in range(bsz):
                for k in range(per_slab):
                    h_ref[j * per_slab + k, pl.ds(b, steps, stride=pitch), :] = (
                        r[b * steps:(b + 1) * steps, k * lanes:(k + 1) * lanes])

    def recur(j):
        for s in range(j * per_slab, (j + 1) * per_slab):
            cols = slice(s * lanes, (s + 1) * lanes)
            lr = jnp.broadcast_to(lre_ref[:, cols], (bsz, lanes))
            li = jnp.broadcast_to(lim_ref[:, cols], (bsz, lanes))
            pr, pi = sre[:, cols], sim[:, cols]
            for t in range(steps):
                rows = pl.ds(t * pitch, bsz)
                pr, pi = (lr * pr - li * pi + hre[s, rows, :], lr * pi + li * pr + him[s, rows, :])
                hre[s, rows, :] = pr
                him[s, rows, :] = pi
            sre[:, cols] = pr
            sim[:, cols] = pi

    def states(h_ref, j):
        return jnp.concatenate(
            [jnp.concatenate([h_ref[j * per_slab + k, pl.ds(b, steps, stride=pitch), :] for k in range(per_slab)],
                             axis=-1) for b in range(bsz)], axis=0)

    ys = [None] * SSM_SLABS

    def project_out(j):
        ys[j] = _dot(states(hre, j).astype(BF16), cre_ref[j]) - _dot(states(him, j).astype(BF16), cim_ref[j])

    for j in range(SSM_SLABS + 2):
        if j < SSM_SLABS:
            project_in(j)
        if 0 <= j - 1 < SSM_SLABS:
            recur(j - 1)
        if 0 <= j - 2 < SSM_SLABS:
            project_out(j - 2)
    y = jnp.concatenate(ys, axis=-1) + d_ref[...] * u
    g = _gelu_tanh(y)
    y2 = g * _sigmoid(_dot(g.astype(BF16), wglu_ref[...]) + bglu_ref[...])
    y_o[...] = _dot(y2.astype(BF16), wo_ref[...]).reshape(bsz, steps, D_MODEL)


def _s5(u, lb_re, lb_im, w, steps):
    bsz, L, _ = u.shape
    pitch = -(-bsz // 4) * 4
    pitch += 4 if (pitch // 4) % 2 == 0 else 0
    grid = (L // steps,)
    kern = functools.partial(_s5_kernel, bsz=bsz, steps=steps, pitch=pitch)
    n_slabs = SSM_STATES // SSM_LANE_SLAB
    in_specs = [
        pl.BlockSpec((bsz, steps, SSM_WIDTH), lambda c: (0, c, 0)),
        _full((1, SSM_STATES)), _full((1, SSM_STATES)),
        _full((SSM_SLABS, SSM_LANE_SLAB, SSM_STATE_SLAB)), _full((SSM_SLABS, SSM_LANE_SLAB, SSM_STATE_SLAB)),
        _full((SSM_SLABS, SSM_STATE_SLAB, SSM_LANE_SLAB)), _full((SSM_SLABS, SSM_STATE_SLAB, SSM_LANE_SLAB)),
        _full((1, SSM_WIDTH)),
        _full((SSM_WIDTH, SSM_WIDTH)), _full((1, SSM_WIDTH)),
        _full((SSM_WIDTH, D_MODEL)),
    ]
    return pl.pallas_call(
        kern, grid=grid, in_specs=in_specs,
        out_specs=pl.BlockSpec((bsz, steps, D_MODEL), lambda c: (0, c, 0)),
        out_shape=jax.ShapeDtypeStruct((bsz, L, D_MODEL), F32),
        scratch_shapes=[pltpu.VMEM((bsz, SSM_STATES), F32), pltpu.VMEM((bsz, SSM_STATES), F32),
                        pltpu.VMEM((n_slabs, steps * pitch, SSM_LANE_SLAB), F32),
                        pltpu.VMEM((n_slabs, steps * pitch, SSM_LANE_SLAB), F32)],
        compiler_params=_cparams("arbitrary"), name="s5",
    )(u, lb_re, lb_im, w["wbre"], w["wbim"], w["cre"], w["cim"], w["d"], w["wglu"], w["bglu"], w["wo"])


def _compress_kernel(kv_ref, pos_ref, w1_ref, b1_ref, w2k_ref, w2vt_ref, kc_o, vct_o, *, n_chunks):
    half = CMP_BLOCK // 2
    for z in range(2):
        for hh in range(N_KV_HEADS):
            j = z * N_KV_HEADS + hh
            first = jnp.zeros((n_chunks, CMP_HIDDEN), F32)
            second = jnp.zeros((n_chunks, CMP_HIDDEN), F32)
            for p in range(half):
                xp = kv_ref[0, j, pl.ds(p, n_chunks, stride=CMP_STRIDE), :]
                first += _dot((xp + pos_ref[z, p:p + 1, :]).astype(BF16), w1_ref[z, p * HEAD_DIM:(p + 1) * HEAD_DIM, :])
                q = half + p
                second += _dot((xp + pos_ref[z, q:q + 1, :]).astype(BF16), w1_ref[z, q * HEAD_DIM:(q + 1) * HEAD_DIM, :])
            pre = first + pltpu.roll(second, n_chunks - 1, 0) + b1_ref[z]
            hid = _gelu_tanh(pre).astype(BF16)
            if z == 0:
                kc_o[0, hh] = _dot(hid, w2k_ref[...]).astype(BF16)
            else:
                vct_o[0, hh * HEAD_DIM:(hh + 1) * HEAD_DIM, :] = _dot_nt(w2vt_ref[...], hid).astype(BF16)


def _compress(kvc, pos, w1, b1, w2k, w2vt):
    bsz, _, L, _ = kvc.shape
    n_chunks = L // CMP_STRIDE
    kern = functools.partial(_compress_kernel, n_chunks=n_chunks)
    return pl.pallas_call(
        kern, grid=(bsz,),
        in_specs=[
            pl.BlockSpec((1, 2 * N_KV_HEADS, L, HEAD_DIM), lambda b: (b, 0, 0, 0)),
            _full((2, CMP_BLOCK, HEAD_DIM)),
            _full((2, CMP_BLOCK * HEAD_DIM, CMP_HIDDEN)),
            _full((2, 1, CMP_HIDDEN)),
            _full((CMP_HIDDEN, HEAD_DIM)),
            _full((HEAD_DIM, CMP_HIDDEN)),
        ],
        out_specs=[
            pl.BlockSpec((1, N_KV_HEADS, n_chunks, HEAD_DIM), lambda b: (b, 0, 0, 0)),
            pl.BlockSpec((1, KV_DIM, n_chunks), lambda b: (b, 0, 0)),
        ],
        out_shape=[
            jax.ShapeDtypeStruct((bsz, N_KV_HEADS, n_chunks, HEAD_DIM), BF16),
            jax.ShapeDtypeStruct((bsz, KV_DIM, n_chunks), BF16),
        ],
        compiler_params=_cparams("parallel"), name="compress",
    )(kvc, pos, w1, b1, w2k, w2vt)


def _nsa_kernel(*refs, tq, n_cmp_pad, n_slc):
    def head_pair(pair, carry):
        heads = [_nsa_head(HEADS_IN_FLIGHT * pair + slot, slot, *refs, tq=tq, n_cmp_pad=n_cmp_pad, n_slc=n_slc)
                 for slot in range(HEADS_IN_FLIGHT)]
        tails, live, step = {}, set(range(HEADS_IN_FLIGHT)), 0
        while live:
            for slot in sorted(live):
                if step >= slot * HEAD_PHASE_LAG:
                    try:
                        out = next(heads[slot])
                        if out is not None:
                            tails[slot] = out
                    except StopIteration:
                        live.discard(slot)
            step += 1
        for slot in range(HEADS_IN_FLIGHT):
            tails[slot]()
        return carry

    lax.fori_loop(0, N_KV_HEADS // HEADS_IN_FLIGHT, head_pair, 0)


def _nsa_head(hkv, par, qt_ref, kc_ref, vct_ref, ks_ref, vst_ref, kw_ref, vwt_ref, gt_ref, ov_ref, causal_ref,
              far_ref, chunk_ref, o_ref, qa_ref, qw_ref, m_ref, acc_ref, s_ref, mx_ref, al_ref, sc_ref, pc_ref,
              need_ref, *, tq, n_cmp_pad, n_slc):
    i = pl.program_id(1)
    slots_per_head = s_ref.shape[0] // HEADS_IN_FLIGHT
    t0 = i * tq
    tk = tq
    n_sel = min(SLC_TOP_N, n_slc)
    slopes = [jnp.exp2(jnp.full((1, tq), -0.5, F32) * jnp.asarray(hkv * GQA_GROUP + g + 1, F32)) * LOG2E
              for g in range(GQA_GROUP)]
    head_rows = lambda g: pl.ds(pl.multiple_of((hkv * GQA_GROUP + g) * HEAD_DIM, HEAD_DIM), HEAD_DIM)
    vt_rows = pl.ds(pl.multiple_of(hkv * VT_ROWS, 16), VT_ROWS)
    q_heads = [qt_ref[0, head_rows(g), :] for g in range(GQA_GROUP)]
    gate = lambda g, z: gt_ref[0, pl.ds(hkv * GATE_ROWS + 3 * g + z, 1), :]

    t_lane = t0 + lax.broadcasted_iota(jnp.int32, (1, tq), 1)
    row8 = lax.broadcasted_iota(jnp.int32, (ONEHOT_LANE - ALIBI_LANE, tq), 0)

    def augmented(g, tail_rows):
        hi = slopes[g].astype(BF16).astype(F32)
        mid = (slopes[g] - hi).astype(BF16).astype(F32)
        lo = slopes[g] - hi - mid
        parts = jnp.where(row8 == 0, hi, jnp.where(row8 == 1, mid, jnp.where(row8 == 2, lo, 0.0)))
        return jnp.concatenate([q_heads[g].astype(F32), parts, tail_rows], axis=0).astype(BF16)

    def attend(state, q_ref, items, first, slot0):
        starts = [pl.multiple_of((c if valid is None else jnp.maximum(c, 0)) * tk, tk) for _, _, c, _, valid in items]
        slot0 = slot0 + par * slots_per_head

        def scores(g):
            for n, (k_ref, _, _, bias_ref, _) in enumerate(items):
                s = _dot(k_ref[0, hkv, pl.ds(starts[n], tk), :], q_ref[par, g])
                if bias_ref is not None:
                    s = s + bias_ref[...]
                s_ref[slot0 + n * GQA_GROUP + g] = s
                mx_ref[slot0 + n * GQA_GROUP + g] = jnp.max(s, axis=0, keepdims=True)

        def fold(g):
            shifts = []
            for _, _, c, _, valid in items:
                shift = slopes[g] * jnp.asarray((c - i) * tk, F32)
                shifts.append(shift if valid is None else jnp.where(valid, shift, NEG_BIG))
            m_new = functools.reduce(jnp.maximum, [mx_ref[slot0 + n * GQA_GROUP + g] + shifts[n]
                                                   for n in range(len(items))])
            if not first:
                m_new = jnp.maximum(m_ref[par, state, g], m_new)
                al_ref[par, g] = jnp.exp2(m_ref[par, state, g] - m_new)
            pv = None
            for n, (_, vt_ref, _, _, _) in enumerate(items):
                slot = slot0 + n * GQA_GROUP + g
                p = jnp.exp2(s_ref[slot] - (m_new - shifts[n])).astype(BF16)
                part = _dot(vt_ref[0, vt_rows, pl.ds(starts[n], tk)], p)
                pv = part if pv is None else pv + part
            acc_ref[par, state, g] = pv if first else al_ref[par, g] * acc_ref[par, state, g] + pv
            m_ref[par, state, g] = m_new

        scores(0)
        for g in range(GQA_GROUP):
            if g + 1 < GQA_GROUP:
                scores(g + 1)
            fold(g)
            yield

    def emit(state, z, accumulate):
        for g in range(GQA_GROUP):
            scale = gate(g, z) / acc_ref[par, state, g, HEAD_DIM:HEAD_DIM + 1, :]
            out = acc_ref[par, state, g, :HEAD_DIM, :] * scale
            o_ref[0, head_rows(g), :] = o_ref[0, head_rows(g), :] + out if accumulate else out

    n_idx = lax.broadcasted_iota(jnp.int32, (n_cmp_pad, tq), 0)
    dist_c = (t0 + lax.broadcasted_iota(jnp.int32, (n_cmp_pad, tq), 1)) - (n_idx * CMP_STRIDE + (CMP_BLOCK - 1))
    mask_c = dist_c >= 0
    dist_cf = dist_c.astype(F32)
    kc = kc_ref[0, hkv]
    vct = vct_ref[0, pl.ds(pl.multiple_of(hkv * HEAD_DIM, HEAD_DIM), HEAD_DIM), :]
    for g in range(GQA_GROUP):
        s = _dot(kc, q_heads[g]) - slopes[g] * dist_cf
        sc_ref[par, g] = jnp.where(mask_c, s, NEG_BIG)
    yield
    p_sum = jnp.zeros((n_cmp_pad, tq), F32)
    for g in range(GQA_GROUP):
        s = sc_ref[par, g]
        m = jnp.max(s, axis=0, keepdims=True)
        e = jnp.exp2(s - m)
        norm = 1.0 / jnp.maximum(jnp.sum(e, axis=0, keepdims=True), 1e-30)
        p = e * jnp.where(m > 0.5 * NEG_BIG, norm, 0.0)
        pc_ref[par, g] = p.astype(BF16)
        p_sum = p_sum + p
    for g in range(GQA_GROUP):
        o_ref[0, head_rows(g), :] = gate(g, 0) * _dot(vct, pc_ref[par, g])

    imp = lax.dot_general(ov_ref[...], p_sum, (((1,), (0,)), ((), ())), precision=lax.Precision.HIGHEST,
                          preferred_element_type=F32)
    yield
    blk = lax.broadcasted_iota(jnp.int32, (n_slc, tq), 0)
    cur = t_lane // SLC_BLOCK
    forced = (blk == 0) | (blk == cur) | (blk == cur - 1)
    future = blk * SLC_BLOCK > t_lane
    imp = jnp.where(forced, SEL_BIG, jnp.where(future, -SEL_BIG, imp))
    sub8 = 8
    tiles = [imp[r:r + sub8] for r in range(0, n_slc, sub8)]
    ranks = [jnp.zeros((sub8, tq), jnp.int32) for _ in tiles]
    for j in range(n_slc):
        row = imp[j:j + 1, :]
        for r, tile in enumerate(tiles):
            if r * sub8 > j:
                ahead = row >= tile
            elif r * sub8 + sub8 - 1 < j:
                ahead = row > tile
            else:
                later = r * sub8 + lax.broadcasted_iota(jnp.int32, (sub8, tq), 0) > j
                ahead = (row > tile) | ((row == tile) & later)
            ranks[r] = ranks[r] + ahead.astype(jnp.int32)
    rank = jnp.concatenate(ranks, axis=0)
    sel_bias = jnp.where(rank < n_sel, 0.0, NEG_BIG)

    qa_rows = jnp.concatenate([sel_bias, jnp.zeros((KEY_LANES - ONEHOT_LANE - n_slc, tq), F32)], axis=0)
    for g in range(GQA_GROUP):
        qa_ref[par, g] = augmented(g, qa_rows)
    yield

    zero_rows = jnp.zeros((KEY_LANES - ONEHOT_LANE, tq), F32)
    for g in range(GQA_GROUP):
        qw_ref[par, g] = augmented(g, zero_rows)
    n_back = WINDOW // tk
    n_win_items = n_back + 1
    win_items = [(kw_ref, vwt_ref, i, causal_ref, None)]
    for back in range(1, n_back + 1):
        win_items.append((kw_ref, vwt_ref, i - back, far_ref if back == n_back else None, i >= back))
    yield from attend(0, qw_ref, win_items, True, 0)
    emit(0, 2, True)
    yield

    slc_items = [(ks_ref, vst_ref, i, causal_ref, None)]
    for back in range(1, SLC_LEAD_CHUNKS):
        slc_items.append((ks_ref, vst_ref, i - back, None, i >= back))
    slc_items.append((ks_ref, vst_ref, 0, None, i >= SLC_LEAD_CHUNKS))
    yield from attend(1, qa_ref, slc_items, True, n_win_items * GQA_GROUP)

    picked = (rank < n_sel).astype(BF16)
    per_chunk = jnp.sum(_dot(chunk_ref[...], picked), axis=1, keepdims=True)
    need0 = par * CHUNK_ROWS
    for c in range(n_slc * SLC_BLOCK // tk):
        need_ref[need0 + c] = (per_chunk[c, 0] > 0.0).astype(jnp.int32)
    last = i - SLC_LEAD_CHUNKS

    def slc_step(pair, carry):
        c0 = 1 + 2 * pair
        c1 = jnp.minimum(c0 + 1, last)
        use0 = need_ref[need0 + c0] > 0
        use1 = (c0 + 1 <= last) & (need_ref[need0 + c1] > 0)

        @pl.when(use0 | use1)
        def _():
            for _ in attend(1, qa_ref, [(ks_ref, vst_ref, c0, None, use0), (ks_ref, vst_ref, c1, None, use1)], False, 0):
                pass
        return carry

    def tail():
        lax.fori_loop(0, jnp.maximum(last + 1, 0) // 2, slc_step, 0)
        emit(1, 1, True)

    yield tail


def _nsa_attn(qt, kc, vct, ks, vst, kw, vwt, gt, tq):
    bsz, _, L = qt.shape
    n_cmp_pad = kc.shape[2]
    n_slc = L // SLC_BLOCK
    n = jnp.arange(n_cmp_pad)[None, :]
    j = jnp.arange(n_slc)[:, None]
    overlap_t = ((n * CMP_STRIDE < (j + 1) * SLC_BLOCK) & (n * CMP_STRIDE + CMP_BLOCK - 1 >= j * SLC_BLOCK)).astype(F32)
    assert tq == ATT_TILE and WINDOW % tq == 0 and n_slc % 8 == 0
    n_slots = (WINDOW // tq + 1 + SLC_LEAD_CHUNKS + 1) * GQA_GROUP
    sub = jnp.arange(tq)[:, None]
    lane = jnp.arange(tq)[None, :]
    causal_bias = jnp.where(sub <= lane, 0.0, NEG_BIG).astype(F32)
    far_bias = jnp.where(sub > lane, 0.0, NEG_BIG).astype(F32)
    assert L // tq <= CHUNK_ROWS
    chunk_of_block = (jnp.arange(CHUNK_ROWS)[:, None] == jnp.arange(n_slc)[None, :] * SLC_BLOCK // tq).astype(BF16)
    kern = functools.partial(_nsa_kernel, tq=tq, n_cmp_pad=n_cmp_pad, n_slc=n_slc)
    return pl.pallas_call(
        kern, grid=(bsz, L // tq),
        in_specs=[
            pl.BlockSpec((1, D_MODEL, tq), lambda b, i: (b, 0, i)),
            pl.BlockSpec((1, N_KV_HEADS, n_cmp_pad, HEAD_DIM), lambda b, i: (b, 0, 0, 0)),
            pl.BlockSpec((1, KV_DIM, n_cmp_pad), lambda b, i: (b, 0, 0)),
            pl.BlockSpec((1, N_KV_HEADS, L, KEY_LANES), lambda b, i: (b, 0, 0, 0)),
            pl.BlockSpec((1, N_KV_HEADS * VT_ROWS, L), lambda b, i: (b, 0, 0)),
            pl.BlockSpec((1, N_KV_HEADS, L, KEY_LANES), lambda b, i: (b, 0, 0, 0)),
            pl.BlockSpec((1, N_KV_HEADS * VT_ROWS, L), lambda b, i: (b, 0, 0)),
            pl.BlockSpec((1, N_KV_HEADS * GATE_ROWS, tq), lambda b, i: (b, 0, i)),
            _full((n_slc, n_cmp_pad)),
            _full((tq, tq)), _full((tq, tq)),
            _full((CHUNK_ROWS, n_slc)),
        ],
        out_specs=pl.BlockSpec((1, D_MODEL, tq), lambda b, i: (b, 0, i)),
        out_shape=jax.ShapeDtypeStruct((bsz, D_MODEL, L), F32),
        scratch_shapes=[pltpu.VMEM((HEADS_IN_FLIGHT, GQA_GROUP, KEY_LANES, tq), BF16),
                        pltpu.VMEM((HEADS_IN_FLIGHT, GQA_GROUP, KEY_LANES, tq), BF16),
                        pltpu.VMEM((HEADS_IN_FLIGHT, 2, GQA_GROUP, 1, tq), F32),
                        pltpu.VMEM((HEADS_IN_FLIGHT, 2, GQA_GROUP, VT_ROWS, tq), F32),
                        pltpu.VMEM((HEADS_IN_FLIGHT * n_slots, tq, tq), F32),
                        pltpu.VMEM((HEADS_IN_FLIGHT * n_slots, 1, tq), F32),
                        pltpu.VMEM((HEADS_IN_FLIGHT, GQA_GROUP, 1, tq), F32),
                        pltpu.VMEM((HEADS_IN_FLIGHT, GQA_GROUP, n_cmp_pad, tq), F32),
                        pltpu.VMEM((HEADS_IN_FLIGHT, GQA_GROUP, n_cmp_pad, tq), BF16),
                        pltpu.SMEM((HEADS_IN_FLIGHT * CHUNK_ROWS,), jnp.int32)],
        compiler_params=_cparams("parallel", "parallel"), name="nsa_attn",
    )(qt, kc, vct, ks, vst, kw, vwt, gt, overlap_t, causal_bias, far_bias, chunk_of_block)


def _mix_kernel(h_ref, ys_ref, ot_ref, wgm_ref, wno_ref, wout_ref, g_ref, b_ref, h1_o):
    h = h_ref[0]
    o = ot_ref[0].T.astype(BF16)
    y_nsa = _dot(o, wno_ref[...])
    gm = _sigmoid(_dot(h.astype(BF16), wgm_ref[...]))
    mix = gm[:, :D_MODEL] * ys_ref[0] + gm[:, D_MODEL:] * y_nsa
    mixed = _dot(mix.astype(BF16), wout_ref[...])
    h1_o[0] = _layer_norm(DEEPNORM_ALPHA * h + mixed, g_ref[...], b_ref[...])


def _mix_ln1(h, y_ssm, o_t, wgm, wno, wout, g, b, tm):
    bsz, L, d = h.shape
    return pl.pallas_call(
        _mix_kernel, grid=(bsz, L // tm),
        in_specs=[
            pl.BlockSpec((1, tm, d), lambda b, i: (b, i, 0)),
            pl.BlockSpec((1, tm, d), lambda b, i: (b, i, 0)),
            pl.BlockSpec((1, d, tm), lambda b, i: (b, 0, i)),
            _full((d, 2 * d)), _full((d, d)), _full((d, d)), _full((1, d)), _full((1, d)),
        ],
        out_specs=pl.BlockSpec((1, tm, d), lambda b, i: (b, i, 0)),
        out_shape=jax.ShapeDtypeStruct((bsz, L, d), F32),
        compiler_params=_cparams("parallel", "parallel"), name="mix_ln1",
    )(h, y_ssm, o_t, wgm, wno, wout, g.reshape(1, d), b.reshape(1, d))


def _memkv_kernel(mem_ref, w_ref, k_o, v_o):
    kv = _dot(mem_ref[0].astype(BF16), w_ref[...])
    k_o[0] = kv[:, :D_MODEL].astype(BF16)
    v_o[0] = kv[:, D_MODEL:].astype(BF16)


def _mem_kv(mem, w_kv):
    bsz, m, d = mem.shape
    spec = pl.BlockSpec((1, m, d), lambda b: (b, 0, 0))
    return pl.pallas_call(
        _memkv_kernel, grid=(bsz,),
        in_specs=[spec, _full((d, 2 * d))],
        out_specs=[spec, spec],
        out_shape=[jax.ShapeDtypeStruct((bsz, m, d), BF16)] * 2,
        compiler_params=_cparams("parallel"), name="mem_kv",
    )(mem, w_kv)


def _xattn_kernel(h_ref, k_ref, v_ref, wq_ref, wo_ref, g_ref, b_ref, h2_o):
    h = h_ref[0]
    q = (_dot(h.astype(BF16), wq_ref[...]) * (X_HEAD_DIM ** -0.5)).astype(BF16)
    outs = []
    for hd in range(X_HEADS):
        cols = slice(hd * X_HEAD_DIM, (hd + 1) * X_HEAD_DIM)
        s = _dot_nt(q[:, cols], k_ref[0, :, cols])
        p = jnp.exp(s - jnp.max(s, axis=-1, keepdims=True))
        p = p / jnp.sum(p, axis=-1, keepdims=True)
        outs.append(_dot(p.astype(BF16), v_ref[0, :, cols]))
    o = jnp.concatenate(outs, axis=-1).astype(BF16)
    h2_o[0] = _layer_norm(DEEPNORM_ALPHA * h + _dot(o, wo_ref[...]), g_ref[...], b_ref[...])


def _xattn_ln2(h, k, v, wq, wo, g, b, tm):
    bsz, L, d = h.shape
    m = k.shape[1]
    row = pl.BlockSpec((1, tm, d), lambda b, i: (b, i, 0))
    kv = pl.BlockSpec((1, m, d), lambda b, i: (b, 0, 0))
    return pl.pallas_call(
        _xattn_kernel, grid=(bsz, L // tm),
        in_specs=[row, kv, kv, _full((d, d)), _full((d, d)), _full((1, d)), _full((1, d))],
        out_specs=row,
        out_shape=jax.ShapeDtypeStruct((bsz, L, d), F32),
        compiler_params=_cparams("parallel", "parallel"), name="xattn_ln2",
    )(h, k, v, wq, wo, g.reshape(1, d), b.reshape(1, d))


def _ffn_kernel(h_ref, win_ref, wout_ref, g_ref, b_ref, o_ref):
    h = h_ref[...]
    gu = _dot(h.astype(BF16), win_ref[...])
    gate = gu[:, :D_FF]
    act = gate * _sigmoid(gate) * gu[:, D_FF:]
    o_ref[...] = _layer_norm(DEEPNORM_ALPHA * h + _dot(act.astype(BF16), wout_ref[...]), g_ref[...], b_ref[...])


def _ffn_ln3(h, win, wout, g, b, tm):
    rows, d = h.shape
    row = pl.BlockSpec((tm, d), lambda i: (i, 0))
    return pl.pallas_call(
        _ffn_kernel, grid=(rows // tm,),
        in_specs=[row, _full((d, 2 * D_FF)), _full((D_FF, d)), _full((1, d)), _full((1, d))],
        out_specs=row,
        out_shape=jax.ShapeDtypeStruct((rows, d), F32),
        compiler_params=_cparams("parallel"), name="ffn_ln3",
    )(h, win, wout, g.reshape(1, d), b.reshape(1, d))


def _inproj_weights(w_in):
    d = D_MODEL
    c0 = SSM_WIDTH
    c1 = c0 + N_HEADS * HEAD_DIM
    c2 = c1 + 2 * KV_DIM
    c3 = c2 + 2 * KV_DIM
    c4 = c3 + 2 * KV_DIM
    c5 = c4 + 3 * N_HEADS
    values_t = lambda m: jnp.pad(m.T.reshape(N_KV_HEADS, HEAD_DIM, d),
                                 ((0, 0), (0, VT_ROWS - HEAD_DIM), (0, 0))).reshape(N_KV_HEADS * VT_ROWS, d)
    wg = w_in[:, c4:c5].T.reshape(N_KV_HEADS, 3 * GQA_GROUP, d)
    wg = jnp.pad(wg, ((0, 0), (0, GATE_ROWS - 3 * GQA_GROUP), (0, 0))).reshape(N_KV_HEADS * GATE_ROWS, d)
    return {
        "wu": w_in[:, :c0].astype(BF16),
        "wqt": w_in[:, c0:c1].T.astype(BF16),
        "wk": jnp.concatenate([w_in[:, c1:c2], w_in[:, c2:c2 + KV_DIM], w_in[:, c3:c3 + KV_DIM]], axis=1).astype(BF16),
        "wvst": values_t(w_in[:, c2 + KV_DIM:c3]).astype(BF16),
        "wvwt": values_t(w_in[:, c3 + KV_DIM:c4]).astype(BF16),
        "wgt": wg.astype(BF16),
        "wgm": w_in[:, c5:].astype(BF16),
    }


def _s5_weights(bb_re, bb_im, c_re, c_im, d_skip, w_glu, b_glu, w_o):
    per_slab = SSM_LANE_SLAB // SSM_GROUP
    eye = jnp.eye(per_slab, dtype=F32)

    def b_blocks(bb):
        bb = bb.reshape(SSM_SLABS, per_slab, SSM_STATE, SSM_GROUP)
        return jnp.einsum('jgnc,gh->jgchn', bb, eye).reshape(SSM_SLABS, SSM_LANE_SLAB, SSM_STATE_SLAB).astype(BF16)

    def c_blocks(c):
        c = c.astype(F32).reshape(SSM_SLABS, per_slab, SSM_GROUP, SSM_STATE)
        return jnp.einsum('jgcn,gh->jgnhc', c, eye).reshape(SSM_SLABS, SSM_STATE_SLAB, SSM_LANE_SLAB).astype(BF16)

    return {
        "wbre": b_blocks(bb_re), "wbim": b_blocks(bb_im), "cre": c_blocks(c_re), "cim": c_blocks(c_im),
        "d": d_skip.astype(F32).reshape(1, SSM_WIDTH), "wglu": w_glu.astype(BF16),
        "bglu": b_glu.astype(F32).reshape(1, SSM_WIDTH), "wo": w_o.astype(BF16),
    }


def _pick(total, want):
    return want if total % want == 0 else total


def kernel(x, mem, ln_emb_g, ln_emb_b, w_in, ssm_a_re, ssm_a_im, ssm_b_re, ssm_b_im, ssm_c_re, ssm_c_im, ssm_d,
           ssm_log_dt, ssm_w_glu, ssm_b_glu, ssm_w_out, cmp_pos, cmp_w1, cmp_b1, cmp_w2, nsa_w_out, w_out,
           ln1_g, ln1_b, xattn_w_q, xattn_w_kv, xattn_w_o, ln2_g, ln2_b, ffn_w_in, ffn_w_out, ln3_g, ln3_b):
    bsz, L, d = x.shape
    assert w_in.shape[0] == 1, "one layer: the trunk-entry LayerNorm is fused into its input projection"
    l = 0
    att_tile = _pick(L, ATT_TILE)
    row_tile = _pick(L, ROW_TILE)
    wi = _inproj_weights(w_in[l])
    h, u, qt, kvc, ks, kw, vst, vwt, gt = _ln_inproj(x, ln_emb_g, ln_emb_b, wi, row_tile)

    lb_re, lb_im, bb_re, bb_im = _zoh_prep(ssm_a_re[l], ssm_a_im[l], ssm_log_dt[l], ssm_b_re[l], ssm_b_im[l])
    ws = _s5_weights(bb_re, bb_im, ssm_c_re[l], ssm_c_im[l], ssm_d[l], ssm_w_glu[l], ssm_b_glu[l], ssm_w_out[l])
    y_ssm = _s5(u, lb_re, lb_im, ws, _pick(L, 32))

    kc, vct = _compress(kvc, cmp_pos[l].astype(F32), cmp_w1[l].astype(BF16),
                        cmp_b1[l].astype(F32).reshape(2, 1, CMP_HIDDEN),
                        cmp_w2[l, 0].astype(BF16), cmp_w2[l, 1].T.astype(BF16))
    o_t = _nsa_attn(qt, kc, vct, ks, vst, kw, vwt, gt, att_tile)

    h = _mix_ln1(h, y_ssm, o_t, wi["wgm"], nsa_w_out[l].astype(BF16), w_out[l].astype(BF16),
                 ln1_g[l], ln1_b[l], _pick(L, MIX_ROW_TILE))
    mk, mv = _mem_kv(mem, xattn_w_kv[l].astype(BF16))
    h = _xattn_ln2(h, mk, mv, xattn_w_q[l].astype(BF16), xattn_w_o[l].astype(BF16), ln2_g[l], ln2_b[l], row_tile)
    h = _ffn_ln3(h.reshape(bsz * L, d), ffn_w_in[l].astype(BF16), ffn_w_out[l].astype(BF16),
                 ln3_g[l], ln3_b[l], row_tile)
    return h.reshape(bsz, L, d)
```

```python
import functools
import math

import jax
import jax.numpy as jnp
from jax import lax
from jax.experimental import pallas as pl
from jax.experimental.pallas import tpu as pltpu

F32 = jnp.float32
BF16 = jnp.bfloat16

D_MODEL = 1024
SSM_WIDTH = 512
SSM_GROUP = 16
SSM_GROUPS = SSM_WIDTH // SSM_GROUP
SSM_STATE = 64
SSM_STATES = SSM_GROUPS * SSM_STATE
SSM_EIG_CLIP = -1e-4
N_HEADS = 16
N_KV_HEADS = 4
HEAD_DIM = 64
GQA_GROUP = N_HEADS // N_KV_HEADS
KV_DIM = N_KV_HEADS * HEAD_DIM
CMP_BLOCK = 32
CMP_STRIDE = 16
CMP_HIDDEN = 256
CMP_QUAD = 4
SLC_BLOCK = 64
SLC_TOP_N = 8
WINDOW = 512
SEL_BIG = 1e9
X_HEADS = 4
X_HEAD_DIM = D_MODEL // X_HEADS
D_FF = 2816
DEEPNORM_ALPHA = 2.0 ** 0.25
LN_EPS = 1e-5
NEG_BIG = -1e30
GATE_ROWS = 16
LOG2E = 1.4426950408889634
ATT_TILE = 256
ROW_TILE = 512
MIX_ROW_TILE = 256
KEY_LANES = 128
ALIBI_LANE = HEAD_DIM
ONEHOT_LANE = HEAD_DIM + 8
VT_ROWS = 80
SLC_LEAD_CHUNKS = 3
HEADS_IN_FLIGHT = 2
HEAD_PHASE_LAG = 2
CHUNK_ROWS = 16

V7X_VMEM_LIMIT_BYTES = 56 * 1024 * 1024
SSM_LANE_SLAB = 128
SSM_SLABS = SSM_WIDTH // SSM_LANE_SLAB
SSM_STATE_SLAB = SSM_STATES // SSM_SLABS


def _cparams(*sem):
    return pltpu.CompilerParams(dimension_semantics=sem, vmem_limit_bytes=V7X_VMEM_LIMIT_BYTES)


def _full(shape):
    zeros = (0,) * len(shape)
    return pl.BlockSpec(shape, lambda *_: zeros, pipeline_mode=pl.Buffered(1))


def _layer_norm(x, g, b):
    mu = jnp.mean(x, axis=-1, keepdims=True)
    xc = x - mu
    var = jnp.mean(xc * xc, axis=-1, keepdims=True)
    return xc * lax.rsqrt(var + LN_EPS) * g + b


def _gelu_tanh(x):
    return 0.5 * x * (1.0 + jnp.tanh(math.sqrt(2.0 / math.pi) * (x + 0.044715 * (x * x * x))))


def _sigmoid(x):
    return 1.0 / (1.0 + jnp.exp(-x))


def _dot(a, b):
    return jnp.dot(a, b, preferred_element_type=F32)


def _dot_nt(a, b):
    return lax.dot_general(a, b, (((1,), (1,)), ((), ())), preferred_element_type=F32)


def _zoh_kernel(a_re, a_im, log_dt, b_re, b_im, lb_re_o, lb_im_o, bb_re_o, bb_im_o):
    lam_re = jnp.minimum(a_re[...], SSM_EIG_CLIP)
    lam_im = a_im[...]
    dt = jnp.exp(log_dt[...])
    mag = jnp.exp(lam_re * dt)
    lb_re = mag * jnp.cos(lam_im * dt)
    lb_im = mag * jnp.sin(lam_im * dt)
    den = lam_re * lam_re + lam_im * lam_im
    nr = lb_re - 1.0
    f_re = (nr * lam_re + lb_im * lam_im) / den
    f_im = (lb_im * lam_re - nr * lam_im) / den
    br = b_re[...]
    bi = b_im[...]
    lb_re_o[...] = lb_re
    lb_im_o[...] = lb_im
    bb_re_o[...] = f_re * br - f_im * bi
    bb_im_o[...] = f_re * bi + f_im * br


def _zoh_prep(a_re, a_im, log_dt, b_re, b_im):
    gn = SSM_STATES
    col = lambda v: v.astype(F32).reshape(gn, 1)
    dt_col = jnp.broadcast_to(log_dt.astype(F32)[:, None], (SSM_GROUPS, SSM_STATE)).reshape(gn, 1)
    outs = pl.pallas_call(
        _zoh_kernel,
        out_shape=[jax.ShapeDtypeStruct((gn, 1), F32)] * 2 + [jax.ShapeDtypeStruct((gn, SSM_GROUP), F32)] * 2,
        name="zoh_prep",
    )(col(a_re), col(a_im), dt_col, b_re.astype(F32).reshape(gn, SSM_GROUP), b_im.astype(F32).reshape(gn, SSM_GROUP))
    lb_re, lb_im, bb_re, bb_im = outs
    shape_b = (SSM_GROUPS, SSM_STATE, SSM_GROUP)
    return lb_re.reshape(1, gn), lb_im.reshape(1, gn), bb_re.reshape(shape_b), bb_im.reshape(shape_b)


def _inproj_kernel(x_ref, g_ref, b_ref, wu_ref, wqt_ref, wk_ref, wvst_ref, wvwt_ref, wgt_ref,
                   h_o, u_o, qt_o, kvc_o, ks_o, kw_o, vst_o, vwt_o, gt_o):
    tl = x_ref.shape[1]
    h = _layer_norm(x_ref[0], g_ref[...], b_ref[...])
    h_o[0] = h
    hb = h.astype(BF16)
    u_o[0] = _dot(hb, wu_ref[...])
    qt_o[0] = (_dot_nt(wqt_ref[...], hb) * (HEAD_DIM ** -0.5 * LOG2E)).astype(BF16)

    k_all = _dot(hb, wk_ref[...])

    def head_tile(j):
        tile = k_all[:, (j // 2) * KEY_LANES:(j // 2 + 1) * KEY_LANES]
        return pltpu.roll(tile, HEAD_DIM, 1) if j % 2 else tile

    for j in range(N_KV_HEADS):
        kvc_o[0, j] = k_all[:, j * KEY_LANES:(j + 1) * KEY_LANES]

    lane = lax.broadcasted_iota(jnp.int32, (tl, KEY_LANES), 1)
    pos = pl.program_id(1) * tl + lax.broadcasted_iota(jnp.int32, (tl, KEY_LANES), 0)
    alibi = jnp.where((lane >= ALIBI_LANE) & (lane < ALIBI_LANE + 3), (pos % ATT_TILE).astype(F32), 0.0)
    slc_feat = alibi + jnp.where((lane >= ONEHOT_LANE) & (lane - ONEHOT_LANE == pos // SLC_BLOCK), 1.0, 0.0)
    is_key = lane < HEAD_DIM
    for j in range(N_KV_HEADS):
        ks_o[0, j] = jnp.where(is_key, head_tile(2 * N_KV_HEADS + j), slc_feat).astype(BF16)
        kw_o[0, j] = jnp.where(is_key, head_tile(3 * N_KV_HEADS + j), alibi).astype(BF16)
    row = lax.broadcasted_iota(jnp.int32, (N_KV_HEADS * VT_ROWS, tl), 0)
    ones_row = jnp.where(row % VT_ROWS == HEAD_DIM, 1.0, 0.0)
    vst_o[0] = (_dot_nt(wvst_ref[...], hb) + ones_row).astype(BF16)
    vwt_o[0] = (_dot_nt(wvwt_ref[...], hb) + ones_row).astype(BF16)
    gt_o[0] = _sigmoid(_dot_nt(wgt_ref[...], hb))


def _ln_inproj(x, ln_g, ln_b, w, tl):
    bsz, L, d = x.shape
    n_gate = N_KV_HEADS * GATE_ROWS
    grid = (bsz, L // tl)
    in_specs = [
        pl.BlockSpec((1, tl, d), lambda b, i: (b, i, 0)),
        _full((1, d)), _full((1, d)),
        _full((d, SSM_WIDTH)),
        _full((D_MODEL, d)),
        _full((d, 4 * KV_DIM)),
        _full((N_KV_HEADS * VT_ROWS, d)),
        _full((N_KV_HEADS * VT_ROWS, d)),
        _full((n_gate, d)),
    ]
    assert tl % ATT_TILE == 0 or ATT_TILE % tl == 0
    assert ONEHOT_LANE + L // SLC_BLOCK <= KEY_LANES
    out_shape = [
        jax.ShapeDtypeStruct((bsz, L, d), F32),
        jax.ShapeDtypeStruct((bsz, L, SSM_WIDTH), F32),
        jax.ShapeDtypeStruct((bsz, D_MODEL, L), BF16),
        jax.ShapeDtypeStruct((bsz, N_KV_HEADS, L, 2 * HEAD_DIM), F32),
        jax.ShapeDtypeStruct((bsz, N_KV_HEADS, L, KEY_LANES), BF16),
        jax.ShapeDtypeStruct((bsz, N_KV_HEADS, L, KEY_LANES), BF16),
        jax.ShapeDtypeStruct((bsz, N_KV_HEADS * VT_ROWS, L), BF16),
        jax.ShapeDtypeStruct((bsz, N_KV_HEADS * VT_ROWS, L), BF16),
        jax.ShapeDtypeStruct((bsz, n_gate, L), F32),
    ]
    out_specs = [
        pl.BlockSpec((1, tl, d), lambda b, i: (b, i, 0)),
        pl.BlockSpec((1, tl, SSM_WIDTH), lambda b, i: (b, i, 0)),
        pl.BlockSpec((1, D_MODEL, tl), lambda b, i: (b, 0, i)),
        pl.BlockSpec((1, N_KV_HEADS, tl, 2 * HEAD_DIM), lambda b, i: (b, 0, i, 0)),
        pl.BlockSpec((1, N_KV_HEADS, tl, KEY_LANES), lambda b, i: (b, 0, i, 0)),
        pl.BlockSpec((1, N_KV_HEADS, tl, KEY_LANES), lambda b, i: (b, 0, i, 0)),
        pl.BlockSpec((1, N_KV_HEADS * VT_ROWS, tl), lambda b, i: (b, 0, i)),
        pl.BlockSpec((1, N_KV_HEADS * VT_ROWS, tl), lambda b, i: (b, 0, i)),
        pl.BlockSpec((1, n_gate, tl), lambda b, i: (b, 0, i)),
    ]
    return pl.pallas_call(
        _inproj_kernel, grid=grid, in_specs=in_specs, out_specs=out_specs, out_shape=out_shape,
        compiler_params=_cparams("parallel", "parallel"), name="ln_inproj",
    )(x, ln_g.reshape(1, d), ln_b.reshape(1, d), w["wu"], w["wqt"], w["wk"],
      w["wvst"], w["wvwt"], w["wgt"])


def _s5_kernel(u_ref, lre_ref, lim_ref, wbre_ref, wbim_ref, cre_ref, cim_ref, d_ref, wglu_ref, bglu_ref, wo_ref,
               y_o, sre, sim, hre, him, *, bsz, steps, pitch):
    @pl.when(pl.program_id(0) == 0)
    def _():
        sre[...] = jnp.zeros_like(sre)
        sim[...] = jnp.zeros_like(sim)

    lanes = SSM_LANE_SLAB
    per_slab = SSM_STATE_SLAB // lanes
    u = u_ref[...].reshape(bsz * steps, SSM_WIDTH)
    ub = u.astype(BF16)

    def project_in(j):
        uj = ub[:, j * lanes:(j + 1) * lanes]
        for w_ref, h_ref in ((wbre_ref, hre), (wbim_ref, him)):
            r = _dot(uj, w_ref[j])
            for b in range(bsz):
                for k in range(per_slab):
                    h_ref[j * per_slab + k, pl.ds(b, steps, stride=pitch), :] = (
                        r[b * steps:(b + 1) * steps, k * lanes:(k + 1) * lanes])

    def recur(j):
        for s in range(j * per_slab, (j + 1) * per_slab):
            cols = slice(s * lanes, (s + 1) * lanes)
            lr = jnp.broadcast_to(lre_ref[:, cols], (bsz, lanes))
            li = jnp.broadcast_to(lim_ref[:, cols], (bsz, lanes))
            pr, pi = sre[:, cols], sim[:, cols]
            for t in range(steps):
                rows = pl.ds(t * pitch, bsz)
                pr, pi = (lr * pr - li * pi + hre[s, rows, :], lr * pi + li * pr + him[s, rows, :])
                hre[s, rows, :] = pr
                him[s, rows, :] = pi
            sre[:, cols] = pr
            sim[:, cols] = pi

    def states(h_ref, j):
        return jnp.concatenate(
            [jnp.concatenate([h_ref[j * per_slab + k, pl.ds(b, steps, stride=pitch), :] for k in range(per_slab)],
                             axis=-1) for b in range(bsz)], axis=0)

    ys = [None] * SSM_SLABS

    def project_out(j):
        ys[j] = _dot(states(hre, j).astype(BF16), cre_ref[j]) - _dot(states(him, j).astype(BF16), cim_ref[j])

    for j in range(SSM_SLABS + 2):
        if j < SSM_SLABS:
            project_in(j)
        if 0 <= j - 1 < SSM_SLABS:
            recur(j - 1)
        if 0 <= j - 2 < SSM_SLABS:
            project_out(j - 2)
    y = jnp.concatenate(ys, axis=-1) + d_ref[...] * u
    g = _gelu_tanh(y)
    y2 = g * _sigmoid(_dot(g.astype(BF16), wglu_ref[...]) + bglu_ref[...])
    y_o[...] = _dot(y2.astype(BF16), wo_ref[...]).reshape(bsz, steps, D_MODEL)


def _s5(u, lb_re, lb_im, w, steps):
    bsz, L, _ = u.shape
    pitch = -(-bsz // 4) * 4
    pitch += 4 if (pitch // 4) % 2 == 0 else 0
    grid = (L // steps,)
    kern = functools.partial(_s5_kernel, bsz=bsz, steps=steps, pitch=pitch)
    n_slabs = SSM_STATES // SSM_LANE_SLAB
    in_specs = [
        pl.BlockSpec((bsz, steps, SSM_WIDTH), lambda c: (0, c, 0)),
        _full((1, SSM_STATES)), _full((1, SSM_STATES)),
        _full((SSM_SLABS, SSM_LANE_SLAB, SSM_STATE_SLAB)), _full((SSM_SLABS, SSM_LANE_SLAB, SSM_STATE_SLAB)),
        _full((SSM_SLABS, SSM_STATE_SLAB, SSM_LANE_SLAB)), _full((SSM_SLABS, SSM_STATE_SLAB, SSM_LANE_SLAB)),
        _full((1, SSM_WIDTH)),
        _full((SSM_WIDTH, SSM_WIDTH)), _full((1, SSM_WIDTH)),
        _full((SSM_WIDTH, D_MODEL)),
    ]
    return pl.pallas_call(
        kern, grid=grid, in_specs=in_specs,
        out_specs=pl.BlockSpec((bsz, steps, D_MODEL), lambda c: (0, c, 0)),
        out_shape=jax.ShapeDtypeStruct((bsz, L, D_MODEL), F32),
        scratch_shapes=[pltpu.VMEM((bsz, SSM_STATES), F32), pltpu.VMEM((bsz, SSM_STATES), F32),
                        pltpu.VMEM((n_slabs, steps * pitch, SSM_LANE_SLAB), F32),
                        pltpu.VMEM((n_slabs, steps * pitch, SSM_LANE_SLAB), F32)],
        compiler_params=_cparams("arbitrary"), name="s5",
    )(u, lb_re, lb_im, w["wbre"], w["wbim"], w["cre"], w["cim"], w["d"], w["wglu"], w["bglu"], w["wo"])


def _compress_kernel(kv_ref, pos_ref, w1_ref, b1_ref, w2k_ref, w2vt_ref, kc_o, vct_o, *, n_chunks):
    half = CMP_BLOCK // 2
    quad = CMP_QUAD
    low_half = lax.broadcasted_iota(jnp.int32, (n_chunks, 2 * HEAD_DIM), 1) < HEAD_DIM
    for tile in range(2 * N_KV_HEADS // 2):
        z = tile // (N_KV_HEADS // 2)
        heads = (2 * (tile % (N_KV_HEADS // 2)), 2 * (tile % (N_KV_HEADS // 2)) + 1)
        first = [jnp.zeros((n_chunks, CMP_HIDDEN), F32) for _ in heads]
        second = [jnp.zeros((n_chunks, CMP_HIDDEN), F32) for _ in heads]
        for q in range(half // quad):
            rows = [kv_ref[0, tile, pl.ds(q * quad + r, n_chunks, stride=CMP_STRIDE), :] for r in range(quad)]
            turned = [pltpu.roll(x, HEAD_DIM, 1) for x in rows]
            for side in range(2):
                pairs = [jnp.where(low_half, rows[r], turned[r + 1]) if side == 0 else
                         jnp.where(low_half, turned[r], rows[r + 1]) for r in range(0, quad, 2)]
                x4 = jnp.concatenate(pairs, axis=-1)
                for part, acc in ((0, first), (1, second)):
                    qq = part * (half // quad) + q
                    w = w1_ref[z, qq * quad * HEAD_DIM:(qq + 1) * quad * HEAD_DIM, :]
                    acc[side] += _dot((x4 + pos_ref[z, qq:qq + 1, :]).astype(BF16), w)
        for side, hh in enumerate(heads):
            pre = first[side] + pltpu.roll(second[side], n_chunks - 1, 0) + b1_ref[z]
            hid = _gelu_tanh(pre).astype(BF16)
            if z == 0:
                kc_o[0, hh] = _dot(hid, w2k_ref[...]).astype(BF16)
            else:
                vct_o[0, hh * HEAD_DIM:(hh + 1) * HEAD_DIM, :] = _dot_nt(w2vt_ref[...], hid).astype(BF16)


def _compress(kvc, pos, w1, b1, w2k, w2vt):
    bsz, _, L, _ = kvc.shape
    n_chunks = L // CMP_STRIDE
    kern = functools.partial(_compress_kernel, n_chunks=n_chunks)
    return pl.pallas_call(
        kern, grid=(bsz,),
        in_specs=[
            pl.BlockSpec((1, N_KV_HEADS, L, 2 * HEAD_DIM), lambda b: (b, 0, 0, 0)),
            _full((2, CMP_BLOCK // CMP_QUAD, CMP_QUAD * HEAD_DIM)),
            _full((2, CMP_BLOCK * HEAD_DIM, CMP_HIDDEN)),
            _full((2, 1, CMP_HIDDEN)),
            _full((CMP_HIDDEN, HEAD_DIM)),
            _full((HEAD_DIM, CMP_HIDDEN)),
        ],
        out_specs=[
            pl.BlockSpec((1, N_KV_HEADS, n_chunks, HEAD_DIM), lambda b: (b, 0, 0, 0)),
            pl.BlockSpec((1, KV_DIM, n_chunks), lambda b: (b, 0, 0)),
        ],
        out_shape=[
            jax.ShapeDtypeStruct((bsz, N_KV_HEADS, n_chunks, HEAD_DIM), BF16),
            jax.ShapeDtypeStruct((bsz, KV_DIM, n_chunks), BF16),
        ],
        compiler_params=_cparams("parallel"), name="compress",
    )(kvc, pos, w1, b1, w2k, w2vt)


def _nsa_kernel(*refs, tq, n_cmp_pad, n_slc):
    def head_pair(pair, carry):
        heads = [_nsa_head(HEADS_IN_FLIGHT * pair + slot, slot, *refs, tq=tq, n_cmp_pad=n_cmp_pad, n_slc=n_slc)
                 for slot in range(HEADS_IN_FLIGHT)]
        tails, live, step = {}, set(range(HEADS_IN_FLIGHT)), 0
        while live:
            for slot in sorted(live):
                if step >= slot * HEAD_PHASE_LAG:
                    try:
                        out = next(heads[slot])
                        if out is not None:
                            tails[slot] = out
                    except StopIteration:
                        live.discard(slot)
            step += 1
        for slot in range(HEADS_IN_FLIGHT):
            tails[slot]()
        return carry

    lax.fori_loop(0, N_KV_HEADS // HEADS_IN_FLIGHT, head_pair, 0)


def _nsa_head(hkv, par, qt_ref, kc_ref, vct_ref, ks_ref, vst_ref, kw_ref, vwt_ref, gt_ref, ov_ref, causal_ref,
              far_ref, chunk_ref, o_ref, qa_ref, qw_ref, m_ref, acc_ref, s_ref, mx_ref, al_ref, sc_ref, pc_ref,
              need_ref, *, tq, n_cmp_pad, n_slc):
    i = pl.program_id(1)
    slots_per_head = s_ref.shape[0] // HEADS_IN_FLIGHT
    t0 = i * tq
    tk = tq
    n_sel = min(SLC_TOP_N, n_slc)
    slopes = [jnp.exp2(jnp.full((1, tq), -0.5, F32) * jnp.asarray(hkv * GQA_GROUP + g + 1, F32)) * LOG2E
              for g in range(GQA_GROUP)]
    head_rows = lambda g: pl.ds(pl.multiple_of((hkv * GQA_GROUP + g) * HEAD_DIM, HEAD_DIM), HEAD_DIM)
    vt_rows = pl.ds(pl.multiple_of(hkv * VT_ROWS, 16), VT_ROWS)
    q_heads = [qt_ref[0, head_rows(g), :] for g in range(GQA_GROUP)]
    gate = lambda g, z: gt_ref[0, pl.ds(hkv * GATE_ROWS + 3 * g + z, 1), :]

    t_lane = t0 + lax.broadcasted_iota(jnp.int32, (1, tq), 1)
    row8 = lax.broadcasted_iota(jnp.int32, (ONEHOT_LANE - ALIBI_LANE, tq), 0)

    def augmented(g, tail_rows):
        hi = slopes[g].astype(BF16).astype(F32)
        mid = (slopes[g] - hi).astype(BF16).astype(F32)
        lo = slopes[g] - hi - mid
        parts = jnp.where(row8 == 0, hi, jnp.where(row8 == 1, mid, jnp.where(row8 == 2, lo, 0.0)))
        return jnp.concatenate([q_heads[g].astype(F32), parts, tail_rows], axis=0).astype(BF16)

    def attend(state, q_ref, items, first, slot0):
        starts = [pl.multiple_of((c if valid is None else jnp.maximum(c, 0)) * tk, tk) for _, _, c, _, valid in items]
        slot0 = slot0 + par * slots_per_head

        def scores(g):
            for n, (k_ref, _, _, bias_ref, _) in enumerate(items):
                s = _dot(k_ref[0, hkv, pl.ds(starts[n], tk), :], q_ref[par, g])
                if bias_ref is not None:
                    s = s + bias_ref[...]
                s_ref[slot0 + n * GQA_GROUP + g] = s
                mx_ref[slot0 + n * GQA_GROUP + g] = jnp.max(s, axis=0, keepdims=True)

        def fold(g):
            shifts = []
            for _, _, c, _, valid in items:
                shift = slopes[g] * jnp.asarray((c - i) * tk, F32)
                shifts.append(shift if valid is None else jnp.where(valid, shift, NEG_BIG))
            m_new = functools.reduce(jnp.maximum, [mx_ref[slot0 + n * GQA_GROUP + g] + shifts[n]
                                                   for n in range(len(items))])
            if not first:
                m_new = jnp.maximum(m_ref[par, state, g], m_new)
                al_ref[par, g] = jnp.exp2(m_ref[par, state, g] - m_new)
            pv = None
            for n, (_, vt_ref, _, _, _) in enumerate(items):
                slot = slot0 + n * GQA_GROUP + g
                p = jnp.exp2(s_ref[slot] - (m_new - shifts[n])).astype(BF16)
                part = _dot(vt_ref[0, vt_rows, pl.ds(starts[n], tk)], p)
                pv = part if pv is None else pv + part
            acc_ref[par, state, g] = pv if first else al_ref[par, g] * acc_ref[par, state, g] + pv
            m_ref[par, state, g] = m_new

        scores(0)
        for g in range(GQA_GROUP):
            if g + 1 < GQA_GROUP:
                scores(g + 1)
            fold(g)
            yield

    def emit(state, z, accumulate):
        for g in range(GQA_GROUP):
            scale = gate(g, z) / acc_ref[par, state, g, HEAD_DIM:HEAD_DIM + 1, :]
            out = acc_ref[par, state, g, :HEAD_DIM, :] * scale
            o_ref[0, head_rows(g), :] = o_ref[0, head_rows(g), :] + out if accumulate else out

    n_idx = lax.broadcasted_iota(jnp.int32, (n_cmp_pad, tq), 0)
    dist_c = (t0 + lax.broadcasted_iota(jnp.int32, (n_cmp_pad, tq), 1)) - (n_idx * CMP_STRIDE + (CMP_BLOCK - 1))
    mask_c = dist_c >= 0
    dist_cf = dist_c.astype(F32)
    kc = kc_ref[0, hkv]
    vct = vct_ref[0, pl.ds(pl.multiple_of(hkv * HEAD_DIM, HEAD_DIM), HEAD_DIM), :]
    for g in range(GQA_GROUP):
        s = _dot(kc, q_heads[g]) - slopes[g] * dist_cf
        sc_ref[par, g] = jnp.where(mask_c, s, NEG_BIG)
    yield
    p_sum = jnp.zeros((n_cmp_pad, tq), F32)
    for g in range(GQA_GROUP):
        s = sc_ref[par, g]
        m = jnp.max(s, axis=0, keepdims=True)
        e = jnp.exp2(s - m)
        norm = 1.0 / jnp.maximum(jnp.sum(e, axis=0, keepdims=True), 1e-30)
        p = e * jnp.where(m > 0.5 * NEG_BIG, norm, 0.0)
        pc_ref[par, g] = p.astype(BF16)
        p_sum = p_sum + p
    for g in range(GQA_GROUP):
        o_ref[0, head_rows(g), :] = gate(g, 0) * _dot(vct, pc_ref[par, g])

    imp = lax.dot_general(ov_ref[...], p_sum, (((1,), (0,)), ((), ())), precision=lax.Precision.HIGHEST,
                          preferred_element_type=F32)
    yield
    blk = lax.broadcasted_iota(jnp.int32, (n_slc, tq), 0)
    cur = t_lane // SLC_BLOCK
    forced = (blk == 0) | (blk == cur) | (blk == cur - 1)
    future = blk * SLC_BLOCK > t_lane
    imp = jnp.where(forced, SEL_BIG, jnp.where(future, -SEL_BIG, imp))
    sub8 = 8
    tiles = [imp[r:r + sub8] for r in range(0, n_slc, sub8)]
    ranks = [jnp.zeros((sub8, tq), jnp.int32) for _ in tiles]
    for j in range(n_slc):
        row = imp[j:j + 1, :]
        for r, tile in enumerate(tiles):
            if r * sub8 > j:
                ahead = row >= tile
            elif r * sub8 + sub8 - 1 < j:
                ahead = row > tile
            else:
                later = r * sub8 + lax.broadcasted_iota(jnp.int32, (sub8, tq), 0) > j
                ahead = (row > tile) | ((row == tile) & later)
            ranks[r] = ranks[r] + ahead.astype(jnp.int32)
    rank = jnp.concatenate(ranks, axis=0)
    sel_bias = jnp.where(rank < n_sel, 0.0, NEG_BIG)

    qa_rows = jnp.concatenate([sel_bias, jnp.zeros((KEY_LANES - ONEHOT_LANE - n_slc, tq), F32)], axis=0)
    for g in range(GQA_GROUP):
        qa_ref[par, g] = augmented(g, qa_rows)
    yield

    zero_rows = jnp.zeros((KEY_LANES - ONEHOT_LANE, tq), F32)
    for g in range(GQA_GROUP):
        qw_ref[par, g] = augmented(g, zero_rows)
    n_back = WINDOW // tk
    n_win_items = n_back + 1
    win_items = [(kw_ref, vwt_ref, i, causal_ref, None)]
    for back in range(1, n_back + 1):
        win_items.append((kw_ref, vwt_ref, i - back, far_ref if back == n_back else None, i >= back))
    yield from attend(0, qw_ref, win_items, True, 0)
    emit(0, 2, True)
    yield

    slc_items = [(ks_ref, vst_ref, i, causal_ref, None)]
    for back in range(1, SLC_LEAD_CHUNKS):
        slc_items.append((ks_ref, vst_ref, i - back, None, i >= back))
    slc_items.append((ks_ref, vst_ref, 0, None, i >= SLC_LEAD_CHUNKS))
    yield from attend(1, qa_ref, slc_items, True, n_win_items * GQA_GROUP)

    picked = (rank < n_sel).astype(BF16)
    per_chunk = jnp.sum(_dot(chunk_ref[...], picked), axis=1, keepdims=True)
    need0 = par * CHUNK_ROWS
    for c in range(n_slc * SLC_BLOCK // tk):
        need_ref[need0 + c] = (per_chunk[c, 0] > 0.0).astype(jnp.int32)
    last = i - SLC_LEAD_CHUNKS

    def slc_step(pair, carry):
        c0 = 1 + 2 * pair
        c1 = jnp.minimum(c0 + 1, last)
        use0 = need_ref[need0 + c0] > 0
        use1 = (c0 + 1 <= last) & (need_ref[need0 + c1] > 0)

        @pl.when(use0 | use1)
        def _():
            for _ in attend(1, qa_ref, [(ks_ref, vst_ref, c0, None, use0), (ks_ref, vst_ref, c1, None, use1)], False, 0):
                pass
        return carry

    def tail():
        lax.fori_loop(0, jnp.maximum(last + 1, 0) // 2, slc_step, 0)
        emit(1, 1, True)

    yield tail


def _nsa_attn(qt, kc, vct, ks, vst, kw, vwt, gt, tq):
    bsz, _, L = qt.shape
    n_cmp_pad = kc.shape[2]
    n_slc = L // SLC_BLOCK
    n = jnp.arange(n_cmp_pad)[None, :]
    j = jnp.arange(n_slc)[:, None]
    overlap_t = ((n * CMP_STRIDE < (j + 1) * SLC_BLOCK) & (n * CMP_STRIDE + CMP_BLOCK - 1 >= j * SLC_BLOCK)).astype(F32)
    assert tq == ATT_TILE and WINDOW % tq == 0 and n_slc % 8 == 0
    n_slots = (WINDOW // tq + 1 + SLC_LEAD_CHUNKS + 1) * GQA_GROUP
    sub = jnp.arange(tq)[:, None]
    lane = jnp.arange(tq)[None, :]
    causal_bias = jnp.where(sub <= lane, 0.0, NEG_BIG).astype(F32)
    far_bias = jnp.where(sub > lane, 0.0, NEG_BIG).astype(F32)
    assert L // tq <= CHUNK_ROWS
    chunk_of_block = (jnp.arange(CHUNK_ROWS)[:, None] == jnp.arange(n_slc)[None, :] * SLC_BLOCK // tq).astype(BF16)
    kern = functools.partial(_nsa_kernel, tq=tq, n_cmp_pad=n_cmp_pad, n_slc=n_slc)
    return pl.pallas_call(
        kern, grid=(bsz, L // tq),
        in_specs=[
            pl.BlockSpec((1, D_MODEL, tq), lambda b, i: (b, 0, i)),
            pl.BlockSpec((1, N_KV_HEADS, n_cmp_pad, HEAD_DIM), lambda b, i: (b, 0, 0, 0)),
            pl.BlockSpec((1, KV_DIM, n_cmp_pad), lambda b, i: (b, 0, 0)),
            pl.BlockSpec((1, N_KV_HEADS, L, KEY_LANES), lambda b, i: (b, 0, 0, 0)),
            pl.BlockSpec((1, N_KV_HEADS * VT_ROWS, L), lambda b, i: (b, 0, 0)),
            pl.BlockSpec((1, N_KV_HEADS, L, KEY_LANES), lambda b, i: (b, 0, 0, 0)),
            pl.BlockSpec((1, N_KV_HEADS * VT_ROWS, L), lambda b, i: (b, 0, 0)),
            pl.BlockSpec((1, N_KV_HEADS * GATE_ROWS, tq), lambda b, i: (b, 0, i)),
            _full((n_slc, n_cmp_pad)),
            _full((tq, tq)), _full((tq, tq)),
            _full((CHUNK_ROWS, n_slc)),
        ],
        out_specs=pl.BlockSpec((1, D_MODEL, tq), lambda b, i: (b, 0, i)),
        out_shape=jax.ShapeDtypeStruct((bsz, D_MODEL, L), F32),
        scratch_shapes=[pltpu.VMEM((HEADS_IN_FLIGHT, GQA_GROUP, KEY_LANES, tq), BF16),
                        pltpu.VMEM((HEADS_IN_FLIGHT, GQA_GROUP, KEY_LANES, tq), BF16),
                        pltpu.VMEM((HEADS_IN_FLIGHT, 2, GQA_GROUP, 1, tq), F32),
                        pltpu.VMEM((HEADS_IN_FLIGHT, 2, GQA_GROUP, VT_ROWS, tq), F32),
                        pltpu.VMEM((HEADS_IN_FLIGHT * n_slots, tq, tq), F32),
                        pltpu.VMEM((HEADS_IN_FLIGHT * n_slots, 1, tq), F32),
                        pltpu.VMEM((HEADS_IN_FLIGHT, GQA_GROUP, 1, tq), F32),
                        pltpu.VMEM((HEADS_IN_FLIGHT, GQA_GROUP, n_cmp_pad, tq), F32),
                        pltpu.VMEM((HEADS_IN_FLIGHT, GQA_GROUP, n_cmp_pad, tq), BF16),
                        pltpu.SMEM((HEADS_IN_FLIGHT * CHUNK_ROWS,), jnp.int32)],
        compiler_params=_cparams("parallel", "parallel"), name="nsa_attn",
    )(qt, kc, vct, ks, vst, kw, vwt, gt, overlap_t, causal_bias, far_bias, chunk_of_block)


def _mix_kernel(h_ref, ys_ref, ot_ref, wgm_ref, wno_ref, wout_ref, g_ref, b_ref, h1_o):
    h = h_ref[0]
    o = ot_ref[0].T.astype(BF16)
    y_nsa = _dot(o, wno_ref[...])
    gm = _sigmoid(_dot(h.astype(BF16), wgm_ref[...]))
    mix = gm[:, :D_MODEL] * ys_ref[0] + gm[:, D_MODEL:] * y_nsa
    mixed = _dot(mix.astype(BF16), wout_ref[...])
    h1_o[0] = _layer_norm(DEEPNORM_ALPHA * h + mixed, g_ref[...], b_ref[...])


def _mix_ln1(h, y_ssm, o_t, wgm, wno, wout, g, b, tm):
    bsz, L, d = h.shape
    return pl.pallas_call(
        _mix_kernel, grid=(bsz, L // tm),
        in_specs=[
            pl.BlockSpec((1, tm, d), lambda b, i: (b, i, 0)),
            pl.BlockSpec((1, tm, d), lambda b, i: (b, i, 0)),
            pl.BlockSpec((1, d, tm), lambda b, i: (b, 0, i)),
            _full((d, 2 * d)), _full((d, d)), _full((d, d)), _full((1, d)), _full((1, d)),
        ],
        out_specs=pl.BlockSpec((1, tm, d), lambda b, i: (b, i, 0)),
        out_shape=jax.ShapeDtypeStruct((bsz, L, d), F32),
        compiler_params=_cparams("parallel", "parallel"), name="mix_ln1",
    )(h, y_ssm, o_t, wgm, wno, wout, g.reshape(1, d), b.reshape(1, d))


def _memkv_kernel(mem_ref, w_ref, k_o, v_o):
    kv = _dot(mem_ref[0].astype(BF16), w_ref[...])
    k_o[0] = kv[:, :D_MODEL].astype(BF16)
    v_o[0] = kv[:, D_MODEL:].astype(BF16)


def _mem_kv(mem, w_kv):
    bsz, m, d = mem.shape
    spec = pl.BlockSpec((1, m, d), lambda b: (b, 0, 0))
    return pl.pallas_call(
        _memkv_kernel, grid=(bsz,),
        in_specs=[spec, _full((d, 2 * d))],
        out_specs=[spec, spec],
        out_shape=[jax.ShapeDtypeStruct((bsz, m, d), BF16)] * 2,
        compiler_params=_cparams("parallel"), name="mem_kv",
    )(mem, w_kv)


def _xattn_kernel(h_ref, k_ref, v_ref, wq_ref, wo_ref, g_ref, b_ref, h2_o):
    h = h_ref[0]
    q = (_dot(h.astype(BF16), wq_ref[...]) * (X_HEAD_DIM ** -0.5 * LOG2E)).astype(BF16)
    cols = [slice(hd * X_HEAD_DIM, (hd + 1) * X_HEAD_DIM) for hd in range(X_HEADS)]
    scores = lambda hd: _dot_nt(q[:, cols[hd]], k_ref[0, :, cols[hd]])
    outs = []
    s_next = scores(0)
    for hd in range(X_HEADS):
        s = s_next
        if hd + 1 < X_HEADS:
            s_next = scores(hd + 1)
        p = jnp.exp2(s - jnp.max(s, axis=-1, keepdims=True))
        norm = 1.0 / jnp.sum(p, axis=-1, keepdims=True)
        outs.append(_dot(p.astype(BF16), v_ref[0, :, cols[hd]]) * norm)
    o = jnp.concatenate(outs, axis=-1).astype(BF16)
    h2_o[0] = _layer_norm(DEEPNORM_ALPHA * h + _dot(o, wo_ref[...]), g_ref[...], b_ref[...])


def _xattn_ln2(h, k, v, wq, wo, g, b, tm):
    bsz, L, d = h.shape
    m = k.shape[1]
    row = pl.BlockSpec((1, tm, d), lambda b, i: (b, i, 0))
    kv = pl.BlockSpec((1, m, d), lambda b, i: (b, 0, 0))
    return pl.pallas_call(
        _xattn_kernel, grid=(bsz, L // tm),
        in_specs=[row, kv, kv, _full((d, d)), _full((d, d)), _full((1, d)), _full((1, d))],
        out_specs=row,
        out_shape=jax.ShapeDtypeStruct((bsz, L, d), F32),
        compiler_params=_cparams("parallel", "parallel"), name="xattn_ln2",
    )(h, k, v, wq, wo, g.reshape(1, d), b.reshape(1, d))


def _ffn_kernel(h_ref, win_ref, wout_ref, g_ref, b_ref, o_ref):
    h = h_ref[...]
    gu = _dot(h.astype(BF16), win_ref[...])
    gate = gu[:, :D_FF]
    act = gate * _sigmoid(gate) * gu[:, D_FF:]
    o_ref[...] = _layer_norm(DEEPNORM_ALPHA * h + _dot(act.astype(BF16), wout_ref[...]), g_ref[...], b_ref[...])


def _ffn_ln3(h, win, wout, g, b, tm):
    rows, d = h.shape
    row = pl.BlockSpec((tm, d), lambda i: (i, 0))
    return pl.pallas_call(
        _ffn_kernel, grid=(rows // tm,),
        in_specs=[row, _full((d, 2 * D_FF)), _full((D_FF, d)), _full((1, d)), _full((1, d))],
        out_specs=row,
        out_shape=jax.ShapeDtypeStruct((rows, d), F32),
        compiler_params=_cparams("parallel"), name="ffn_ln3",
    )(h, win, wout, g.reshape(1, d), b.reshape(1, d))


def _inproj_weights(w_in):
    d = D_MODEL
    c0 = SSM_WIDTH
    c1 = c0 + N_HEADS * HEAD_DIM
    c2 = c1 + 2 * KV_DIM
    c3 = c2 + 2 * KV_DIM
    c4 = c3 + 2 * KV_DIM
    c5 = c4 + 3 * N_HEADS
    values_t = lambda m: jnp.pad(m.T.reshape(N_KV_HEADS, HEAD_DIM, d),
                                 ((0, 0), (0, VT_ROWS - HEAD_DIM), (0, 0))).reshape(N_KV_HEADS * VT_ROWS, d)
    wg = w_in[:, c4:c5].T.reshape(N_KV_HEADS, 3 * GQA_GROUP, d)
    wg = jnp.pad(wg, ((0, 0), (0, GATE_ROWS - 3 * GQA_GROUP), (0, 0))).reshape(N_KV_HEADS * GATE_ROWS, d)
    return {
        "wu": w_in[:, :c0].astype(BF16),
        "wqt": w_in[:, c0:c1].T.astype(BF16),
        "wk": jnp.concatenate([w_in[:, c1:c2], w_in[:, c2:c2 + KV_DIM], w_in[:, c3:c3 + KV_DIM]], axis=1).astype(BF16),
        "wvst": values_t(w_in[:, c2 + KV_DIM:c3]).astype(BF16),
        "wvwt": values_t(w_in[:, c3 + KV_DIM:c4]).astype(BF16),
        "wgt": wg.astype(BF16),
        "wgm": w_in[:, c5:].astype(BF16),
    }


def _s5_weights(bb_re, bb_im, c_re, c_im, d_skip, w_glu, b_glu, w_o):
    per_slab = SSM_LANE_SLAB // SSM_GROUP
    eye = jnp.eye(per_slab, dtype=F32)

    def b_blocks(bb):
        bb = bb.reshape(SSM_SLABS, per_slab, SSM_STATE, SSM_GROUP)
        return jnp.einsum('jgnc,gh->jgchn', bb, eye).reshape(SSM_SLABS, SSM_LANE_SLAB, SSM_STATE_SLAB).astype(BF16)

    def c_blocks(c):
        c = c.astype(F32).reshape(SSM_SLABS, per_slab, SSM_GROUP, SSM_STATE)
        return jnp.einsum('jgcn,gh->jgnhc', c, eye).reshape(SSM_SLABS, SSM_STATE_SLAB, SSM_LANE_SLAB).astype(BF16)

    return {
        "wbre": b_blocks(bb_re), "wbim": b_blocks(bb_im), "cre": c_blocks(c_re), "cim": c_blocks(c_im),
        "d": d_skip.astype(F32).reshape(1, SSM_WIDTH), "wglu": w_glu.astype(BF16),
        "bglu": b_glu.astype(F32).reshape(1, SSM_WIDTH), "wo": w_o.astype(BF16),
    }


def _pick(total, want):
    return want if total % want == 0 else total


def kernel(x, mem, ln_emb_g, ln_emb_b, w_in, ssm_a_re, ssm_a_im, ssm_b_re, ssm_b_im, ssm_c_re, ssm_c_im, ssm_d,
           ssm_log_dt, ssm_w_glu, ssm_b_glu, ssm_w_out, cmp_pos, cmp_w1, cmp_b1, cmp_w2, nsa_w_out, w_out,
           ln1_g, ln1_b, xattn_w_q, xattn_w_kv, xattn_w_o, ln2_g, ln2_b, ffn_w_in, ffn_w_out, ln3_g, ln3_b):
    bsz, L, d = x.shape
    assert w_in.shape[0] == 1, "one layer: the trunk-entry LayerNorm is fused into its input projection"
    l = 0
    att_tile = _pick(L, ATT_TILE)
    row_tile = _pick(L, ROW_TILE)
    wi = _inproj_weights(w_in[l])
    h, u, qt, kvc, ks, kw, vst, vwt, gt = _ln_inproj(x, ln_emb_g, ln_emb_b, wi, row_tile)

    lb_re, lb_im, bb_re, bb_im = _zoh_prep(ssm_a_re[l], ssm_a_im[l], ssm_log_dt[l], ssm_b_re[l], ssm_b_im[l])
    ws = _s5_weights(bb_re, bb_im, ssm_c_re[l], ssm_c_im[l], ssm_d[l], ssm_w_glu[l], ssm_b_glu[l], ssm_w_out[l])
    y_ssm = _s5(u, lb_re, lb_im, ws, _pick(L, 32))

    kc, vct = _compress(kvc, cmp_pos[l].astype(F32).reshape(2, CMP_BLOCK // CMP_QUAD, CMP_QUAD * HEAD_DIM),
                        cmp_w1[l].astype(BF16),
                        cmp_b1[l].astype(F32).reshape(2, 1, CMP_HIDDEN),
                        cmp_w2[l, 0].astype(BF16), cmp_w2[l, 1].T.astype(BF16))
    o_t = _nsa_attn(qt, kc, vct, ks, vst, kw, vwt, gt, att_tile)

    h = _mix_ln1(h, y_ssm, o_t, wi["wgm"], nsa_w_out[l].astype(BF16), w_out[l].astype(BF16),
                 ln1_g[l], ln1_b[l], _pick(L, MIX_ROW_TILE))
    mk, mv = _mem_kv(mem, xattn_w_kv[l].astype(BF16))
    h = _xattn_ln2(h, mk, mv, xattn_w_q[l].astype(BF16), xattn_w_o[l].astype(BF16), ln2_g[l], ln2_b[l], row_tile)
    h = _ffn_ln3(h.reshape(bsz * L, d), ffn_w_in[l].astype(BF16), ffn_w_out[l].astype(BF16),
                 ln3_g[l], ln3_b[l], row_tile)
    return h.reshape(bsz, L, d)
```

```python
import functools
import math

import jax
import jax.numpy as jnp
from jax import lax
from jax.experimental import pallas as pl
from jax.experimental.pallas import tpu as pltpu

F32 = jnp.float32
BF16 = jnp.bfloat16

D_MODEL = 1024
SSM_WIDTH = 512
SSM_GROUP = 16
SSM_GROUPS = SSM_WIDTH // SSM_GROUP
SSM_STATE = 64
SSM_STATES = SSM_GROUPS * SSM_STATE
SSM_EIG_CLIP = -1e-4
N_HEADS = 16
N_KV_HEADS = 4
HEAD_DIM = 64
GQA_GROUP = N_HEADS // N_KV_HEADS
KV_DIM = N_KV_HEADS * HEAD_DIM
CMP_BLOCK = 32
CMP_STRIDE = 16
CMP_HIDDEN = 256
CMP_QUAD = 4
SLC_BLOCK = 64
SLC_TOP_N = 8
WINDOW = 512
SEL_BIG = 1e9
X_HEADS = 4
X_HEAD_DIM = D_MODEL // X_HEADS
D_FF = 2816
DEEPNORM_ALPHA = 2.0 ** 0.25
LN_EPS = 1e-5
NEG_BIG = -1e30
GATE_ROWS = 16
LOG2E = 1.4426950408889634
ATT_TILE = 256
ROW_TILE = 512
ROW_STREAMS = 2
MIX_ROW_TILE = 512
KEY_LANES = 128
ALIBI_LANE = HEAD_DIM
ONEHOT_LANE = HEAD_DIM + 8
VT_ROWS = 80
SLC_LEAD_CHUNKS = 3
HEADS_IN_FLIGHT = 2
HEAD_PHASE_LAG = 2
CHUNK_ROWS = 16

V7X_VMEM_LIMIT_BYTES = 56 * 1024 * 1024
SSM_LANE_SLAB = 128
SSM_SLABS = SSM_WIDTH // SSM_LANE_SLAB
SSM_STATE_SLAB = SSM_STATES // SSM_SLABS


def _cparams(*sem):
    return pltpu.CompilerParams(dimension_semantics=sem, vmem_limit_bytes=V7X_VMEM_LIMIT_BYTES)


def _full(shape):
    zeros = (0,) * len(shape)
    return pl.BlockSpec(shape, lambda *_: zeros, pipeline_mode=pl.Buffered(1))


def _layer_norm(x, g, b):
    mu = jnp.mean(x, axis=-1, keepdims=True)
    xc = x - mu
    var = jnp.mean(xc * xc, axis=-1, keepdims=True)
    return xc * lax.rsqrt(var + LN_EPS) * g + b


def _gelu_tanh(x):
    return 0.5 * x * (1.0 + jnp.tanh(math.sqrt(2.0 / math.pi) * (x + 0.044715 * (x * x * x))))


def _sigmoid(x):
    return 1.0 / (1.0 + jnp.exp(-x))


def _dot(a, b):
    return jnp.dot(a, b, preferred_element_type=F32)


def _dot_nt(a, b):
    return lax.dot_general(a, b, (((1,), (1,)), ((), ())), preferred_element_type=F32)


def _zoh_kernel(a_re, a_im, log_dt, b_re, b_im, lb_re_o, lb_im_o, bb_re_o, bb_im_o):
    lam_re = jnp.minimum(a_re[...], SSM_EIG_CLIP)
    lam_im = a_im[...]
    dt = jnp.exp(log_dt[...])
    mag = jnp.exp(lam_re * dt)
    lb_re = mag * jnp.cos(lam_im * dt)
    lb_im = mag * jnp.sin(lam_im * dt)
    den = lam_re * lam_re + lam_im * lam_im
    nr = lb_re - 1.0
    f_re = (nr * lam_re + lb_im * lam_im) / den
    f_im = (lb_im * lam_re - nr * lam_im) / den
    br = b_re[...]
    bi = b_im[...]
    lb_re_o[...] = lb_re
    lb_im_o[...] = lb_im
    bb_re_o[...] = f_re * br - f_im * bi
    bb_im_o[...] = f_re * bi + f_im * br


def _zoh_prep(a_re, a_im, log_dt, b_re, b_im):
    gn = SSM_STATES
    col = lambda v: v.astype(F32).reshape(gn, 1)
    dt_col = jnp.broadcast_to(log_dt.astype(F32)[:, None], (SSM_GROUPS, SSM_STATE)).reshape(gn, 1)
    outs = pl.pallas_call(
        _zoh_kernel,
        out_shape=[jax.ShapeDtypeStruct((gn, 1), F32)] * 2 + [jax.ShapeDtypeStruct((gn, SSM_GROUP), F32)] * 2,
        name="zoh_prep",
    )(col(a_re), col(a_im), dt_col, b_re.astype(F32).reshape(gn, SSM_GROUP), b_im.astype(F32).reshape(gn, SSM_GROUP))
    lb_re, lb_im, bb_re, bb_im = outs
    shape_b = (SSM_GROUPS, SSM_STATE, SSM_GROUP)
    return lb_re.reshape(1, gn), lb_im.reshape(1, gn), bb_re.reshape(shape_b), bb_im.reshape(shape_b)


def _inproj_kernel(x_ref, g_ref, b_ref, wu_ref, wqt_ref, wk_ref, wvst_ref, wvwt_ref, wgt_ref,
                   h_o, u_o, qt_o, kvc_o, ks_o, kw_o, vst_o, vwt_o, gt_o):
    tl = x_ref.shape[1]
    h = _layer_norm(x_ref[0], g_ref[...], b_ref[...])
    h_o[0] = h
    hb = h.astype(BF16)
    u_o[0] = _dot(hb, wu_ref[...])
    qt_o[0] = (_dot_nt(wqt_ref[...], hb) * (HEAD_DIM ** -0.5 * LOG2E)).astype(BF16)

    k_all = _dot(hb, wk_ref[...])

    def head_tile(j):
        tile = k_all[:, (j // 2) * KEY_LANES:(j // 2 + 1) * KEY_LANES]
        return pltpu.roll(tile, HEAD_DIM, 1) if j % 2 else tile

    for j in range(N_KV_HEADS):
        kvc_o[0, j] = k_all[:, j * KEY_LANES:(j + 1) * KEY_LANES]

    lane = lax.broadcasted_iota(jnp.int32, (tl, KEY_LANES), 1)
    pos = pl.program_id(1) * tl + lax.broadcasted_iota(jnp.int32, (tl, KEY_LANES), 0)
    alibi = jnp.where((lane >= ALIBI_LANE) & (lane < ALIBI_LANE + 3), (pos % ATT_TILE).astype(F32), 0.0)
    slc_feat = alibi + jnp.where((lane >= ONEHOT_LANE) & (lane - ONEHOT_LANE == pos // SLC_BLOCK), 1.0, 0.0)
    is_key = lane < HEAD_DIM
    for j in range(N_KV_HEADS):
        ks_o[0, j] = jnp.where(is_key, head_tile(2 * N_KV_HEADS + j), slc_feat).astype(BF16)
        kw_o[0, j] = jnp.where(is_key, head_tile(3 * N_KV_HEADS + j), alibi).astype(BF16)
    row = lax.broadcasted_iota(jnp.int32, (N_KV_HEADS * VT_ROWS, tl), 0)
    ones_row = jnp.where(row % VT_ROWS == HEAD_DIM, 1.0, 0.0)
    vst_o[0] = (_dot_nt(wvst_ref[...], hb) + ones_row).astype(BF16)
    vwt_o[0] = (_dot_nt(wvwt_ref[...], hb) + ones_row).astype(BF16)
    gt_o[0] = _sigmoid(_dot_nt(wgt_ref[...], hb))


def _ln_inproj(x, ln_g, ln_b, w, tl):
    bsz, L, d = x.shape
    n_gate = N_KV_HEADS * GATE_ROWS
    grid = (bsz, L // tl)
    in_specs = [
        pl.BlockSpec((1, tl, d), lambda b, i: (b, i, 0)),
        _full((1, d)), _full((1, d)),
        _full((d, SSM_WIDTH)),
        _full((D_MODEL, d)),
        _full((d, 4 * KV_DIM)),
        _full((N_KV_HEADS * VT_ROWS, d)),
        _full((N_KV_HEADS * VT_ROWS, d)),
        _full((n_gate, d)),
    ]
    assert tl % ATT_TILE == 0 or ATT_TILE % tl == 0
    assert ONEHOT_LANE + L // SLC_BLOCK <= KEY_LANES
    out_shape = [
        jax.ShapeDtypeStruct((bsz, L, d), F32),
        jax.ShapeDtypeStruct((bsz, L, SSM_WIDTH), F32),
        jax.ShapeDtypeStruct((bsz, D_MODEL, L), BF16),
        jax.ShapeDtypeStruct((bsz, N_KV_HEADS, L, 2 * HEAD_DIM), F32),
        jax.ShapeDtypeStruct((bsz, N_KV_HEADS, L, KEY_LANES), BF16),
        jax.ShapeDtypeStruct((bsz, N_KV_HEADS, L, KEY_LANES), BF16),
        jax.ShapeDtypeStruct((bsz, N_KV_HEADS * VT_ROWS, L), BF16),
        jax.ShapeDtypeStruct((bsz, N_KV_HEADS * VT_ROWS, L), BF16),
        jax.ShapeDtypeStruct((bsz, n_gate, L), F32),
    ]
    out_specs = [
        pl.BlockSpec((1, tl, d), lambda b, i: (b, i, 0)),
        pl.BlockSpec((1, tl, SSM_WIDTH), lambda b, i: (b, i, 0)),
        pl.BlockSpec((1, D_MODEL, tl), lambda b, i: (b, 0, i)),
        pl.BlockSpec((1, N_KV_HEADS, tl, 2 * HEAD_DIM), lambda b, i: (b, 0, i, 0)),
        pl.BlockSpec((1, N_KV_HEADS, tl, KEY_LANES), lambda b, i: (b, 0, i, 0)),
        pl.BlockSpec((1, N_KV_HEADS, tl, KEY_LANES), lambda b, i: (b, 0, i, 0)),
        pl.BlockSpec((1, N_KV_HEADS * VT_ROWS, tl), lambda b, i: (b, 0, i)),
        pl.BlockSpec((1, N_KV_HEADS * VT_ROWS, tl), lambda b, i: (b, 0, i)),
        pl.BlockSpec((1, n_gate, tl), lambda b, i: (b, 0, i)),
    ]
    return pl.pallas_call(
        _inproj_kernel, grid=grid, in_specs=in_specs, out_specs=out_specs, out_shape=out_shape,
        compiler_params=_cparams("parallel", "parallel"), name="ln_inproj",
    )(x, ln_g.reshape(1, d), ln_b.reshape(1, d), w["wu"], w["wqt"], w["wk"],
      w["wvst"], w["wvwt"], w["wgt"])


def _s5_kernel(u_ref, lre_ref, lim_ref, wbre_ref, wbim_ref, cre_ref, cim_ref, d_ref, wglu_ref, bglu_ref, wo_ref,
               y_o, sre, sim, hre, him, *, bsz, steps, pitch):
    @pl.when(pl.program_id(0) == 0)
    def _():
        sre[...] = jnp.zeros_like(sre)
        sim[...] = jnp.zeros_like(sim)

    lanes = SSM_LANE_SLAB
    per_slab = SSM_STATE_SLAB // lanes
    u = u_ref[...].reshape(bsz * steps, SSM_WIDTH)
    ub = u.astype(BF16)

    def project_in(j):
        uj = ub[:, j * lanes:(j + 1) * lanes]
        for w_ref, h_ref in ((wbre_ref, hre), (wbim_ref, him)):
            r = _dot(uj, w_ref[j])
            for b in range(bsz):
                for k in range(per_slab):
                    h_ref[j * per_slab + k, pl.ds(b, steps, stride=pitch), :] = (
                        r[b * steps:(b + 1) * steps, k * lanes:(k + 1) * lanes])

    def recur(j):
        for s in range(j * per_slab, (j + 1) * per_slab):
            cols = slice(s * lanes, (s + 1) * lanes)
            lr = jnp.broadcast_to(lre_ref[:, cols], (bsz, lanes))
            li = jnp.broadcast_to(lim_ref[:, cols], (bsz, lanes))
            pr, pi = sre[:, cols], sim[:, cols]
            for t in range(steps):
                rows = pl.ds(t * pitch, bsz)
                pr, pi = (lr * pr - li * pi + hre[s, rows, :], lr * pi + li * pr + him[s, rows, :])
                hre[s, rows, :] = pr
                him[s, rows, :] = pi
            sre[:, cols] = pr
            sim[:, cols] = pi

    def states(h_ref, j):
        return jnp.concatenate(
            [jnp.concatenate([h_ref[j * per_slab + k, pl.ds(b, steps, stride=pitch), :] for k in range(per_slab)],
                             axis=-1) for b in range(bsz)], axis=0)

    ys = [None] * SSM_SLABS

    def project_out(j):
        ys[j] = _dot(states(hre, j).astype(BF16), cre_ref[j]) - _dot(states(him, j).astype(BF16), cim_ref[j])

    for j in range(SSM_SLABS + 2):
        if j < SSM_SLABS:
            project_in(j)
        if 0 <= j - 1 < SSM_SLABS:
            recur(j - 1)
        if 0 <= j - 2 < SSM_SLABS:
            project_out(j - 2)
    y = jnp.concatenate(ys, axis=-1) + d_ref[...] * u
    g = _gelu_tanh(y)
    y2 = g * _sigmoid(_dot(g.astype(BF16), wglu_ref[...]) + bglu_ref[...])
    y_o[...] = _dot(y2.astype(BF16), wo_ref[...]).reshape(bsz, steps, D_MODEL)


def _s5(u, lb_re, lb_im, w, steps):
    bsz, L, _ = u.shape
    pitch = -(-bsz // 4) * 4
    pitch += 4 if (pitch // 4) % 2 == 0 else 0
    grid = (L // steps,)
    kern = functools.partial(_s5_kernel, bsz=bsz, steps=steps, pitch=pitch)
    n_slabs = SSM_STATES // SSM_LANE_SLAB
    in_specs = [
        pl.BlockSpec((bsz, steps, SSM_WIDTH), lambda c: (0, c, 0)),
        _full((1, SSM_STATES)), _full((1, SSM_STATES)),
        _full((SSM_SLABS, SSM_LANE_SLAB, SSM_STATE_SLAB)), _full((SSM_SLABS, SSM_LANE_SLAB, SSM_STATE_SLAB)),
        _full((SSM_SLABS, SSM_STATE_SLAB, SSM_LANE_SLAB)), _full((SSM_SLABS, SSM_STATE_SLAB, SSM_LANE_SLAB)),
        _full((1, SSM_WIDTH)),
        _full((SSM_WIDTH, SSM_WIDTH)), _full((1, SSM_WIDTH)),
        _full((SSM_WIDTH, D_MODEL)),
    ]
    return pl.pallas_call(
        kern, grid=grid, in_specs=in_specs,
        out_specs=pl.BlockSpec((bsz, steps, D_MODEL), lambda c: (0, c, 0)),
        out_shape=jax.ShapeDtypeStruct((bsz, L, D_MODEL), F32),
        scratch_shapes=[pltpu.VMEM((bsz, SSM_STATES), F32), pltpu.VMEM((bsz, SSM_STATES), F32),
                        pltpu.VMEM((n_slabs, steps * pitch, SSM_LANE_SLAB), F32),
                        pltpu.VMEM((n_slabs, steps * pitch, SSM_LANE_SLAB), F32)],
        compiler_params=_cparams("arbitrary"), name="s5",
    )(u, lb_re, lb_im, w["wbre"], w["wbim"], w["cre"], w["cim"], w["d"], w["wglu"], w["bglu"], w["wo"])


def _compress_kernel(kv_ref, pos_ref, w1_ref, b1_ref, w2k_ref, w2vt_ref, kc_o, vct_o, *, n_chunks):
    half = CMP_BLOCK // 2
    quad = CMP_QUAD
    low_half = lax.broadcasted_iota(jnp.int32, (n_chunks, 2 * HEAD_DIM), 1) < HEAD_DIM
    for tile in range(2 * N_KV_HEADS // 2):
        z = tile // (N_KV_HEADS // 2)
        heads = (2 * (tile % (N_KV_HEADS // 2)), 2 * (tile % (N_KV_HEADS // 2)) + 1)
        first = [jnp.zeros((n_chunks, CMP_HIDDEN), F32) for _ in heads]
        second = [jnp.zeros((n_chunks, CMP_HIDDEN), F32) for _ in heads]
        for q in range(half // quad):
            rows = [kv_ref[0, tile, pl.ds(q * quad + r, n_chunks, stride=CMP_STRIDE), :] for r in range(quad)]
            turned = [pltpu.roll(x, HEAD_DIM, 1) for x in rows]
            for side in range(2):
                pairs = [jnp.where(low_half, rows[r], turned[r + 1]) if side == 0 else
                         jnp.where(low_half, turned[r], rows[r + 1]) for r in range(0, quad, 2)]
                x4 = jnp.concatenate(pairs, axis=-1)
                for part, acc in ((0, first), (1, second)):
                    qq = part * (half // quad) + q
                    w = w1_ref[z, qq * quad * HEAD_DIM:(qq + 1) * quad * HEAD_DIM, :]
                    acc[side] += _dot((x4 + pos_ref[z, qq:qq + 1, :]).astype(BF16), w)
        for side, hh in enumerate(heads):
            pre = first[side] + pltpu.roll(second[side], n_chunks - 1, 0) + b1_ref[z]
            hid = _gelu_tanh(pre).astype(BF16)
            if z == 0:
                kc_o[0, hh] = _dot(hid, w2k_ref[...]).astype(BF16)
            else:
                vct_o[0, hh * HEAD_DIM:(hh + 1) * HEAD_DIM, :] = _dot_nt(w2vt_ref[...], hid).astype(BF16)


def _compress(kvc, pos, w1, b1, w2k, w2vt):
    bsz, _, L, _ = kvc.shape
    n_chunks = L // CMP_STRIDE
    kern = functools.partial(_compress_kernel, n_chunks=n_chunks)
    return pl.pallas_call(
        kern, grid=(bsz,),
        in_specs=[
            pl.BlockSpec((1, N_KV_HEADS, L, 2 * HEAD_DIM), lambda b: (b, 0, 0, 0)),
            _full((2, CMP_BLOCK // CMP_QUAD, CMP_QUAD * HEAD_DIM)),
            _full((2, CMP_BLOCK * HEAD_DIM, CMP_HIDDEN)),
            _full((2, 1, CMP_HIDDEN)),
            _full((CMP_HIDDEN, HEAD_DIM)),
            _full((HEAD_DIM, CMP_HIDDEN)),
        ],
        out_specs=[
            pl.BlockSpec((1, N_KV_HEADS, n_chunks, HEAD_DIM), lambda b: (b, 0, 0, 0)),
            pl.BlockSpec((1, KV_DIM, n_chunks), lambda b: (b, 0, 0)),
        ],
        out_shape=[
            jax.ShapeDtypeStruct((bsz, N_KV_HEADS, n_chunks, HEAD_DIM), BF16),
            jax.ShapeDtypeStruct((bsz, KV_DIM, n_chunks), BF16),
        ],
        compiler_params=_cparams("parallel"), name="compress",
    )(kvc, pos, w1, b1, w2k, w2vt)


def _nsa_kernel(*refs, tq, n_cmp_pad, n_slc):
    def head_pair(pair, carry):
        heads = [_nsa_head(HEADS_IN_FLIGHT * pair + slot, slot, *refs, tq=tq, n_cmp_pad=n_cmp_pad, n_slc=n_slc)
                 for slot in range(HEADS_IN_FLIGHT)]
        tails, live, step = {}, set(range(HEADS_IN_FLIGHT)), 0
        while live:
            for slot in sorted(live):
                if step >= slot * HEAD_PHASE_LAG:
                    try:
                        out = next(heads[slot])
                        if out is not None:
                            tails[slot] = out
                    except StopIteration:
                        live.discard(slot)
            step += 1
        for slot in range(HEADS_IN_FLIGHT):
            tails[slot]()
        return carry

    lax.fori_loop(0, N_KV_HEADS // HEADS_IN_FLIGHT, head_pair, 0)


def _nsa_head(hkv, par, qt_ref, kc_ref, vct_ref, ks_ref, vst_ref, kw_ref, vwt_ref, gt_ref, ov_ref, causal_ref,
              far_ref, chunk_ref, o_ref, qa_ref, qw_ref, m_ref, acc_ref, s_ref, mx_ref, al_ref, sc_ref, pc_ref,
              need_ref, *, tq, n_cmp_pad, n_slc):
    i = pl.program_id(1)
    slots_per_head = s_ref.shape[0] // HEADS_IN_FLIGHT
    t0 = i * tq
    tk = tq
    n_sel = min(SLC_TOP_N, n_slc)
    slopes = [jnp.exp2(jnp.full((1, tq), -0.5, F32) * jnp.asarray(hkv * GQA_GROUP + g + 1, F32)) * LOG2E
              for g in range(GQA_GROUP)]
    head_rows = lambda g: pl.ds(pl.multiple_of((hkv * GQA_GROUP + g) * HEAD_DIM, HEAD_DIM), HEAD_DIM)
    vt_rows = pl.ds(pl.multiple_of(hkv * VT_ROWS, 16), VT_ROWS)
    q_heads = [qt_ref[0, head_rows(g), :] for g in range(GQA_GROUP)]
    gate = lambda g, z: gt_ref[0, pl.ds(hkv * GATE_ROWS + 3 * g + z, 1), :]

    t_lane = t0 + lax.broadcasted_iota(jnp.int32, (1, tq), 1)
    row8 = lax.broadcasted_iota(jnp.int32, (ONEHOT_LANE - ALIBI_LANE, tq), 0)

    def augmented(g, tail_rows):
        hi = slopes[g].astype(BF16).astype(F32)
        mid = (slopes[g] - hi).astype(BF16).astype(F32)
        lo = slopes[g] - hi - mid
        parts = jnp.where(row8 == 0, hi, jnp.where(row8 == 1, mid, jnp.where(row8 == 2, lo, 0.0)))
        return jnp.concatenate([q_heads[g].astype(F32), parts, tail_rows], axis=0).astype(BF16)

    def attend(state, q_ref, items, first, slot0):
        starts = [pl.multiple_of((c if valid is None else jnp.maximum(c, 0)) * tk, tk) for _, _, c, _, valid in items]
        slot0 = slot0 + par * slots_per_head

        def scores(g):
            for n, (k_ref, _, _, bias_ref, _) in enumerate(items):
                s = _dot(k_ref[0, hkv, pl.ds(starts[n], tk), :], q_ref[par, g])
                if bias_ref is not None:
                    s = s + bias_ref[...]
                s_ref[slot0 + n * GQA_GROUP + g] = s
                mx_ref[slot0 + n * GQA_GROUP + g] = jnp.max(s, axis=0, keepdims=True)

        def fold(g):
            shifts = []
            for _, _, c, _, valid in items:
                shift = slopes[g] * jnp.asarray((c - i) * tk, F32)
                shifts.append(shift if valid is None else jnp.where(valid, shift, NEG_BIG))
            m_new = functools.reduce(jnp.maximum, [mx_ref[slot0 + n * GQA_GROUP + g] + shifts[n]
                                                   for n in range(len(items))])
            if not first:
                m_new = jnp.maximum(m_ref[par, state, g], m_new)
                al_ref[par, g] = jnp.exp2(m_ref[par, state, g] - m_new)
            pv = None
            for n, (_, vt_ref, _, _, _) in enumerate(items):
                slot = slot0 + n * GQA_GROUP + g
                p = jnp.exp2(s_ref[slot] - (m_new - shifts[n])).astype(BF16)
                part = _dot(vt_ref[0, vt_rows, pl.ds(starts[n], tk)], p)
                pv = part if pv is None else pv + part
            acc_ref[par, state, g] = pv if first else al_ref[par, g] * acc_ref[par, state, g] + pv
            m_ref[par, state, g] = m_new

        scores(0)
        for g in range(GQA_GROUP):
            if g + 1 < GQA_GROUP:
                scores(g + 1)
            fold(g)
            yield

    def emit(state, z, accumulate):
        for g in range(GQA_GROUP):
            scale = gate(g, z) / acc_ref[par, state, g, HEAD_DIM:HEAD_DIM + 1, :]
            out = acc_ref[par, state, g, :HEAD_DIM, :] * scale
            o_ref[0, head_rows(g), :] = o_ref[0, head_rows(g), :] + out if accumulate else out

    n_idx = lax.broadcasted_iota(jnp.int32, (n_cmp_pad, tq), 0)
    dist_c = (t0 + lax.broadcasted_iota(jnp.int32, (n_cmp_pad, tq), 1)) - (n_idx * CMP_STRIDE + (CMP_BLOCK - 1))
    mask_c = dist_c >= 0
    dist_cf = dist_c.astype(F32)
    kc = kc_ref[0, hkv]
    vct = vct_ref[0, pl.ds(pl.multiple_of(hkv * HEAD_DIM, HEAD_DIM), HEAD_DIM), :]
    for g in range(GQA_GROUP):
        s = _dot(kc, q_heads[g]) - slopes[g] * dist_cf
        sc_ref[par, g] = jnp.where(mask_c, s, NEG_BIG)
    yield
    p_sum = jnp.zeros((n_cmp_pad, tq), F32)
    for g in range(GQA_GROUP):
        s = sc_ref[par, g]
        m = jnp.max(s, axis=0, keepdims=True)
        e = jnp.exp2(s - m)
        norm = 1.0 / jnp.maximum(jnp.sum(e, axis=0, keepdims=True), 1e-30)
        p = e * jnp.where(m > 0.5 * NEG_BIG, norm, 0.0)
        pc_ref[par, g] = p.astype(BF16)
        p_sum = p_sum + p
    for g in range(GQA_GROUP):
        o_ref[0, head_rows(g), :] = gate(g, 0) * _dot(vct, pc_ref[par, g])

    imp = lax.dot_general(ov_ref[...], p_sum, (((1,), (0,)), ((), ())), precision=lax.Precision.HIGHEST,
                          preferred_element_type=F32)
    yield
    blk = lax.broadcasted_iota(jnp.int32, (n_slc, tq), 0)
    cur = t_lane // SLC_BLOCK
    forced = (blk == 0) | (blk == cur) | (blk == cur - 1)
    future = blk * SLC_BLOCK > t_lane
    imp = jnp.where(forced, SEL_BIG, jnp.where(future, -SEL_BIG, imp))
    sub8 = 8
    tiles = [imp[r:r + sub8] for r in range(0, n_slc, sub8)]
    ranks = [jnp.zeros((sub8, tq), jnp.int32) for _ in tiles]
    for j in range(n_slc):
        row = imp[j:j + 1, :]
        for r, tile in enumerate(tiles):
            if r * sub8 > j:
                ahead = row >= tile
            elif r * sub8 + sub8 - 1 < j:
                ahead = row > tile
            else:
                later = r * sub8 + lax.broadcasted_iota(jnp.int32, (sub8, tq), 0) > j
                ahead = (row > tile) | ((row == tile) & later)
            ranks[r] = ranks[r] + ahead.astype(jnp.int32)
    rank = jnp.concatenate(ranks, axis=0)
    sel_bias = jnp.where(rank < n_sel, 0.0, NEG_BIG)

    qa_rows = jnp.concatenate([sel_bias, jnp.zeros((KEY_LANES - ONEHOT_LANE - n_slc, tq), F32)], axis=0)
    for g in range(GQA_GROUP):
        qa_ref[par, g] = augmented(g, qa_rows)
    yield

    zero_rows = jnp.zeros((KEY_LANES - ONEHOT_LANE, tq), F32)
    for g in range(GQA_GROUP):
        qw_ref[par, g] = augmented(g, zero_rows)
    n_back = WINDOW // tk
    n_win_items = n_back + 1
    win_items = [(kw_ref, vwt_ref, i, causal_ref, None)]
    for back in range(1, n_back + 1):
        win_items.append((kw_ref, vwt_ref, i - back, far_ref if back == n_back else None, i >= back))
    yield from attend(0, qw_ref, win_items, True, 0)
    emit(0, 2, True)
    yield

    slc_items = [(ks_ref, vst_ref, i, causal_ref, None)]
    for back in range(1, SLC_LEAD_CHUNKS):
        slc_items.append((ks_ref, vst_ref, i - back, None, i >= back))
    slc_items.append((ks_ref, vst_ref, 0, None, i >= SLC_LEAD_CHUNKS))
    yield from attend(1, qa_ref, slc_items, True, n_win_items * GQA_GROUP)

    picked = (rank < n_sel).astype(BF16)
    per_chunk = jnp.sum(_dot(chunk_ref[...], picked), axis=1, keepdims=True)
    need0 = par * CHUNK_ROWS
    for c in range(n_slc * SLC_BLOCK // tk):
        need_ref[need0 + c] = (per_chunk[c, 0] > 0.0).astype(jnp.int32)
    last = i - SLC_LEAD_CHUNKS

    def slc_step(pair, carry):
        c0 = 1 + 2 * pair
        c1 = jnp.minimum(c0 + 1, last)
        use0 = need_ref[need0 + c0] > 0
        use1 = (c0 + 1 <= last) & (need_ref[need0 + c1] > 0)

        @pl.when(use0 | use1)
        def _():
            for _ in attend(1, qa_ref, [(ks_ref, vst_ref, c0, None, use0), (ks_ref, vst_ref, c1, None, use1)], False, 0):
                pass
        return carry

    def tail():
        lax.fori_loop(0, jnp.maximum(last + 1, 0) // 2, slc_step, 0)
        emit(1, 1, True)

    yield tail


def _nsa_attn(qt, kc, vct, ks, vst, kw, vwt, gt, tq):
    bsz, _, L = qt.shape
    n_cmp_pad = kc.shape[2]
    n_slc = L // SLC_BLOCK
    n = jnp.arange(n_cmp_pad)[None, :]
    j = jnp.arange(n_slc)[:, None]
    overlap_t = ((n * CMP_STRIDE < (j + 1) * SLC_BLOCK) & (n * CMP_STRIDE + CMP_BLOCK - 1 >= j * SLC_BLOCK)).astype(F32)
    assert tq == ATT_TILE and WINDOW % tq == 0 and n_slc % 8 == 0
    n_slots = (WINDOW // tq + 1 + SLC_LEAD_CHUNKS + 1) * GQA_GROUP
    sub = jnp.arange(tq)[:, None]
    lane = jnp.arange(tq)[None, :]
    causal_bias = jnp.where(sub <= lane, 0.0, NEG_BIG).astype(F32)
    far_bias = jnp.where(sub > lane, 0.0, NEG_BIG).astype(F32)
    assert L // tq <= CHUNK_ROWS
    chunk_of_block = (jnp.arange(CHUNK_ROWS)[:, None] == jnp.arange(n_slc)[None, :] * SLC_BLOCK // tq).astype(BF16)
    kern = functools.partial(_nsa_kernel, tq=tq, n_cmp_pad=n_cmp_pad, n_slc=n_slc)
    return pl.pallas_call(
        kern, grid=(bsz, L // tq),
        in_specs=[
            pl.BlockSpec((1, D_MODEL, tq), lambda b, i: (b, 0, i)),
            pl.BlockSpec((1, N_KV_HEADS, n_cmp_pad, HEAD_DIM), lambda b, i: (b, 0, 0, 0)),
            pl.BlockSpec((1, KV_DIM, n_cmp_pad), lambda b, i: (b, 0, 0)),
            pl.BlockSpec((1, N_KV_HEADS, L, KEY_LANES), lambda b, i: (b, 0, 0, 0)),
            pl.BlockSpec((1, N_KV_HEADS * VT_ROWS, L), lambda b, i: (b, 0, 0)),
            pl.BlockSpec((1, N_KV_HEADS, L, KEY_LANES), lambda b, i: (b, 0, 0, 0)),
            pl.BlockSpec((1, N_KV_HEADS * VT_ROWS, L), lambda b, i: (b, 0, 0)),
            pl.BlockSpec((1, N_KV_HEADS * GATE_ROWS, tq), lambda b, i: (b, 0, i)),
            _full((n_slc, n_cmp_pad)),
            _full((tq, tq)), _full((tq, tq)),
            _full((CHUNK_ROWS, n_slc)),
        ],
        out_specs=pl.BlockSpec((1, D_MODEL, tq), lambda b, i: (b, 0, i)),
        out_shape=jax.ShapeDtypeStruct((bsz, D_MODEL, L), F32),
        scratch_shapes=[pltpu.VMEM((HEADS_IN_FLIGHT, GQA_GROUP, KEY_LANES, tq), BF16),
                        pltpu.VMEM((HEADS_IN_FLIGHT, GQA_GROUP, KEY_LANES, tq), BF16),
                        pltpu.VMEM((HEADS_IN_FLIGHT, 2, GQA_GROUP, 1, tq), F32),
                        pltpu.VMEM((HEADS_IN_FLIGHT, 2, GQA_GROUP, VT_ROWS, tq), F32),
                        pltpu.VMEM((HEADS_IN_FLIGHT * n_slots, tq, tq), F32),
                        pltpu.VMEM((HEADS_IN_FLIGHT * n_slots, 1, tq), F32),
                        pltpu.VMEM((HEADS_IN_FLIGHT, GQA_GROUP, 1, tq), F32),
                        pltpu.VMEM((HEADS_IN_FLIGHT, GQA_GROUP, n_cmp_pad, tq), F32),
                        pltpu.VMEM((HEADS_IN_FLIGHT, GQA_GROUP, n_cmp_pad, tq), BF16),
                        pltpu.SMEM((HEADS_IN_FLIGHT * CHUNK_ROWS,), jnp.int32)],
        compiler_params=_cparams("parallel", "parallel"), name="nsa_attn",
    )(qt, kc, vct, ks, vst, kw, vwt, gt, overlap_t, causal_bias, far_bias, chunk_of_block)


def _mix_kernel(h_ref, ys_ref, ot_ref, wgm_ref, wno_ref, wout_ref, g_ref, b_ref, h1_o):
    stream = h_ref.shape[1] // ROW_STREAMS
    for r in range(ROW_STREAMS):
        rows = slice(r * stream, (r + 1) * stream)
        h = h_ref[0, rows, :]
        o = ot_ref[0, :, rows].T.astype(BF16)
        y_nsa = _dot(o, wno_ref[...])
        gm = _sigmoid(_dot(h.astype(BF16), wgm_ref[...]))
        mix = gm[:, :D_MODEL] * ys_ref[0, rows, :] + gm[:, D_MODEL:] * y_nsa
        mixed = _dot(mix.astype(BF16), wout_ref[...])
        h1_o[0, rows, :] = _layer_norm(DEEPNORM_ALPHA * h + mixed, g_ref[...], b_ref[...])


def _mix_ln1(h, y_ssm, o_t, wgm, wno, wout, g, b, tm):
    bsz, L, d = h.shape
    return pl.pallas_call(
        _mix_kernel, grid=(bsz, L // tm),
        in_specs=[
            pl.BlockSpec((1, tm, d), lambda b, i: (b, i, 0)),
            pl.BlockSpec((1, tm, d), lambda b, i: (b, i, 0)),
            pl.BlockSpec((1, d, tm), lambda b, i: (b, 0, i)),
            _full((d, 2 * d)), _full((d, d)), _full((d, d)), _full((1, d)), _full((1, d)),
        ],
        out_specs=pl.BlockSpec((1, tm, d), lambda b, i: (b, i, 0)),
        out_shape=jax.ShapeDtypeStruct((bsz, L, d), F32),
        compiler_params=_cparams("parallel", "parallel"), name="mix_ln1",
    )(h, y_ssm, o_t, wgm, wno, wout, g.reshape(1, d), b.reshape(1, d))


def _memkv_kernel(mem_ref, w_ref, k_o, v_o):
    kv = _dot(mem_ref[0].astype(BF16), w_ref[...])
    k_o[0] = kv[:, :D_MODEL].astype(BF16)
    v_o[0] = kv[:, D_MODEL:].astype(BF16)


def _mem_kv(mem, w_kv):
    bsz, m, d = mem.shape
    spec = pl.BlockSpec((1, m, d), lambda b: (b, 0, 0))
    return pl.pallas_call(
        _memkv_kernel, grid=(bsz,),
        in_specs=[spec, _full((d, 2 * d))],
        out_specs=[spec, spec],
        out_shape=[jax.ShapeDtypeStruct((bsz, m, d), BF16)] * 2,
        compiler_params=_cparams("parallel"), name="mem_kv",
    )(mem, w_kv)


def _xattn_kernel(h_ref, k_ref, v_ref, wq_ref, wo_ref, g_ref, b_ref, h2_o):
    h = h_ref[0]
    q = (_dot(h.astype(BF16), wq_ref[...]) * (X_HEAD_DIM ** -0.5 * LOG2E)).astype(BF16)
    cols = [slice(hd * X_HEAD_DIM, (hd + 1) * X_HEAD_DIM) for hd in range(X_HEADS)]
    scores = lambda hd: _dot_nt(q[:, cols[hd]], k_ref[0, :, cols[hd]])
    outs = []
    s_next = scores(0)
    for hd in range(X_HEADS):
        s = s_next
        if hd + 1 < X_HEADS:
            s_next = scores(hd + 1)
        p = jnp.exp2(s - jnp.max(s, axis=-1, keepdims=True))
        norm = 1.0 / jnp.sum(p, axis=-1, keepdims=True)
        outs.append(_dot(p.astype(BF16), v_ref[0, :, cols[hd]]) * norm)
    o = jnp.concatenate(outs, axis=-1).astype(BF16)
    h2_o[0] = _layer_norm(DEEPNORM_ALPHA * h + _dot(o, wo_ref[...]), g_ref[...], b_ref[...])


def _xattn_ln2(h, k, v, wq, wo, g, b, tm):
    bsz, L, d = h.shape
    m = k.shape[1]
    row = pl.BlockSpec((1, tm, d), lambda b, i: (b, i, 0))
    kv = pl.BlockSpec((1, m, d), lambda b, i: (b, 0, 0))
    return pl.pallas_call(
        _xattn_kernel, grid=(bsz, L // tm),
        in_specs=[row, kv, kv, _full((d, d)), _full((d, d)), _full((1, d)), _full((1, d))],
        out_specs=row,
        out_shape=jax.ShapeDtypeStruct((bsz, L, d), F32),
        compiler_params=_cparams("parallel", "parallel"), name="xattn_ln2",
    )(h, k, v, wq, wo, g.reshape(1, d), b.reshape(1, d))


def _ffn_kernel(h_ref, win_ref, wout_ref, g_ref, b_ref, o_ref):
    stream = h_ref.shape[0] // ROW_STREAMS
    for r in range(ROW_STREAMS):
        rows = slice(r * stream, (r + 1) * stream)
        h = h_ref[rows, :]
        gu = _dot(h.astype(BF16), win_ref[...])
        gate = gu[:, :D_FF]
        act = gate * _sigmoid(gate) * gu[:, D_FF:]
        o_ref[rows, :] = _layer_norm(DEEPNORM_ALPHA * h + _dot(act.astype(BF16), wout_ref[...]),
                                     g_ref[...], b_ref[...])


def _ffn_ln3(h, win, wout, g, b, tm):
    rows, d = h.shape
    row = pl.BlockSpec((tm, d), lambda i: (i, 0))
    return pl.pallas_call(
        _ffn_kernel, grid=(rows // tm,),
        in_specs=[row, _full((d, 2 * D_FF)), _full((D_FF, d)), _full((1, d)), _full((1, d))],
        out_specs=row,
        out_shape=jax.ShapeDtypeStruct((rows, d), F32),
        compiler_params=_cparams("parallel"), name="ffn_ln3",
    )(h, win, wout, g.reshape(1, d), b.reshape(1, d))


def _inproj_weights(w_in):
    d = D_MODEL
    c0 = SSM_WIDTH
    c1 = c0 + N_HEADS * HEAD_DIM
    c2 = c1 + 2 * KV_DIM
    c3 = c2 + 2 * KV_DIM
    c4 = c3 + 2 * KV_DIM
    c5 = c4 + 3 * N_HEADS
    values_t = lambda m: jnp.pad(m.T.reshape(N_KV_HEADS, HEAD_DIM, d),
                                 ((0, 0), (0, VT_ROWS - HEAD_DIM), (0, 0))).reshape(N_KV_HEADS * VT_ROWS, d)
    wg = w_in[:, c4:c5].T.reshape(N_KV_HEADS, 3 * GQA_GROUP, d)
    wg = jnp.pad(wg, ((0, 0), (0, GATE_ROWS - 3 * GQA_GROUP), (0, 0))).reshape(N_KV_HEADS * GATE_ROWS, d)
    return {
        "wu": w_in[:, :c0].astype(BF16),
        "wqt": w_in[:, c0:c1].T.astype(BF16),
        "wk": jnp.concatenate([w_in[:, c1:c2], w_in[:, c2:c2 + KV_DIM], w_in[:, c3:c3 + KV_DIM]], axis=1).astype(BF16),
        "wvst": values_t(w_in[:, c2 + KV_DIM:c3]).astype(BF16),
        "wvwt": values_t(w_in[:, c3 + KV_DIM:c4]).astype(BF16),
        "wgt": wg.astype(BF16),
        "wgm": w_in[:, c5:].astype(BF16),
    }


def _s5_weights(bb_re, bb_im, c_re, c_im, d_skip, w_glu, b_glu, w_o):
    per_slab = SSM_LANE_SLAB // SSM_GROUP
    eye = jnp.eye(per_slab, dtype=F32)

    def b_blocks(bb):
        bb = bb.reshape(SSM_SLABS, per_slab, SSM_STATE, SSM_GROUP)
        return jnp.einsum('jgnc,gh->jgchn', bb, eye).reshape(SSM_SLABS, SSM_LANE_SLAB, SSM_STATE_SLAB).astype(BF16)

    def c_blocks(c):
        c = c.astype(F32).reshape(SSM_SLABS, per_slab, SSM_GROUP, SSM_STATE)
        return jnp.einsum('jgcn,gh->jgnhc', c, eye).reshape(SSM_SLABS, SSM_STATE_SLAB, SSM_LANE_SLAB).astype(BF16)

    return {
        "wbre": b_blocks(bb_re), "wbim": b_blocks(bb_im), "cre": c_blocks(c_re), "cim": c_blocks(c_im),
        "d": d_skip.astype(F32).reshape(1, SSM_WIDTH), "wglu": w_glu.astype(BF16),
        "bglu": b_glu.astype(F32).reshape(1, SSM_WIDTH), "wo": w_o.astype(BF16),
    }


def _pick(total, want):
    return want if total % want == 0 else total


def kernel(x, mem, ln_emb_g, ln_emb_b, w_in, ssm_a_re, ssm_a_im, ssm_b_re, ssm_b_im, ssm_c_re, ssm_c_im, ssm_d,
           ssm_log_dt, ssm_w_glu, ssm_b_glu, ssm_w_out, cmp_pos, cmp_w1, cmp_b1, cmp_w2, nsa_w_out, w_out,
           ln1_g, ln1_b, xattn_w_q, xattn_w_kv, xattn_w_o, ln2_g, ln2_b, ffn_w_in, ffn_w_out, ln3_g, ln3_b):
    bsz, L, d = x.shape
    assert w_in.shape[0] == 1, "one layer: the trunk-entry LayerNorm is fused into its input projection"
    l = 0
    att_tile = _pick(L, ATT_TILE)
    row_tile = _pick(L, ROW_TILE)
    wi = _inproj_weights(w_in[l])
    h, u, qt, kvc, ks, kw, vst, vwt, gt = _ln_inproj(x, ln_emb_g, ln_emb_b, wi, row_tile)

    lb_re, lb_im, bb_re, bb_im = _zoh_prep(ssm_a_re[l], ssm_a_im[l], ssm_log_dt[l], ssm_b_re[l], ssm_b_im[l])
    ws = _s5_weights(bb_re, bb_im, ssm_c_re[l], ssm_c_im[l], ssm_d[l], ssm_w_glu[l], ssm_b_glu[l], ssm_w_out[l])
    y_ssm = _s5(u, lb_re, lb_im, ws, _pick(L, 64))

    kc, vct = _compress(kvc, cmp_pos[l].astype(F32).reshape(2, CMP_BLOCK // CMP_QUAD, CMP_QUAD * HEAD_DIM),
                        cmp_w1[l].astype(BF16),
                        cmp_b1[l].astype(F32).reshape(2, 1, CMP_HIDDEN),
                        cmp_w2[l, 0].astype(BF16), cmp_w2[l, 1].T.astype(BF16))
    o_t = _nsa_attn(qt, kc, vct, ks, vst, kw, vwt, gt, att_tile)

    h = _mix_ln1(h, y_ssm, o_t, wi["wgm"], nsa_w_out[l].astype(BF16), w_out[l].astype(BF16),
                 ln1_g[l], ln1_b[l], _pick(L, MIX_ROW_TILE))
    mk, mv = _mem_kv(mem, xattn_w_kv[l].astype(BF16))
    h = _xattn_ln2(h, mk, mv, xattn_w_q[l].astype(BF16), xattn_w_o[l].astype(BF16), ln2_g[l], ln2_b[l], row_tile)
    h = _ffn_ln3(h.reshape(bsz * L, d), ffn_w_in[l].astype(BF16), ffn_w_out[l].astype(BF16),
                 ln3_g[l], ln3_b[l], row_tile)
    return h.reshape(bsz, L, d)
```

```python
import functools
import math

import jax
import jax.numpy as jnp
from jax import lax
from jax.experimental import pallas as pl
from jax.experimental.pallas import tpu as pltpu

F32 = jnp.float32
BF16 = jnp.bfloat16

D_MODEL = 1024
SSM_WIDTH = 512
SSM_GROUP = 16
SSM_GROUPS = SSM_WIDTH // SSM_GROUP
SSM_STATE = 64
SSM_STATES = SSM_GROUPS * SSM_STATE
SSM_EIG_CLIP = -1e-4
N_HEADS = 16
N_KV_HEADS = 4
HEAD_DIM = 64
GQA_GROUP = N_HEADS // N_KV_HEADS
KV_DIM = N_KV_HEADS * HEAD_DIM
CMP_BLOCK = 32
CMP_STRIDE = 16
CMP_HIDDEN = 256
CMP_QUAD = 4
SLC_BLOCK = 64
SLC_TOP_N = 8
WINDOW = 512
SEL_BIG = 1e9
X_HEADS = 4
X_HEAD_DIM = D_MODEL // X_HEADS
D_FF = 2816
DEEPNORM_ALPHA = 2.0 ** 0.25
LN_EPS = 1e-5
NEG_BIG = -1e30
GATE_ROWS = 16
LOG2E = 1.4426950408889634
ATT_TILE = 256
ROW_TILE = 512
ROW_STREAMS = 2
MIX_ROW_TILE = 512
KEY_LANES = 128
ALIBI_LANE = HEAD_DIM
ONEHOT_LANE = HEAD_DIM + 8
VT_ROWS = 80
SLC_LEAD_CHUNKS = 3
HEADS_IN_FLIGHT = 2
HEAD_PHASE_LAG = 2
CHUNK_ROWS = 16

V7X_VMEM_LIMIT_BYTES = 56 * 1024 * 1024
SSM_LANE_SLAB = 128
SSM_SLABS = SSM_WIDTH // SSM_LANE_SLAB
SSM_STATE_SLAB = SSM_STATES // SSM_SLABS


def _cparams(*sem):
    return pltpu.CompilerParams(dimension_semantics=sem, vmem_limit_bytes=V7X_VMEM_LIMIT_BYTES)


def _full(shape):
    zeros = (0,) * len(shape)
    return pl.BlockSpec(shape, lambda *_: zeros, pipeline_mode=pl.Buffered(1))


def _layer_norm(x, g, b):
    mu = jnp.mean(x, axis=-1, keepdims=True)
    xc = x - mu
    var = jnp.mean(xc * xc, axis=-1, keepdims=True)
    return xc * lax.rsqrt(var + LN_EPS) * g + b


def _gelu_tanh(x):
    return 0.5 * x * (1.0 + jnp.tanh(math.sqrt(2.0 / math.pi) * (x + 0.044715 * (x * x * x))))


def _sigmoid(x):
    return 1.0 / (1.0 + jnp.exp(-x))


def _dot(a, b):
    return jnp.dot(a, b, preferred_element_type=F32)


def _dot_nt(a, b):
    return lax.dot_general(a, b, (((1,), (1,)), ((), ())), preferred_element_type=F32)


def _zoh_kernel(a_re, a_im, log_dt, b_re, b_im, lb_re_o, lb_im_o, bb_re_o, bb_im_o):
    lam_re = jnp.minimum(a_re[...], SSM_EIG_CLIP)
    lam_im = a_im[...]
    dt = jnp.exp(log_dt[...])
    mag = jnp.exp(lam_re * dt)
    lb_re = mag * jnp.cos(lam_im * dt)
    lb_im = mag * jnp.sin(lam_im * dt)
    den = lam_re * lam_re + lam_im * lam_im
    nr = lb_re - 1.0
    f_re = (nr * lam_re + lb_im * lam_im) / den
    f_im = (lb_im * lam_re - nr * lam_im) / den
    br = b_re[...]
    bi = b_im[...]
    lb_re_o[...] = lb_re
    lb_im_o[...] = lb_im
    bb_re_o[...] = f_re * br - f_im * bi
    bb_im_o[...] = f_re * bi + f_im * br


def _zoh_prep(a_re, a_im, log_dt, b_re, b_im):
    gn = SSM_STATES
    row = lambda v: v.astype(F32).reshape(1, gn)
    dt_row = jnp.broadcast_to(log_dt.astype(F32)[:, None], (SSM_GROUPS, SSM_STATE)).reshape(1, gn)
    chan = lambda v: v.astype(F32).reshape(gn, SSM_GROUP).T
    return pl.pallas_call(
        _zoh_kernel,
        out_shape=[jax.ShapeDtypeStruct((1, gn), F32)] * 2 + [jax.ShapeDtypeStruct((SSM_GROUP, gn), F32)] * 2,
        name="zoh_prep",
    )(row(a_re), row(a_im), dt_row, chan(b_re), chan(b_im))


def _inproj_kernel(x_ref, g_ref, b_ref, wu_ref, wqt_ref, wk_ref, wvst_ref, wvwt_ref, wgt_ref,
                   h_o, u_o, qt_o, kvc_o, ks_o, kw_o, vst_o, vwt_o, gt_o):
    tl = x_ref.shape[1]
    stream = tl // ROW_STREAMS
    for r in range(ROW_STREAMS):
        _inproj_rows(slice(r * stream, (r + 1) * stream), pl.program_id(1) * tl + r * stream,
                     x_ref, g_ref, b_ref, wu_ref, wqt_ref, wk_ref, wvst_ref, wvwt_ref, wgt_ref,
                     h_o, u_o, qt_o, kvc_o, ks_o, kw_o, vst_o, vwt_o, gt_o)


def _inproj_rows(rows, first_pos, x_ref, g_ref, b_ref, wu_ref, wqt_ref, wk_ref, wvst_ref, wvwt_ref, wgt_ref,
                 h_o, u_o, qt_o, kvc_o, ks_o, kw_o, vst_o, vwt_o, gt_o):
    tl = rows.stop - rows.start
    h = _layer_norm(x_ref[0, rows, :], g_ref[...], b_ref[...])
    h_o[0, rows, :] = h
    hb = h.astype(BF16)
    u_o[0, rows, :] = _dot(hb, wu_ref[...])
    qt_o[0, :, rows] = (_dot_nt(wqt_ref[...], hb) * (HEAD_DIM ** -0.5 * LOG2E)).astype(BF16)

    k_all = _dot(hb, wk_ref[...])

    def head_tile(j):
        tile = k_all[:, (j // 2) * KEY_LANES:(j // 2 + 1) * KEY_LANES]
        return pltpu.roll(tile, HEAD_DIM, 1) if j % 2 else tile

    for j in range(N_KV_HEADS):
        kvc_o[0, j, rows, :] = k_all[:, j * KEY_LANES:(j + 1) * KEY_LANES]

    lane = lax.broadcasted_iota(jnp.int32, (tl, KEY_LANES), 1)
    pos = first_pos + lax.broadcasted_iota(jnp.int32, (tl, KEY_LANES), 0)
    alibi = jnp.where((lane >= ALIBI_LANE) & (lane < ALIBI_LANE + 3), (pos % ATT_TILE).astype(F32), 0.0)
    slc_feat = alibi + jnp.where((lane >= ONEHOT_LANE) & (lane - ONEHOT_LANE == pos // SLC_BLOCK), 1.0, 0.0)
    is_key = lane < HEAD_DIM
    for j in range(N_KV_HEADS):
        ks_o[0, j, rows, :] = jnp.where(is_key, head_tile(2 * N_KV_HEADS + j), slc_feat).astype(BF16)
        kw_o[0, j, rows, :] = jnp.where(is_key, head_tile(3 * N_KV_HEADS + j), alibi).astype(BF16)
    row = lax.broadcasted_iota(jnp.int32, (N_KV_HEADS * VT_ROWS, tl), 0)
    ones_row = jnp.where(row % VT_ROWS == HEAD_DIM, 1.0, 0.0)
    vst_o[0, :, rows] = (_dot_nt(wvst_ref[...], hb) + ones_row).astype(BF16)
    vwt_o[0, :, rows] = (_dot_nt(wvwt_ref[...], hb) + ones_row).astype(BF16)
    gt_o[0, :, rows] = _sigmoid(_dot_nt(wgt_ref[...], hb))


def _ln_inproj(x, ln_g, ln_b, w, tl):
    bsz, L, d = x.shape
    n_gate = N_KV_HEADS * GATE_ROWS
    grid = (bsz, L // tl)
    in_specs = [
        pl.BlockSpec((1, tl, d), lambda b, i: (b, i, 0)),
        _full((1, d)), _full((1, d)),
        _full((d, SSM_WIDTH)),
        _full((D_MODEL, d)),
        _full((d, 4 * KV_DIM)),
        _full((N_KV_HEADS * VT_ROWS, d)),
        _full((N_KV_HEADS * VT_ROWS, d)),
        _full((n_gate, d)),
    ]
    assert tl % ATT_TILE == 0 or ATT_TILE % tl == 0
    assert ONEHOT_LANE + L // SLC_BLOCK <= KEY_LANES
    out_shape = [
        jax.ShapeDtypeStruct((bsz, L, d), F32),
        jax.ShapeDtypeStruct((bsz, L, SSM_WIDTH), F32),
        jax.ShapeDtypeStruct((bsz, D_MODEL, L), BF16),
        jax.ShapeDtypeStruct((bsz, N_KV_HEADS, L, 2 * HEAD_DIM), F32),
        jax.ShapeDtypeStruct((bsz, N_KV_HEADS, L, KEY_LANES), BF16),
        jax.ShapeDtypeStruct((bsz, N_KV_HEADS, L, KEY_LANES), BF16),
        jax.ShapeDtypeStruct((bsz, N_KV_HEADS * VT_ROWS, L), BF16),
        jax.ShapeDtypeStruct((bsz, N_KV_HEADS * VT_ROWS, L), BF16),
        jax.ShapeDtypeStruct((bsz, n_gate, L), F32),
    ]
    out_specs = [
        pl.BlockSpec((1, tl, d), lambda b, i: (b, i, 0)),
        pl.BlockSpec((1, tl, SSM_WIDTH), lambda b, i: (b, i, 0)),
        pl.BlockSpec((1, D_MODEL, tl), lambda b, i: (b, 0, i)),
        pl.BlockSpec((1, N_KV_HEADS, tl, 2 * HEAD_DIM), lambda b, i: (b, 0, i, 0)),
        pl.BlockSpec((1, N_KV_HEADS, tl, KEY_LANES), lambda b, i: (b, 0, i, 0)),
        pl.BlockSpec((1, N_KV_HEADS, tl, KEY_LANES), lambda b, i: (b, 0, i, 0)),
        pl.BlockSpec((1, N_KV_HEADS * VT_ROWS, tl), lambda b, i: (b, 0, i)),
        pl.BlockSpec((1, N_KV_HEADS * VT_ROWS, tl), lambda b, i: (b, 0, i)),
        pl.BlockSpec((1, n_gate, tl), lambda b, i: (b, 0, i)),
    ]
    return pl.pallas_call(
        _inproj_kernel, grid=grid, in_specs=in_specs, out_specs=out_specs, out_shape=out_shape,
        compiler_params=_cparams("parallel", "parallel"), name="ln_inproj",
    )(x, ln_g.reshape(1, d), ln_b.reshape(1, d), w["wu"], w["wqt"], w["wk"],
      w["wvst"], w["wvwt"], w["wgt"])


def _s5_kernel(u_ref, lre_ref, lim_ref, wbre_ref, wbim_ref, cre_ref, cim_ref, d_ref, wglu_ref, bglu_ref, wo_ref,
               y_o, sre, sim, hre, him, *, bsz, steps, pitch):
    @pl.when(pl.program_id(0) == 0)
    def _():
        sre[...] = jnp.zeros_like(sre)
        sim[...] = jnp.zeros_like(sim)

    lanes = SSM_LANE_SLAB
    per_slab = SSM_STATE_SLAB // lanes
    u = u_ref[...].reshape(bsz * steps, SSM_WIDTH)
    ub = u.astype(BF16)

    def project_in(j):
        uj = ub[:, j * lanes:(j + 1) * lanes]
        for w_ref, h_ref in ((wbre_ref, hre), (wbim_ref, him)):
            r = _dot(uj, w_ref[j])
            for b in range(bsz):
                for k in range(per_slab):
                    h_ref[j * per_slab + k, pl.ds(b, steps, stride=pitch), :] = (
                        r[b * steps:(b + 1) * steps, k * lanes:(k + 1) * lanes])

    def recur(j):
        for s in range(j * per_slab, (j + 1) * per_slab):
            cols = slice(s * lanes, (s + 1) * lanes)
            lr = jnp.broadcast_to(lre_ref[:, cols], (bsz, lanes))
            li = jnp.broadcast_to(lim_ref[:, cols], (bsz, lanes))
            pr, pi = sre[:, cols], sim[:, cols]
            for t in range(steps):
                rows = pl.ds(t * pitch, bsz)
                pr, pi = (lr * pr - li * pi + hre[s, rows, :], lr * pi + li * pr + him[s, rows, :])
                hre[s, rows, :] = pr
                him[s, rows, :] = pi
            sre[:, cols] = pr
            sim[:, cols] = pi

    def states(h_ref, j):
        return jnp.concatenate(
            [jnp.concatenate([h_ref[j * per_slab + k, pl.ds(b, steps, stride=pitch), :] for k in range(per_slab)],
                             axis=-1) for b in range(bsz)], axis=0)

    ys = [None] * SSM_SLABS

    def project_out(j):
        ys[j] = _dot(states(hre, j).astype(BF16), cre_ref[j]) - _dot(states(him, j).astype(BF16), cim_ref[j])

    for j in range(SSM_SLABS + 2):
        if j < SSM_SLABS:
            project_in(j)
        if 0 <= j - 1 < SSM_SLABS:
            recur(j - 1)
        if 0 <= j - 2 < SSM_SLABS:
            project_out(j - 2)
    y = jnp.concatenate(ys, axis=-1) + d_ref[...] * u
    g = _gelu_tanh(y)
    y2 = g * _sigmoid(_dot(g.astype(BF16), wglu_ref[...]) + bglu_ref[...])
    y_o[...] = _dot(y2.astype(BF16), wo_ref[...]).reshape(bsz, steps, D_MODEL)


def _s5(u, lb_re, lb_im, w, steps):
    bsz, L, _ = u.shape
    pitch = -(-bsz // 4) * 4
    pitch += 4 if (pitch // 4) % 2 == 0 else 0
    grid = (L // steps,)
    kern = functools.partial(_s5_kernel, bsz=bsz, steps=steps, pitch=pitch)
    n_slabs = SSM_STATES // SSM_LANE_SLAB
    in_specs = [
        pl.BlockSpec((bsz, steps, SSM_WIDTH), lambda c: (0, c, 0)),
        _full((1, SSM_STATES)), _full((1, SSM_STATES)),
        _full((SSM_SLABS, SSM_LANE_SLAB, SSM_STATE_SLAB)), _full((SSM_SLABS, SSM_LANE_SLAB, SSM_STATE_SLAB)),
        _full((SSM_SLABS, SSM_STATE_SLAB, SSM_LANE_SLAB)), _full((SSM_SLABS, SSM_STATE_SLAB, SSM_LANE_SLAB)),
        _full((1, SSM_WIDTH)),
        _full((SSM_WIDTH, SSM_WIDTH)), _full((1, SSM_WIDTH)),
        _full((SSM_WIDTH, D_MODEL)),
    ]
    return pl.pallas_call(
        kern, grid=grid, in_specs=in_specs,
        out_specs=pl.BlockSpec((bsz, steps, D_MODEL), lambda c: (0, c, 0)),
        out_shape=jax.ShapeDtypeStruct((bsz, L, D_MODEL), F32),
        scratch_shapes=[pltpu.VMEM((bsz, SSM_STATES), F32), pltpu.VMEM((bsz, SSM_STATES), F32),
                        pltpu.VMEM((n_slabs, steps * pitch, SSM_LANE_SLAB), F32),
                        pltpu.VMEM((n_slabs, steps * pitch, SSM_LANE_SLAB), F32)],
        compiler_params=_cparams("arbitrary"), name="s5",
    )(u, lb_re, lb_im, w["wbre"], w["wbim"], w["cre"], w["cim"], w["d"], w["wglu"], w["bglu"], w["wo"])


def _compress_kernel(kv_ref, pos_ref, w1_ref, b1_ref, w2k_ref, w2vt_ref, kc_o, vct_o, *, n_chunks):
    half = CMP_BLOCK // 2
    quad = CMP_QUAD
    low_half = lax.broadcasted_iota(jnp.int32, (n_chunks, 2 * HEAD_DIM), 1) < HEAD_DIM
    for tile in range(2 * N_KV_HEADS // 2):
        z = tile // (N_KV_HEADS // 2)
        heads = (2 * (tile % (N_KV_HEADS // 2)), 2 * (tile % (N_KV_HEADS // 2)) + 1)
        first = [jnp.zeros((n_chunks, CMP_HIDDEN), F32) for _ in heads]
        second = [jnp.zeros((n_chunks, CMP_HIDDEN), F32) for _ in heads]
        for q in range(half // quad):
            rows = [kv_ref[0, tile, pl.ds(q * quad + r, n_chunks, stride=CMP_STRIDE), :] for r in range(quad)]
            turned = [pltpu.roll(x, HEAD_DIM, 1) for x in rows]
            for side in range(2):
                pairs = [jnp.where(low_half, rows[r], turned[r + 1]) if side == 0 else
                         jnp.where(low_half, turned[r], rows[r + 1]) for r in range(0, quad, 2)]
                x4 = jnp.concatenate(pairs, axis=-1)
                for part, acc in ((0, first), (1, second)):
                    qq = part * (half // quad) + q
                    w = w1_ref[z, qq * quad * HEAD_DIM:(qq + 1) * quad * HEAD_DIM, :]
                    acc[side] += _dot((x4 + pos_ref[z, qq:qq + 1, :]).astype(BF16), w)
        for side, hh in enumerate(heads):
            pre = first[side] + pltpu.roll(second[side], n_chunks - 1, 0) + b1_ref[z]
            hid = _gelu_tanh(pre).astype(BF16)
            if z == 0:
                kc_o[0, hh] = _dot(hid, w2k_ref[...]).astype(BF16)
            else:
                vct_o[0, hh * HEAD_DIM:(hh + 1) * HEAD_DIM, :] = _dot_nt(w2vt_ref[...], hid).astype(BF16)


def _compress(kvc, pos, w1, b1, w2k, w2vt):
    bsz, _, L, _ = kvc.shape
    n_chunks = L // CMP_STRIDE
    kern = functools.partial(_compress_kernel, n_chunks=n_chunks)
    return pl.pallas_call(
        kern, grid=(bsz,),
        in_specs=[
            pl.BlockSpec((1, N_KV_HEADS, L, 2 * HEAD_DIM), lambda b: (b, 0, 0, 0)),
            _full((2, CMP_BLOCK // CMP_QUAD, CMP_QUAD * HEAD_DIM)),
            _full((2, CMP_BLOCK * HEAD_DIM, CMP_HIDDEN)),
            _full((2, 1, CMP_HIDDEN)),
            _full((CMP_HIDDEN, HEAD_DIM)),
            _full((HEAD_DIM, CMP_HIDDEN)),
        ],
        out_specs=[
            pl.BlockSpec((1, N_KV_HEADS, n_chunks, HEAD_DIM), lambda b: (b, 0, 0, 0)),
            pl.BlockSpec((1, KV_DIM, n_chunks), lambda b: (b, 0, 0)),
        ],
        out_shape=[
            jax.ShapeDtypeStruct((bsz, N_KV_HEADS, n_chunks, HEAD_DIM), BF16),
            jax.ShapeDtypeStruct((bsz, KV_DIM, n_chunks), BF16),
        ],
        compiler_params=_cparams("parallel"), name="compress",
    )(kvc, pos, w1, b1, w2k, w2vt)


def _nsa_kernel(*refs, tq, n_cmp_pad, n_slc):
    def head_pair(pair, carry):
        heads = [_nsa_head(HEADS_IN_FLIGHT * pair + slot, slot, *refs, tq=tq, n_cmp_pad=n_cmp_pad, n_slc=n_slc)
                 for slot in range(HEADS_IN_FLIGHT)]
        tails, live, step = {}, set(range(HEADS_IN_FLIGHT)), 0
        while live:
            for slot in sorted(live):
                if step >= slot * HEAD_PHASE_LAG:
                    try:
                        out = next(heads[slot])
                        if out is not None:
                            tails[slot] = out
                    except StopIteration:
                        live.discard(slot)
            step += 1
        for slot in range(HEADS_IN_FLIGHT):
            tails[slot]()
        return carry

    lax.fori_loop(0, N_KV_HEADS // HEADS_IN_FLIGHT, head_pair, 0)


def _nsa_head(hkv, par, qt_ref, kc_ref, vct_ref, ks_ref, vst_ref, kw_ref, vwt_ref, gt_ref, ov_ref, causal_ref,
              far_ref, chunk_ref, o_ref, qa_ref, qw_ref, m_ref, acc_ref, s_ref, mx_ref, al_ref, sc_ref, pc_ref,
              need_ref, *, tq, n_cmp_pad, n_slc):
    i = pl.program_id(1)
    slots_per_head = s_ref.shape[0] // HEADS_IN_FLIGHT
    t0 = i * tq
    tk = tq
    n_sel = min(SLC_TOP_N, n_slc)
    slopes = [jnp.exp2(jnp.full((1, tq), -0.5, F32) * jnp.asarray(hkv * GQA_GROUP + g + 1, F32)) * LOG2E
              for g in range(GQA_GROUP)]
    head_rows = lambda g: pl.ds(pl.multiple_of((hkv * GQA_GROUP + g) * HEAD_DIM, HEAD_DIM), HEAD_DIM)
    vt_rows = pl.ds(pl.multiple_of(hkv * VT_ROWS, 16), VT_ROWS)
    q_heads = [qt_ref[0, head_rows(g), :] for g in range(GQA_GROUP)]
    gate = lambda g, z: gt_ref[0, pl.ds(hkv * GATE_ROWS + 3 * g + z, 1), :]

    t_lane = t0 + lax.broadcasted_iota(jnp.int32, (1, tq), 1)
    row8 = lax.broadcasted_iota(jnp.int32, (ONEHOT_LANE - ALIBI_LANE, tq), 0)

    def augmented(g, tail_rows):
        hi = slopes[g].astype(BF16).astype(F32)
        mid = (slopes[g] - hi).astype(BF16).astype(F32)
        lo = slopes[g] - hi - mid
        parts = jnp.where(row8 == 0, hi, jnp.where(row8 == 1, mid, jnp.where(row8 == 2, lo, 0.0)))
        return jnp.concatenate([q_heads[g].astype(F32), parts, tail_rows], axis=0).astype(BF16)

    def attend(state, q_ref, items, first, slot0):
        starts = [pl.multiple_of((c if valid is None else jnp.maximum(c, 0)) * tk, tk) for _, _, c, _, valid in items]
        slot0 = slot0 + par * slots_per_head

        def scores(g):
            for n, (k_ref, _, _, bias_ref, _) in enumerate(items):
                s = _dot(k_ref[0, hkv, pl.ds(starts[n], tk), :], q_ref[par, g])
                if bias_ref is not None:
                    s = s + bias_ref[...]
                s_ref[slot0 + n * GQA_GROUP + g] = s
                mx_ref[slot0 + n * GQA_GROUP + g] = jnp.max(s, axis=0, keepdims=True)

        def fold(g):
            shifts = []
            for _, _, c, _, valid in items:
                shift = slopes[g] * jnp.asarray((c - i) * tk, F32)
                shifts.append(shift if valid is None else jnp.where(valid, shift, NEG_BIG))
            m_new = functools.reduce(jnp.maximum, [mx_ref[slot0 + n * GQA_GROUP + g] + shifts[n]
                                                   for n in range(len(items))])
            if not first:
                m_new = jnp.maximum(m_ref[par, state, g], m_new)
                al_ref[par, g] = jnp.exp2(m_ref[par, state, g] - m_new)
            pv = None
            for n, (_, vt_ref, _, _, _) in enumerate(items):
                slot = slot0 + n * GQA_GROUP + g
                p = jnp.exp2(s_ref[slot] - (m_new - shifts[n])).astype(BF16)
                part = _dot(vt_ref[0, vt_rows, pl.ds(starts[n], tk)], p)
                pv = part if pv is None else pv + part
            acc_ref[par, state, g] = pv if first else al_ref[par, g] * acc_ref[par, state, g] + pv
            m_ref[par, state, g] = m_new

        scores(0)
        for g in range(GQA_GROUP):
            if g + 1 < GQA_GROUP:
                scores(g + 1)
            fold(g)
            yield

    def emit(state, z, accumulate):
        for g in range(GQA_GROUP):
            scale = gate(g, z) / acc_ref[par, state, g, HEAD_DIM:HEAD_DIM + 1, :]
            out = acc_ref[par, state, g, :HEAD_DIM, :] * scale
            o_ref[0, head_rows(g), :] = o_ref[0, head_rows(g), :] + out if accumulate else out

    n_idx = lax.broadcasted_iota(jnp.int32, (n_cmp_pad, tq), 0)
    dist_c = (t0 + lax.broadcasted_iota(jnp.int32, (n_cmp_pad, tq), 1)) - (n_idx * CMP_STRIDE + (CMP_BLOCK - 1))
    mask_c = dist_c >= 0
    dist_cf = dist_c.astype(F32)
    kc = kc_ref[0, hkv]
    vct = vct_ref[0, pl.ds(pl.multiple_of(hkv * HEAD_DIM, HEAD_DIM), HEAD_DIM), :]
    for g in range(GQA_GROUP):
        s = _dot(kc, q_heads[g]) - slopes[g] * dist_cf
        sc_ref[par, g] = jnp.where(mask_c, s, NEG_BIG)
    yield
    p_sum = jnp.zeros((n_cmp_pad, tq), F32)
    for g in range(GQA_GROUP):
        s = sc_ref[par, g]
        m = jnp.max(s, axis=0, keepdims=True)
        e = jnp.exp2(s - m)
        norm = 1.0 / jnp.maximum(jnp.sum(e, axis=0, keepdims=True), 1e-30)
        p = e * jnp.where(m > 0.5 * NEG_BIG, norm, 0.0)
        pc_ref[par, g] = p.astype(BF16)
        p_sum = p_sum + p
    for g in range(GQA_GROUP):
        o_ref[0, head_rows(g), :] = gate(g, 0) * _dot(vct, pc_ref[par, g])

    imp = lax.dot_general(ov_ref[...], p_sum, (((1,), (0,)), ((), ())), precision=lax.Precision.HIGHEST,
                          preferred_element_type=F32)
    yield
    blk = lax.broadcasted_iota(jnp.int32, (n_slc, tq), 0)
    cur = t_lane // SLC_BLOCK
    forced = (blk == 0) | (blk == cur) | (blk == cur - 1)
    future = blk * SLC_BLOCK > t_lane
    imp = jnp.where(forced, SEL_BIG, jnp.where(future, -SEL_BIG, imp))
    sub8 = 8
    tiles = [imp[r:r + sub8] for r in range(0, n_slc, sub8)]
    ranks = [jnp.zeros((sub8, tq), jnp.int32) for _ in tiles]
    for j in range(n_slc):
        row = imp[j:j + 1, :]
        for r, tile in enumerate(tiles):
            if r * sub8 > j:
                ahead = row >= tile
            elif r * sub8 + sub8 - 1 < j:
                ahead = row > tile
            else:
                later = r * sub8 + lax.broadcasted_iota(jnp.int32, (sub8, tq), 0) > j
                ahead = (row > tile) | ((row == tile) & later)
            ranks[r] = ranks[r] + ahead.astype(jnp.int32)
    rank = jnp.concatenate(ranks, axis=0)
    sel_bias = jnp.where(rank < n_sel, 0.0, NEG_BIG)

    qa_rows = jnp.concatenate([sel_bias, jnp.zeros((KEY_LANES - ONEHOT_LANE - n_slc, tq), F32)], axis=0)
    for g in range(GQA_GROUP):
        qa_ref[par, g] = augmented(g, qa_rows)
    yield

    zero_rows = jnp.zeros((KEY_LANES - ONEHOT_LANE, tq), F32)
    for g in range(GQA_GROUP):
        qw_ref[par, g] = augmented(g, zero_rows)
    n_back = WINDOW // tk
    n_win_items = n_back + 1
    win_items = [(kw_ref, vwt_ref, i, causal_ref, None)]
    for back in range(1, n_back + 1):
        win_items.append((kw_ref, vwt_ref, i - back, far_ref if back == n_back else None, i >= back))
    yield from attend(0, qw_ref, win_items, True, 0)
    emit(0, 2, True)
    yield

    slc_items = [(ks_ref, vst_ref, i, causal_ref, None)]
    for back in range(1, SLC_LEAD_CHUNKS):
        slc_items.append((ks_ref, vst_ref, i - back, None, i >= back))
    slc_items.append((ks_ref, vst_ref, 0, None, i >= SLC_LEAD_CHUNKS))
    yield from attend(1, qa_ref, slc_items, True, n_win_items * GQA_GROUP)

    picked = (rank < n_sel).astype(BF16)
    per_chunk = jnp.sum(_dot(chunk_ref[...], picked), axis=1, keepdims=True)
    need0 = par * CHUNK_ROWS
    for c in range(n_slc * SLC_BLOCK // tk):
        need_ref[need0 + c] = (per_chunk[c, 0] > 0.0).astype(jnp.int32)
    last = i - SLC_LEAD_CHUNKS

    def slc_step(pair, carry):
        c0 = 1 + 2 * pair
        c1 = jnp.minimum(c0 + 1, last)
        use0 = need_ref[need0 + c0] > 0
        use1 = (c0 + 1 <= last) & (need_ref[need0 + c1] > 0)

        @pl.when(use0 | use1)
        def _():
            for _ in attend(1, qa_ref, [(ks_ref, vst_ref, c0, None, use0), (ks_ref, vst_ref, c1, None, use1)], False, 0):
                pass
        return carry

    def tail():
        lax.fori_loop(0, jnp.maximum(last + 1, 0) // 2, slc_step, 0)
        emit(1, 1, True)

    yield tail


def _nsa_attn(qt, kc, vct, ks, vst, kw, vwt, gt, tq):
    bsz, _, L = qt.shape
    n_cmp_pad = kc.shape[2]
    n_slc = L // SLC_BLOCK
    n = jnp.arange(n_cmp_pad)[None, :]
    j = jnp.arange(n_slc)[:, None]
    overlap_t = ((n * CMP_STRIDE < (j + 1) * SLC_BLOCK) & (n * CMP_STRIDE + CMP_BLOCK - 1 >= j * SLC_BLOCK)).astype(F32)
    assert tq == ATT_TILE and WINDOW % tq == 0 and n_slc % 8 == 0
    n_slots = (WINDOW // tq + 1 + SLC_LEAD_CHUNKS + 1) * GQA_GROUP
    sub = jnp.arange(tq)[:, None]
    lane = jnp.arange(tq)[None, :]
    causal_bias = jnp.where(sub <= lane, 0.0, NEG_BIG).astype(F32)
    far_bias = jnp.where(sub > lane, 0.0, NEG_BIG).astype(F32)
    assert L // tq <= CHUNK_ROWS
    chunk_of_block = (jnp.arange(CHUNK_ROWS)[:, None] == jnp.arange(n_slc)[None, :] * SLC_BLOCK // tq).astype(BF16)
    kern = functools.partial(_nsa_kernel, tq=tq, n_cmp_pad=n_cmp_pad, n_slc=n_slc)
    return pl.pallas_call(
        kern, grid=(bsz, L // tq),
        in_specs=[
            pl.BlockSpec((1, D_MODEL, tq), lambda b, i: (b, 0, i)),
            pl.BlockSpec((1, N_KV_HEADS, n_cmp_pad, HEAD_DIM), lambda b, i: (b, 0, 0, 0)),
            pl.BlockSpec((1, KV_DIM, n_cmp_pad), lambda b, i: (b, 0, 0)),
            pl.BlockSpec((1, N_KV_HEADS, L, KEY_LANES), lambda b, i: (b, 0, 0, 0)),
            pl.BlockSpec((1, N_KV_HEADS * VT_ROWS, L), lambda b, i: (b, 0, 0)),
            pl.BlockSpec((1, N_KV_HEADS, L, KEY_LANES), lambda b, i: (b, 0, 0, 0)),
            pl.BlockSpec((1, N_KV_HEADS * VT_ROWS, L), lambda b, i: (b, 0, 0)),
            pl.BlockSpec((1, N_KV_HEADS * GATE_ROWS, tq), lambda b, i: (b, 0, i)),
            _full((n_slc, n_cmp_pad)),
            _full((tq, tq)), _full((tq, tq)),
            _full((CHUNK_ROWS, n_slc)),
        ],
        out_specs=pl.BlockSpec((1, D_MODEL, tq), lambda b, i: (b, 0, i)),
        out_shape=jax.ShapeDtypeStruct((bsz, D_MODEL, L), F32),
        scratch_shapes=[pltpu.VMEM((HEADS_IN_FLIGHT, GQA_GROUP, KEY_LANES, tq), BF16),
                        pltpu.VMEM((HEADS_IN_FLIGHT, GQA_GROUP, KEY_LANES, tq), BF16),
                        pltpu.VMEM((HEADS_IN_FLIGHT, 2, GQA_GROUP, 1, tq), F32),
                        pltpu.VMEM((HEADS_IN_FLIGHT, 2, GQA_GROUP, VT_ROWS, tq), F32),
                        pltpu.VMEM((HEADS_IN_FLIGHT * n_slots, tq, tq), F32),
                        pltpu.VMEM((HEADS_IN_FLIGHT * n_slots, 1, tq), F32),
                        pltpu.VMEM((HEADS_IN_FLIGHT, GQA_GROUP, 1, tq), F32),
                        pltpu.VMEM((HEADS_IN_FLIGHT, GQA_GROUP, n_cmp_pad, tq), F32),
                        pltpu.VMEM((HEADS_IN_FLIGHT, GQA_GROUP, n_cmp_pad, tq), BF16),
                        pltpu.SMEM((HEADS_IN_FLIGHT * CHUNK_ROWS,), jnp.int32)],
        compiler_params=_cparams("parallel", "parallel"), name="nsa_attn",
    )(qt, kc, vct, ks, vst, kw, vwt, gt, overlap_t, causal_bias, far_bias, chunk_of_block)


def _mix_kernel(h_ref, ys_ref, ot_ref, wgm_ref, wno_ref, wout_ref, g_ref, b_ref, h1_o):
    stream = h_ref.shape[1] // ROW_STREAMS
    for r in range(ROW_STREAMS):
        rows = slice(r * stream, (r + 1) * stream)
        h = h_ref[0, rows, :]
        o = ot_ref[0, :, rows].T.astype(BF16)
        y_nsa = _dot(o, wno_ref[...])
        gm = _sigmoid(_dot(h.astype(BF16), wgm_ref[...]))
        mix = gm[:, :D_MODEL] * ys_ref[0, rows, :] + gm[:, D_MODEL:] * y_nsa
        mixed = _dot(mix.astype(BF16), wout_ref[...])
        h1_o[0, rows, :] = _layer_norm(DEEPNORM_ALPHA * h + mixed, g_ref[...], b_ref[...])


def _mix_ln1(h, y_ssm, o_t, wgm, wno, wout, g, b, tm):
    bsz, L, d = h.shape
    return pl.pallas_call(
        _mix_kernel, grid=(bsz, L // tm),
        in_specs=[
            pl.BlockSpec((1, tm, d), lambda b, i: (b, i, 0)),
            pl.BlockSpec((1, tm, d), lambda b, i: (b, i, 0)),
            pl.BlockSpec((1, d, tm), lambda b, i: (b, 0, i)),
            _full((d, 2 * d)), _full((d, d)), _full((d, d)), _full((1, d)), _full((1, d)),
        ],
        out_specs=pl.BlockSpec((1, tm, d), lambda b, i: (b, i, 0)),
        out_shape=jax.ShapeDtypeStruct((bsz, L, d), F32),
        compiler_params=_cparams("parallel", "parallel"), name="mix_ln1",
    )(h, y_ssm, o_t, wgm, wno, wout, g.reshape(1, d), b.reshape(1, d))


def _memkv_kernel(mem_ref, w_ref, k_o, v_o):
    kv = _dot(mem_ref[0].astype(BF16), w_ref[...])
    k_o[0] = kv[:, :D_MODEL].astype(BF16)
    v_o[0] = kv[:, D_MODEL:].astype(BF16)


def _mem_kv(mem, w_kv):
    bsz, m, d = mem.shape
    spec = pl.BlockSpec((1, m, d), lambda b: (b, 0, 0))
    return pl.pallas_call(
        _memkv_kernel, grid=(bsz,),
        in_specs=[spec, _full((d, 2 * d))],
        out_specs=[spec, spec],
        out_shape=[jax.ShapeDtypeStruct((bsz, m, d), BF16)] * 2,
        compiler_params=_cparams("parallel"), name="mem_kv",
    )(mem, w_kv)


def _xattn_kernel(h_ref, k_ref, v_ref, wq_ref, wo_ref, g_ref, b_ref, h2_o):
    h = h_ref[0]
    q = (_dot(h.astype(BF16), wq_ref[...]) * (X_HEAD_DIM ** -0.5 * LOG2E)).astype(BF16)
    cols = [slice(hd * X_HEAD_DIM, (hd + 1) * X_HEAD_DIM) for hd in range(X_HEADS)]
    scores = lambda hd: _dot_nt(q[:, cols[hd]], k_ref[0, :, cols[hd]])
    outs = []
    s_next = scores(0)
    for hd in range(X_HEADS):
        s = s_next
        if hd + 1 < X_HEADS:
            s_next = scores(hd + 1)
        p = jnp.exp2(s - jnp.max(s, axis=-1, keepdims=True))
        norm = 1.0 / jnp.sum(p, axis=-1, keepdims=True)
        outs.append(_dot(p.astype(BF16), v_ref[0, :, cols[hd]]) * norm)
    o = jnp.concatenate(outs, axis=-1).astype(BF16)
    h2_o[0] = _layer_norm(DEEPNORM_ALPHA * h + _dot(o, wo_ref[...]), g_ref[...], b_ref[...])


def _xattn_ln2(h, k, v, wq, wo, g, b, tm):
    bsz, L, d = h.shape
    m = k.shape[1]
    row = pl.BlockSpec((1, tm, d), lambda b, i: (b, i, 0))
    kv = pl.BlockSpec((1, m, d), lambda b, i: (b, 0, 0))
    return pl.pallas_call(
        _xattn_kernel, grid=(bsz, L // tm),
        in_specs=[row, kv, kv, _full((d, d)), _full((d, d)), _full((1, d)), _full((1, d))],
        out_specs=row,
        out_shape=jax.ShapeDtypeStruct((bsz, L, d), F32),
        compiler_params=_cparams("parallel", "parallel"), name="xattn_ln2",
    )(h, k, v, wq, wo, g.reshape(1, d), b.reshape(1, d))


def _ffn_kernel(h_ref, win_ref, wout_ref, g_ref, b_ref, o_ref):
    stream = h_ref.shape[0] // ROW_STREAMS
    for r in range(ROW_STREAMS):
        rows = slice(r * stream, (r + 1) * stream)
        h = h_ref[rows, :]
        gu = _dot(h.astype(BF16), win_ref[...])
        gate = gu[:, :D_FF]
        act = gate * _sigmoid(gate) * gu[:, D_FF:]
        o_ref[rows, :] = _layer_norm(DEEPNORM_ALPHA * h + _dot(act.astype(BF16), wout_ref[...]),
                                     g_ref[...], b_ref[...])


def _ffn_ln3(h, win, wout, g, b, tm):
    rows, d = h.shape
    row = pl.BlockSpec((tm, d), lambda i: (i, 0))
    return pl.pallas_call(
        _ffn_kernel, grid=(rows // tm,),
        in_specs=[row, _full((d, 2 * D_FF)), _full((D_FF, d)), _full((1, d)), _full((1, d))],
        out_specs=row,
        out_shape=jax.ShapeDtypeStruct((rows, d), F32),
        compiler_params=_cparams("parallel"), name="ffn_ln3",
    )(h, win, wout, g.reshape(1, d), b.reshape(1, d))


def _inproj_weights(w_in):
    d = D_MODEL
    c0 = SSM_WIDTH
    c1 = c0 + N_HEADS * HEAD_DIM
    c2 = c1 + 2 * KV_DIM
    c3 = c2 + 2 * KV_DIM
    c4 = c3 + 2 * KV_DIM
    c5 = c4 + 3 * N_HEADS
    values_t = lambda m: jnp.pad(m.T.reshape(N_KV_HEADS, HEAD_DIM, d),
                                 ((0, 0), (0, VT_ROWS - HEAD_DIM), (0, 0))).reshape(N_KV_HEADS * VT_ROWS, d)
    wg = w_in[:, c4:c5].T.reshape(N_KV_HEADS, 3 * GQA_GROUP, d)
    wg = jnp.pad(wg, ((0, 0), (0, GATE_ROWS - 3 * GQA_GROUP), (0, 0))).reshape(N_KV_HEADS * GATE_ROWS, d)
    return {
        "wu": w_in[:, :c0].astype(BF16),
        "wqt": w_in[:, c0:c1].T.astype(BF16),
        "wk": jnp.concatenate([w_in[:, c1:c2], w_in[:, c2:c2 + KV_DIM], w_in[:, c3:c3 + KV_DIM]], axis=1).astype(BF16),
        "wvst": values_t(w_in[:, c2 + KV_DIM:c3]).astype(BF16),
        "wvwt": values_t(w_in[:, c3 + KV_DIM:c4]).astype(BF16),
        "wgt": wg.astype(BF16),
        "wgm": w_in[:, c5:].astype(BF16),
    }


def _s5_weights(bb_re, bb_im, c_re, c_im, d_skip, w_glu, b_glu, w_o):
    per_slab = SSM_LANE_SLAB // SSM_GROUP
    eye = jnp.eye(per_slab, dtype=F32)

    def b_blocks(bb_t):
        bb_t = bb_t.reshape(SSM_GROUP, SSM_SLABS, per_slab, SSM_STATE)
        return jnp.einsum('cjgn,gh->jgchn', bb_t, eye).reshape(SSM_SLABS, SSM_LANE_SLAB, SSM_STATE_SLAB).astype(BF16)

    def c_blocks(c):
        c = c.astype(F32).reshape(SSM_SLABS, per_slab, SSM_GROUP, SSM_STATE)
        return jnp.einsum('jgcn,gh->jgnhc', c, eye).reshape(SSM_SLABS, SSM_STATE_SLAB, SSM_LANE_SLAB).astype(BF16)

    return {
        "wbre": b_blocks(bb_re), "wbim": b_blocks(bb_im), "cre": c_blocks(c_re), "cim": c_blocks(c_im),
        "d": d_skip.astype(F32).reshape(1, SSM_WIDTH), "wglu": w_glu.astype(BF16),
        "bglu": b_glu.astype(F32).reshape(1, SSM_WIDTH), "wo": w_o.astype(BF16),
    }


def _pick(total, want):
    return want if total % want == 0 else total


def kernel(x, mem, ln_emb_g, ln_emb_b, w_in, ssm_a_re, ssm_a_im, ssm_b_re, ssm_b_im, ssm_c_re, ssm_c_im, ssm_d,
           ssm_log_dt, ssm_w_glu, ssm_b_glu, ssm_w_out, cmp_pos, cmp_w1, cmp_b1, cmp_w2, nsa_w_out, w_out,
           ln1_g, ln1_b, xattn_w_q, xattn_w_kv, xattn_w_o, ln2_g, ln2_b, ffn_w_in, ffn_w_out, ln3_g, ln3_b):
    bsz, L, d = x.shape
    assert w_in.shape[0] == 1, "one layer: the trunk-entry LayerNorm is fused into its input projection"
    l = 0
    att_tile = _pick(L, ATT_TILE)
    row_tile = _pick(L, ROW_TILE)
    wi = _inproj_weights(w_in[l])
    h, u, qt, kvc, ks, kw, vst, vwt, gt = _ln_inproj(x, ln_emb_g, ln_emb_b, wi, row_tile)

    lb_re, lb_im, bb_re, bb_im = _zoh_prep(ssm_a_re[l], ssm_a_im[l], ssm_log_dt[l], ssm_b_re[l], ssm_b_im[l])
    ws = _s5_weights(bb_re, bb_im, ssm_c_re[l], ssm_c_im[l], ssm_d[l], ssm_w_glu[l], ssm_b_glu[l], ssm_w_out[l])
    y_ssm = _s5(u, lb_re, lb_im, ws, _pick(L, 64))

    kc, vct = _compress(kvc, cmp_pos[l].astype(F32).reshape(2, CMP_BLOCK // CMP_QUAD, CMP_QUAD * HEAD_DIM),
                        cmp_w1[l].astype(BF16),
                        cmp_b1[l].astype(F32).reshape(2, 1, CMP_HIDDEN),
                        cmp_w2[l, 0].astype(BF16), cmp_w2[l, 1].T.astype(BF16))
    o_t = _nsa_attn(qt, kc, vct, ks, vst, kw, vwt, gt, att_tile)

    h = _mix_ln1(h, y_ssm, o_t, wi["wgm"], nsa_w_out[l].astype(BF16), w_out[l].astype(BF16),
                 ln1_g[l], ln1_b[l], _pick(L, MIX_ROW_TILE))
    mk, mv = _mem_kv(mem, xattn_w_kv[l].astype(BF16))
    h = _xattn_ln2(h, mk, mv, xattn_w_q[l].astype(BF16), xattn_w_o[l].astype(BF16), ln2_g[l], ln2_b[l],
                   _pick(L, 2 * ROW_TILE))
    h = _ffn_ln3(h.reshape(bsz * L, d), ffn_w_in[l].astype(BF16), ffn_w_out[l].astype(BF16),
                 ln3_g[l], ln3_b[l], row_tile)
    return h.reshape(bsz, L, d)
```

```python
import functools
import math

import jax
import jax.numpy as jnp
from jax import lax
from jax.experimental import pallas as pl
from jax.experimental.pallas import tpu as pltpu

F32 = jnp.float32
BF16 = jnp.bfloat16

D_MODEL = 1024
SSM_WIDTH = 512
SSM_GROUP = 16
SSM_GROUPS = SSM_WIDTH // SSM_GROUP
SSM_STATE = 64
SSM_STATES = SSM_GROUPS * SSM_STATE
SSM_EIG_CLIP = -1e-4
N_HEADS = 16
N_KV_HEADS = 4
HEAD_DIM = 64
GQA_GROUP = N_HEADS // N_KV_HEADS
KV_DIM = N_KV_HEADS * HEAD_DIM
CMP_BLOCK = 32
CMP_STRIDE = 16
CMP_HIDDEN = 256
CMP_QUAD = 4
SLC_BLOCK = 64
SLC_TOP_N = 8
WINDOW = 512
SEL_BIG = 1e9
X_HEADS = 4
X_HEAD_DIM = D_MODEL // X_HEADS
D_FF = 2816
DEEPNORM_ALPHA = 2.0 ** 0.25
LN_EPS = 1e-5
NEG_BIG = -1e30
GATE_ROWS = 16
LOG2E = 1.4426950408889634
ATT_TILE = 256
ROW_TILE = 512
XATTN_ROW_TILE = 1024
ROW_STREAMS = 2
S5_STEPS = 64
MEM_BATCHES = 4
KEY_LANES = 128
ALIBI_LANE = HEAD_DIM
ONEHOT_LANE = HEAD_DIM + 8
VT_ROWS = 80
SLC_LEAD_CHUNKS = 3
HEADS_IN_FLIGHT = 2
HEAD_PHASE_LAG = 2
CHUNK_ROWS = 16

V7X_VMEM_LIMIT_BYTES = 56 * 1024 * 1024
SSM_LANE_SLAB = 128
SSM_SLABS = SSM_WIDTH // SSM_LANE_SLAB
SSM_STATE_SLAB = SSM_STATES // SSM_SLABS


def _cparams(*sem):
    return pltpu.CompilerParams(dimension_semantics=sem, vmem_limit_bytes=V7X_VMEM_LIMIT_BYTES)


def _full(shape):
    zeros = (0,) * len(shape)
    return pl.BlockSpec(shape, lambda *_: zeros, pipeline_mode=pl.Buffered(1))


def _layer_norm(x, g, b):
    mu = jnp.mean(x, axis=-1, keepdims=True)
    xc = x - mu
    var = jnp.mean(xc * xc, axis=-1, keepdims=True)
    return xc * lax.rsqrt(var + LN_EPS) * g + b


def _gelu_tanh(x):
    return 0.5 * x * (1.0 + jnp.tanh(math.sqrt(2.0 / math.pi) * (x + 0.044715 * (x * x * x))))


def _sigmoid(x):
    return 1.0 / (1.0 + jnp.exp(-x))


def _dot(a, b):
    return jnp.dot(a, b, preferred_element_type=F32)


def _dot_nt(a, b):
    return lax.dot_general(a, b, (((1,), (1,)), ((), ())), preferred_element_type=F32)


def _zoh_kernel(a_re, a_im, log_dt, b_re, b_im, lb_re_o, lb_im_o, bb_re_o, bb_im_o):
    lam_re = jnp.minimum(a_re[...], SSM_EIG_CLIP)
    lam_im = a_im[...]
    dt = jnp.exp(log_dt[...])
    mag = jnp.exp(lam_re * dt)
    lb_re = mag * jnp.cos(lam_im * dt)
    lb_im = mag * jnp.sin(lam_im * dt)
    den = lam_re * lam_re + lam_im * lam_im
    nr = lb_re - 1.0
    f_re = (nr * lam_re + lb_im * lam_im) / den
    f_im = (lb_im * lam_re - nr * lam_im) / den
    br = b_re[...]
    bi = b_im[...]
    lb_re_o[...] = lb_re
    lb_im_o[...] = lb_im
    bb_re_o[...] = f_re * br - f_im * bi
    bb_im_o[...] = f_re * bi + f_im * br


def _zoh_prep(a_re, a_im, log_dt, b_re, b_im):
    gn = SSM_STATES
    row = lambda v: v.astype(F32).reshape(1, gn)
    dt_row = jnp.broadcast_to(log_dt.astype(F32)[:, None], (SSM_GROUPS, SSM_STATE)).reshape(1, gn)
    chan = lambda v: v.astype(F32).reshape(gn, SSM_GROUP).T
    return pl.pallas_call(
        _zoh_kernel,
        out_shape=[jax.ShapeDtypeStruct((1, gn), F32)] * 2 + [jax.ShapeDtypeStruct((SSM_GROUP, gn), F32)] * 2,
        name="zoh_prep",
    )(row(a_re), row(a_im), dt_row, chan(b_re), chan(b_im))


def _inproj_kernel(x_ref, g_ref, b_ref, wu_ref, wqt_ref, wk_ref, wvst_ref, wvwt_ref, wgt_ref,
                   h_o, u_o, qt_o, kvc_o, ks_o, kw_o, vst_o, vwt_o, gt_o):
    tl = x_ref.shape[1]
    stream = tl // ROW_STREAMS
    for r in range(ROW_STREAMS):
        _inproj_rows(slice(r * stream, (r + 1) * stream), pl.program_id(1) * tl + r * stream,
                     x_ref, g_ref, b_ref, wu_ref, wqt_ref, wk_ref, wvst_ref, wvwt_ref, wgt_ref,
                     h_o, u_o, qt_o, kvc_o, ks_o, kw_o, vst_o, vwt_o, gt_o)


def _inproj_rows(rows, first_pos, x_ref, g_ref, b_ref, wu_ref, wqt_ref, wk_ref, wvst_ref, wvwt_ref, wgt_ref,
                 h_o, u_o, qt_o, kvc_o, ks_o, kw_o, vst_o, vwt_o, gt_o):
    tl = rows.stop - rows.start
    h = _layer_norm(x_ref[0, rows, :], g_ref[...], b_ref[...])
    h_o[0, rows, :] = h
    hb = h.astype(BF16)
    u_o[0, rows, :] = _dot(hb, wu_ref[...])
    qt_o[0, :, rows] = (_dot_nt(wqt_ref[...], hb) * (HEAD_DIM ** -0.5 * LOG2E)).astype(BF16)

    k_all = _dot(hb, wk_ref[...])

    def head_tile(j):
        tile = k_all[:, (j // 2) * KEY_LANES:(j // 2 + 1) * KEY_LANES]
        return pltpu.roll(tile, HEAD_DIM, 1) if j % 2 else tile

    for j in range(N_KV_HEADS):
        kvc_o[0, j, rows, :] = k_all[:, j * KEY_LANES:(j + 1) * KEY_LANES]

    lane = lax.broadcasted_iota(jnp.int32, (tl, KEY_LANES), 1)
    pos = first_pos + lax.broadcasted_iota(jnp.int32, (tl, KEY_LANES), 0)
    alibi = jnp.where((lane >= ALIBI_LANE) & (lane < ALIBI_LANE + 3), (pos % ATT_TILE).astype(F32), 0.0)
    slc_feat = alibi + jnp.where((lane >= ONEHOT_LANE) & (lane - ONEHOT_LANE == pos // SLC_BLOCK), 1.0, 0.0)
    is_key = lane < HEAD_DIM
    for j in range(N_KV_HEADS):
        ks_o[0, j, rows, :] = jnp.where(is_key, head_tile(2 * N_KV_HEADS + j), slc_feat).astype(BF16)
        kw_o[0, j, rows, :] = jnp.where(is_key, head_tile(3 * N_KV_HEADS + j), alibi).astype(BF16)
    row = lax.broadcasted_iota(jnp.int32, (N_KV_HEADS * VT_ROWS, tl), 0)
    ones_row = jnp.where(row % VT_ROWS == HEAD_DIM, 1.0, 0.0)
    vst_o[0, :, rows] = (_dot_nt(wvst_ref[...], hb) + ones_row).astype(BF16)
    vwt_o[0, :, rows] = (_dot_nt(wvwt_ref[...], hb) + ones_row).astype(BF16)
    gt_o[0, :, rows] = _sigmoid(_dot_nt(wgt_ref[...], hb))


def _ln_inproj(x, ln_g, ln_b, w, tl):
    bsz, L, d = x.shape
    n_gate = N_KV_HEADS * GATE_ROWS
    grid = (bsz, L // tl)
    in_specs = [
        pl.BlockSpec((1, tl, d), lambda b, i: (b, i, 0)),
        _full((1, d)), _full((1, d)),
        _full((d, SSM_WIDTH)),
        _full((D_MODEL, d)),
        _full((d, 4 * KV_DIM)),
        _full((N_KV_HEADS * VT_ROWS, d)),
        _full((N_KV_HEADS * VT_ROWS, d)),
        _full((n_gate, d)),
    ]
    assert tl % ATT_TILE == 0 or ATT_TILE % tl == 0
    assert ONEHOT_LANE + L // SLC_BLOCK <= KEY_LANES
    out_shape = [
        jax.ShapeDtypeStruct((bsz, L, d), F32),
        jax.ShapeDtypeStruct((bsz, L, SSM_WIDTH), F32),
        jax.ShapeDtypeStruct((bsz, D_MODEL, L), BF16),
        jax.ShapeDtypeStruct((bsz, N_KV_HEADS, L, 2 * HEAD_DIM), F32),
        jax.ShapeDtypeStruct((bsz, N_KV_HEADS, L, KEY_LANES), BF16),
        jax.ShapeDtypeStruct((bsz, N_KV_HEADS, L, KEY_LANES), BF16),
        jax.ShapeDtypeStruct((bsz, N_KV_HEADS * VT_ROWS, L), BF16),
        jax.ShapeDtypeStruct((bsz, N_KV_HEADS * VT_ROWS, L), BF16),
        jax.ShapeDtypeStruct((bsz, n_gate, L), F32),
    ]
    out_specs = [
        pl.BlockSpec((1, tl, d), lambda b, i: (b, i, 0)),
        pl.BlockSpec((1, tl, SSM_WIDTH), lambda b, i: (b, i, 0)),
        pl.BlockSpec((1, D_MODEL, tl), lambda b, i: (b, 0, i)),
        pl.BlockSpec((1, N_KV_HEADS, tl, 2 * HEAD_DIM), lambda b, i: (b, 0, i, 0)),
        pl.BlockSpec((1, N_KV_HEADS, tl, KEY_LANES), lambda b, i: (b, 0, i, 0)),
        pl.BlockSpec((1, N_KV_HEADS, tl, KEY_LANES), lambda b, i: (b, 0, i, 0)),
        pl.BlockSpec((1, N_KV_HEADS * VT_ROWS, tl), lambda b, i: (b, 0, i)),
        pl.BlockSpec((1, N_KV_HEADS * VT_ROWS, tl), lambda b, i: (b, 0, i)),
        pl.BlockSpec((1, n_gate, tl), lambda b, i: (b, 0, i)),
    ]
    return pl.pallas_call(
        _inproj_kernel, grid=grid, in_specs=in_specs, out_specs=out_specs, out_shape=out_shape,
        compiler_params=_cparams("parallel", "parallel"), name="ln_inproj",
    )(x, ln_g.reshape(1, d), ln_b.reshape(1, d), w["wu"], w["wqt"], w["wk"],
      w["wvst"], w["wvwt"], w["wgt"])


def _s5_kernel(u_ref, lre_ref, lim_ref, wbre_ref, wbim_ref, cre_ref, cim_ref, d_ref, wglu_ref, bglu_ref, wo_ref,
               y_o, sre, sim, hre, him, *, bsz, steps, pitch):
    @pl.when(pl.program_id(0) == 0)
    def _():
        sre[...] = jnp.zeros_like(sre)
        sim[...] = jnp.zeros_like(sim)

    lanes = SSM_LANE_SLAB
    per_slab = SSM_STATE_SLAB // lanes
    u = u_ref[...].reshape(bsz * steps, SSM_WIDTH)
    ub = u.astype(BF16)

    def project_in(j):
        uj = ub[:, j * lanes:(j + 1) * lanes]
        for w_ref, h_ref in ((wbre_ref, hre), (wbim_ref, him)):
            r = _dot(uj, w_ref[j])
            for b in range(bsz):
                for k in range(per_slab):
                    h_ref[j * per_slab + k, pl.ds(b, steps, stride=pitch), :] = (
                        r[b * steps:(b + 1) * steps, k * lanes:(k + 1) * lanes])

    def recur(j):
        for s in range(j * per_slab, (j + 1) * per_slab):
            cols = slice(s * lanes, (s + 1) * lanes)
            lr = jnp.broadcast_to(lre_ref[:, cols], (bsz, lanes))
            li = jnp.broadcast_to(lim_ref[:, cols], (bsz, lanes))
            pr, pi = sre[:, cols], sim[:, cols]
            for t in range(steps):
                rows = pl.ds(t * pitch, bsz)
                pr, pi = (lr * pr - li * pi + hre[s, rows, :], lr * pi + li * pr + him[s, rows, :])
                hre[s, rows, :] = pr
                him[s, rows, :] = pi
            sre[:, cols] = pr
            sim[:, cols] = pi

    def states(h_ref, j):
        return jnp.concatenate(
            [jnp.concatenate([h_ref[j * per_slab + k, pl.ds(b, steps, stride=pitch), :] for k in range(per_slab)],
                             axis=-1) for b in range(bsz)], axis=0)

    ys = [None] * SSM_SLABS

    def project_out(j):
        ys[j] = _dot(states(hre, j).astype(BF16), cre_ref[j]) - _dot(states(him, j).astype(BF16), cim_ref[j])

    for j in range(SSM_SLABS + 2):
        if j < SSM_SLABS:
            project_in(j)
        if 0 <= j - 1 < SSM_SLABS:
            recur(j - 1)
        if 0 <= j - 2 < SSM_SLABS:
            project_out(j - 2)
    y = jnp.concatenate(ys, axis=-1) + d_ref[...] * u
    g = _gelu_tanh(y)
    y2 = g * _sigmoid(_dot(g.astype(BF16), wglu_ref[...]) + bglu_ref[...])
    y_o[...] = _dot(y2.astype(BF16), wo_ref[...]).reshape(bsz, steps, D_MODEL)


def _s5(u, lb_re, lb_im, w, steps):
    bsz, L, _ = u.shape
    pitch = -(-bsz // 4) * 4
    pitch += 4 if (pitch // 4) % 2 == 0 else 0
    grid = (L // steps,)
    kern = functools.partial(_s5_kernel, bsz=bsz, steps=steps, pitch=pitch)
    n_slabs = SSM_STATES // SSM_LANE_SLAB
    in_specs = [
        pl.BlockSpec((bsz, steps, SSM_WIDTH), lambda c: (0, c, 0)),
        _full((1, SSM_STATES)), _full((1, SSM_STATES)),
        _full((SSM_SLABS, SSM_LANE_SLAB, SSM_STATE_SLAB)), _full((SSM_SLABS, SSM_LANE_SLAB, SSM_STATE_SLAB)),
        _full((SSM_SLABS, SSM_STATE_SLAB, SSM_LANE_SLAB)), _full((SSM_SLABS, SSM_STATE_SLAB, SSM_LANE_SLAB)),
        _full((1, SSM_WIDTH)),
        _full((SSM_WIDTH, SSM_WIDTH)), _full((1, SSM_WIDTH)),
        _full((SSM_WIDTH, D_MODEL)),
    ]
    return pl.pallas_call(
        kern, grid=grid, in_specs=in_specs,
        out_specs=pl.BlockSpec((bsz, steps, D_MODEL), lambda c: (0, c, 0)),
        out_shape=jax.ShapeDtypeStruct((bsz, L, D_MODEL), F32),
        scratch_shapes=[pltpu.VMEM((bsz, SSM_STATES), F32), pltpu.VMEM((bsz, SSM_STATES), F32),
                        pltpu.VMEM((n_slabs, steps * pitch, SSM_LANE_SLAB), F32),
                        pltpu.VMEM((n_slabs, steps * pitch, SSM_LANE_SLAB), F32)],
        compiler_params=_cparams("arbitrary"), name="s5",
    )(u, lb_re, lb_im, w["wbre"], w["wbim"], w["cre"], w["cim"], w["d"], w["wglu"], w["bglu"], w["wo"])


def _compress_kernel(kv_ref, pos_ref, w1_ref, b1_ref, w2k_ref, w2vt_ref, kc_o, vct_o, *, n_chunks):
    half = CMP_BLOCK // 2
    quad = CMP_QUAD
    low_half = lax.broadcasted_iota(jnp.int32, (n_chunks, 2 * HEAD_DIM), 1) < HEAD_DIM
    for tile in range(2 * N_KV_HEADS // 2):
        z = tile // (N_KV_HEADS // 2)
        heads = (2 * (tile % (N_KV_HEADS // 2)), 2 * (tile % (N_KV_HEADS // 2)) + 1)
        first = [jnp.zeros((n_chunks, CMP_HIDDEN), F32) for _ in heads]
        second = [jnp.zeros((n_chunks, CMP_HIDDEN), F32) for _ in heads]
        for q in range(half // quad):
            rows = [kv_ref[0, tile, pl.ds(q * quad + r, n_chunks, stride=CMP_STRIDE), :] for r in range(quad)]
            turned = [pltpu.roll(x, HEAD_DIM, 1) for x in rows]
            for side in range(2):
                pairs = [jnp.where(low_half, rows[r], turned[r + 1]) if side == 0 else
                         jnp.where(low_half, turned[r], rows[r + 1]) for r in range(0, quad, 2)]
                x4 = jnp.concatenate(pairs, axis=-1)
                for part, acc in ((0, first), (1, second)):
                    qq = part * (half // quad) + q
                    w = w1_ref[z, qq * quad * HEAD_DIM:(qq + 1) * quad * HEAD_DIM, :]
                    acc[side] += _dot((x4 + pos_ref[z, qq:qq + 1, :]).astype(BF16), w)
        for side, hh in enumerate(heads):
            pre = first[side] + pltpu.roll(second[side], n_chunks - 1, 0) + b1_ref[z]
            hid = _gelu_tanh(pre).astype(BF16)
            if z == 0:
                kc_o[0, hh] = _dot(hid, w2k_ref[...]).astype(BF16)
            else:
                vct_o[0, hh * HEAD_DIM:(hh + 1) * HEAD_DIM, :] = _dot_nt(w2vt_ref[...], hid).astype(BF16)


def _compress(kvc, pos, w1, b1, w2k, w2vt):
    bsz, _, L, _ = kvc.shape
    n_chunks = L // CMP_STRIDE
    kern = functools.partial(_compress_kernel, n_chunks=n_chunks)
    return pl.pallas_call(
        kern, grid=(bsz,),
        in_specs=[
            pl.BlockSpec((1, N_KV_HEADS, L, 2 * HEAD_DIM), lambda b: (b, 0, 0, 0)),
            _full((2, CMP_BLOCK // CMP_QUAD, CMP_QUAD * HEAD_DIM)),
            _full((2, CMP_BLOCK * HEAD_DIM, CMP_HIDDEN)),
            _full((2, 1, CMP_HIDDEN)),
            _full((CMP_HIDDEN, HEAD_DIM)),
            _full((HEAD_DIM, CMP_HIDDEN)),
        ],
        out_specs=[
            pl.BlockSpec((1, N_KV_HEADS, n_chunks, HEAD_DIM), lambda b: (b, 0, 0, 0)),
            pl.BlockSpec((1, KV_DIM, n_chunks), lambda b: (b, 0, 0)),
        ],
        out_shape=[
            jax.ShapeDtypeStruct((bsz, N_KV_HEADS, n_chunks, HEAD_DIM), BF16),
            jax.ShapeDtypeStruct((bsz, KV_DIM, n_chunks), BF16),
        ],
        compiler_params=_cparams("parallel"), name="compress",
    )(kvc, pos, w1, b1, w2k, w2vt)


def _nsa_kernel(*refs, tq, n_cmp_pad, n_slc):
    def head_pair(pair, carry):
        heads = [_nsa_head(HEADS_IN_FLIGHT * pair + slot, slot, *refs, tq=tq, n_cmp_pad=n_cmp_pad, n_slc=n_slc)
                 for slot in range(HEADS_IN_FLIGHT)]
        tails, live, step = {}, set(range(HEADS_IN_FLIGHT)), 0
        while live:
            for slot in sorted(live):
                if step >= slot * HEAD_PHASE_LAG:
                    try:
                        out = next(heads[slot])
                        if out is not None:
                            tails[slot] = out
                    except StopIteration:
                        live.discard(slot)
            step += 1
        for slot in range(HEADS_IN_FLIGHT):
            tails[slot]()
        return carry

    lax.fori_loop(0, N_KV_HEADS // HEADS_IN_FLIGHT, head_pair, 0)


def _nsa_head(hkv, par, qt_ref, kc_ref, vct_ref, ks_ref, vst_ref, kw_ref, vwt_ref, gt_ref, ov_ref, causal_ref,
              far_ref, chunk_ref, o_ref, qa_ref, qw_ref, m_ref, acc_ref, s_ref, mx_ref, al_ref, sc_ref, pc_ref,
              need_ref, *, tq, n_cmp_pad, n_slc):
    i = pl.program_id(1)
    slots_per_head = s_ref.shape[0] // HEADS_IN_FLIGHT
    t0 = i * tq
    tk = tq
    n_sel = min(SLC_TOP_N, n_slc)
    slopes = [jnp.exp2(jnp.full((1, tq), -0.5, F32) * jnp.asarray(hkv * GQA_GROUP + g + 1, F32)) * LOG2E
              for g in range(GQA_GROUP)]
    head_rows = lambda g: pl.ds(pl.multiple_of((hkv * GQA_GROUP + g) * HEAD_DIM, HEAD_DIM), HEAD_DIM)
    vt_rows = pl.ds(pl.multiple_of(hkv * VT_ROWS, 16), VT_ROWS)
    q_heads = [qt_ref[0, head_rows(g), :] for g in range(GQA_GROUP)]
    gate = lambda g, z: gt_ref[0, pl.ds(hkv * GATE_ROWS + 3 * g + z, 1), :]

    t_lane = t0 + lax.broadcasted_iota(jnp.int32, (1, tq), 1)
    row8 = lax.broadcasted_iota(jnp.int32, (ONEHOT_LANE - ALIBI_LANE, tq), 0)

    def augmented(g, tail_rows):
        hi = slopes[g].astype(BF16).astype(F32)
        mid = (slopes[g] - hi).astype(BF16).astype(F32)
        lo = slopes[g] - hi - mid
        parts = jnp.where(row8 == 0, hi, jnp.where(row8 == 1, mid, jnp.where(row8 == 2, lo, 0.0)))
        return jnp.concatenate([q_heads[g].astype(F32), parts, tail_rows], axis=0).astype(BF16)

    def attend(state, q_ref, items, first, slot0):
        starts = [pl.multiple_of((c if valid is None else jnp.maximum(c, 0)) * tk, tk) for _, _, c, _, valid in items]
        slot0 = slot0 + par * slots_per_head

        def scores(g):
            for n, (k_ref, _, _, bias_ref, _) in enumerate(items):
                s = _dot(k_ref[0, hkv, pl.ds(starts[n], tk), :], q_ref[par, g])
                if bias_ref is not None:
                    s = s + bias_ref[...]
                s_ref[slot0 + n * GQA_GROUP + g] = s
                mx_ref[slot0 + n * GQA_GROUP + g] = jnp.max(s, axis=0, keepdims=True)

        def fold(g):
            shifts = []
            for _, _, c, _, valid in items:
                shift = slopes[g] * jnp.asarray((c - i) * tk, F32)
                shifts.append(shift if valid is None else jnp.where(valid, shift, NEG_BIG))
            m_new = functools.reduce(jnp.maximum, [mx_ref[slot0 + n * GQA_GROUP + g] + shifts[n]
                                                   for n in range(len(items))])
            if not first:
                m_new = jnp.maximum(m_ref[par, state, g], m_new)
                al_ref[par, g] = jnp.exp2(m_ref[par, state, g] - m_new)
            pv = None
            for n, (_, vt_ref, _, _, _) in enumerate(items):
                slot = slot0 + n * GQA_GROUP + g
                p = jnp.exp2(s_ref[slot] - (m_new - shifts[n])).astype(BF16)
                part = _dot(vt_ref[0, vt_rows, pl.ds(starts[n], tk)], p)
                pv = part if pv is None else pv + part
            acc_ref[par, state, g] = pv if first else al_ref[par, g] * acc_ref[par, state, g] + pv
            m_ref[par, state, g] = m_new

        scores(0)
        for g in range(GQA_GROUP):
            if g + 1 < GQA_GROUP:
                scores(g + 1)
            fold(g)
            yield

    def emit(state, z, accumulate):
        for g in range(GQA_GROUP):
            scale = gate(g, z) / acc_ref[par, state, g, HEAD_DIM:HEAD_DIM + 1, :]
            out = acc_ref[par, state, g, :HEAD_DIM, :] * scale
            o_ref[0, head_rows(g), :] = o_ref[0, head_rows(g), :] + out if accumulate else out

    n_idx = lax.broadcasted_iota(jnp.int32, (n_cmp_pad, tq), 0)
    dist_c = (t0 + lax.broadcasted_iota(jnp.int32, (n_cmp_pad, tq), 1)) - (n_idx * CMP_STRIDE + (CMP_BLOCK - 1))
    mask_c = dist_c >= 0
    dist_cf = dist_c.astype(F32)
    kc = kc_ref[0, hkv]
    vct = vct_ref[0, pl.ds(pl.multiple_of(hkv * HEAD_DIM, HEAD_DIM), HEAD_DIM), :]
    for g in range(GQA_GROUP):
        s = _dot(kc, q_heads[g]) - slopes[g] * dist_cf
        sc_ref[par, g] = jnp.where(mask_c, s, NEG_BIG)
    yield
    p_sum = jnp.zeros((n_cmp_pad, tq), F32)
    for g in range(GQA_GROUP):
        s = sc_ref[par, g]
        m = jnp.max(s, axis=0, keepdims=True)
        e = jnp.exp2(s - m)
        norm = 1.0 / jnp.maximum(jnp.sum(e, axis=0, keepdims=True), 1e-30)
        p = e * jnp.where(m > 0.5 * NEG_BIG, norm, 0.0)
        pc_ref[par, g] = p.astype(BF16)
        p_sum = p_sum + p
    for g in range(GQA_GROUP):
        o_ref[0, head_rows(g), :] = gate(g, 0) * _dot(vct, pc_ref[par, g])

    imp = lax.dot_general(ov_ref[...], p_sum, (((1,), (0,)), ((), ())), precision=lax.Precision.HIGHEST,
                          preferred_element_type=F32)
    yield
    blk = lax.broadcasted_iota(jnp.int32, (n_slc, tq), 0)
    cur = t_lane // SLC_BLOCK
    forced = (blk == 0) | (blk == cur) | (blk == cur - 1)
    future = blk * SLC_BLOCK > t_lane
    imp = jnp.where(forced, SEL_BIG, jnp.where(future, -SEL_BIG, imp))
    sub8 = 8
    tiles = [imp[r:r + sub8] for r in range(0, n_slc, sub8)]
    ranks = [jnp.zeros((sub8, tq), jnp.int32) for _ in tiles]
    for j in range(n_slc):
        row = imp[j:j + 1, :]
        for r, tile in enumerate(tiles):
            if r * sub8 > j:
                ahead = row >= tile
            elif r * sub8 + sub8 - 1 < j:
                ahead = row > tile
            else:
                later = r * sub8 + lax.broadcasted_iota(jnp.int32, (sub8, tq), 0) > j
                ahead = (row > tile) | ((row == tile) & later)
            ranks[r] = ranks[r] + ahead.astype(jnp.int32)
    rank = jnp.concatenate(ranks, axis=0)
    sel_bias = jnp.where(rank < n_sel, 0.0, NEG_BIG)

    qa_rows = jnp.concatenate([sel_bias, jnp.zeros((KEY_LANES - ONEHOT_LANE - n_slc, tq), F32)], axis=0)
    for g in range(GQA_GROUP):
        qa_ref[par, g] = augmented(g, qa_rows)
    yield

    zero_rows = jnp.zeros((KEY_LANES - ONEHOT_LANE, tq), F32)
    for g in range(GQA_GROUP):
        qw_ref[par, g] = augmented(g, zero_rows)
    n_back = WINDOW // tk
    n_win_items = n_back + 1
    win_items = [(kw_ref, vwt_ref, i, causal_ref, None)]
    for back in range(1, n_back + 1):
        win_items.append((kw_ref, vwt_ref, i - back, far_ref if back == n_back else None, i >= back))
    yield from attend(0, qw_ref, win_items, True, 0)
    emit(0, 2, True)
    yield

    slc_items = [(ks_ref, vst_ref, i, causal_ref, None)]
    for back in range(1, SLC_LEAD_CHUNKS):
        slc_items.append((ks_ref, vst_ref, i - back, None, i >= back))
    slc_items.append((ks_ref, vst_ref, 0, None, i >= SLC_LEAD_CHUNKS))
    yield from attend(1, qa_ref, slc_items, True, n_win_items * GQA_GROUP)

    picked = (rank < n_sel).astype(BF16)
    per_chunk = jnp.sum(_dot(chunk_ref[...], picked), axis=1, keepdims=True)
    need0 = par * CHUNK_ROWS
    for c in range(n_slc * SLC_BLOCK // tk):
        need_ref[need0 + c] = (per_chunk[c, 0] > 0.0).astype(jnp.int32)
    last = i - SLC_LEAD_CHUNKS

    def slc_step(pair, carry):
        c0 = 1 + 2 * pair
        c1 = jnp.minimum(c0 + 1, last)
        use0 = need_ref[need0 + c0] > 0
        use1 = (c0 + 1 <= last) & (need_ref[need0 + c1] > 0)

        @pl.when(use0 | use1)
        def _():
            for _ in attend(1, qa_ref, [(ks_ref, vst_ref, c0, None, use0), (ks_ref, vst_ref, c1, None, use1)], False, 0):
                pass
        return carry

    def tail():
        lax.fori_loop(0, jnp.maximum(last + 1, 0) // 2, slc_step, 0)
        emit(1, 1, True)

    yield tail


def _nsa_attn(qt, kc, vct, ks, vst, kw, vwt, gt, tq):
    bsz, _, L = qt.shape
    n_cmp_pad = kc.shape[2]
    n_slc = L // SLC_BLOCK
    n = jnp.arange(n_cmp_pad)[None, :]
    j = jnp.arange(n_slc)[:, None]
    overlap_t = ((n * CMP_STRIDE < (j + 1) * SLC_BLOCK) & (n * CMP_STRIDE + CMP_BLOCK - 1 >= j * SLC_BLOCK)).astype(F32)
    assert tq == ATT_TILE and WINDOW % tq == 0 and n_slc % 8 == 0
    n_slots = (WINDOW // tq + 1 + SLC_LEAD_CHUNKS + 1) * GQA_GROUP
    sub = jnp.arange(tq)[:, None]
    lane = jnp.arange(tq)[None, :]
    causal_bias = jnp.where(sub <= lane, 0.0, NEG_BIG).astype(F32)
    far_bias = jnp.where(sub > lane, 0.0, NEG_BIG).astype(F32)
    assert L // tq <= CHUNK_ROWS
    chunk_of_block = (jnp.arange(CHUNK_ROWS)[:, None] == jnp.arange(n_slc)[None, :] * SLC_BLOCK // tq).astype(BF16)
    kern = functools.partial(_nsa_kernel, tq=tq, n_cmp_pad=n_cmp_pad, n_slc=n_slc)
    return pl.pallas_call(
        kern, grid=(bsz, L // tq),
        in_specs=[
            pl.BlockSpec((1, D_MODEL, tq), lambda b, i: (b, 0, i)),
            pl.BlockSpec((1, N_KV_HEADS, n_cmp_pad, HEAD_DIM), lambda b, i: (b, 0, 0, 0)),
            pl.BlockSpec((1, KV_DIM, n_cmp_pad), lambda b, i: (b, 0, 0)),
            pl.BlockSpec((1, N_KV_HEADS, L, KEY_LANES), lambda b, i: (b, 0, 0, 0)),
            pl.BlockSpec((1, N_KV_HEADS * VT_ROWS, L), lambda b, i: (b, 0, 0)),
            pl.BlockSpec((1, N_KV_HEADS, L, KEY_LANES), lambda b, i: (b, 0, 0, 0)),
            pl.BlockSpec((1, N_KV_HEADS * VT_ROWS, L), lambda b, i: (b, 0, 0)),
            pl.BlockSpec((1, N_KV_HEADS * GATE_ROWS, tq), lambda b, i: (b, 0, i)),
            _full((n_slc, n_cmp_pad)),
            _full((tq, tq)), _full((tq, tq)),
            _full((CHUNK_ROWS, n_slc)),
        ],
        out_specs=pl.BlockSpec((1, D_MODEL, tq), lambda b, i: (b, 0, i)),
        out_shape=jax.ShapeDtypeStruct((bsz, D_MODEL, L), F32),
        scratch_shapes=[pltpu.VMEM((HEADS_IN_FLIGHT, GQA_GROUP, KEY_LANES, tq), BF16),
                        pltpu.VMEM((HEADS_IN_FLIGHT, GQA_GROUP, KEY_LANES, tq), BF16),
                        pltpu.VMEM((HEADS_IN_FLIGHT, 2, GQA_GROUP, 1, tq), F32),
                        pltpu.VMEM((HEADS_IN_FLIGHT, 2, GQA_GROUP, VT_ROWS, tq), F32),
                        pltpu.VMEM((HEADS_IN_FLIGHT * n_slots, tq, tq), F32),
                        pltpu.VMEM((HEADS_IN_FLIGHT * n_slots, 1, tq), F32),
                        pltpu.VMEM((HEADS_IN_FLIGHT, GQA_GROUP, 1, tq), F32),
                        pltpu.VMEM((HEADS_IN_FLIGHT, GQA_GROUP, n_cmp_pad, tq), F32),
                        pltpu.VMEM((HEADS_IN_FLIGHT, GQA_GROUP, n_cmp_pad, tq), BF16),
                        pltpu.SMEM((HEADS_IN_FLIGHT * CHUNK_ROWS,), jnp.int32)],
        compiler_params=_cparams("parallel", "parallel"), name="nsa_attn",
    )(qt, kc, vct, ks, vst, kw, vwt, gt, overlap_t, causal_bias, far_bias, chunk_of_block)


def _mix_kernel(h_ref, ys_ref, ot_ref, wgm_ref, wno_ref, wout_ref, g_ref, b_ref, h1_o):
    stream = h_ref.shape[1] // ROW_STREAMS
    for r in range(ROW_STREAMS):
        rows = slice(r * stream, (r + 1) * stream)
        h = h_ref[0, rows, :]
        o = ot_ref[0, :, rows].T.astype(BF16)
        y_nsa = _dot(o, wno_ref[...])
        gm = _sigmoid(_dot(h.astype(BF16), wgm_ref[...]))
        mix = gm[:, :D_MODEL] * ys_ref[0, rows, :] + gm[:, D_MODEL:] * y_nsa
        mixed = _dot(mix.astype(BF16), wout_ref[...])
        h1_o[0, rows, :] = _layer_norm(DEEPNORM_ALPHA * h + mixed, g_ref[...], b_ref[...])


def _mix_ln1(h, y_ssm, o_t, wgm, wno, wout, g, b, tm):
    bsz, L, d = h.shape
    return pl.pallas_call(
        _mix_kernel, grid=(bsz, L // tm),
        in_specs=[
            pl.BlockSpec((1, tm, d), lambda b, i: (b, i, 0)),
            pl.BlockSpec((1, tm, d), lambda b, i: (b, i, 0)),
            pl.BlockSpec((1, d, tm), lambda b, i: (b, 0, i)),
            _full((d, 2 * d)), _full((d, d)), _full((d, d)), _full((1, d)), _full((1, d)),
        ],
        out_specs=pl.BlockSpec((1, tm, d), lambda b, i: (b, i, 0)),
        out_shape=jax.ShapeDtypeStruct((bsz, L, d), F32),
        compiler_params=_cparams("parallel", "parallel"), name="mix_ln1",
    )(h, y_ssm, o_t, wgm, wno, wout, g.reshape(1, d), b.reshape(1, d))


def _memkv_kernel(mem_ref, w_ref, k_o, v_o):
    nb, m, d = mem_ref.shape
    kv = _dot(mem_ref[...].reshape(nb * m, d).astype(BF16), w_ref[...])
    k_o[...] = kv[:, :D_MODEL].astype(BF16).reshape(nb, m, d)
    v_o[...] = kv[:, D_MODEL:].astype(BF16).reshape(nb, m, d)


def _mem_kv(mem, w_kv):
    bsz, m, d = mem.shape
    nb = _pick(bsz, MEM_BATCHES)
    spec = pl.BlockSpec((nb, m, d), lambda b: (b, 0, 0))
    return pl.pallas_call(
        _memkv_kernel, grid=(bsz // nb,),
        in_specs=[spec, _full((d, 2 * d))],
        out_specs=[spec, spec],
        out_shape=[jax.ShapeDtypeStruct((bsz, m, d), BF16)] * 2,
        compiler_params=_cparams("parallel"), name="mem_kv",
    )(mem, w_kv)


def _xattn_kernel(h_ref, k_ref, v_ref, wq_ref, wo_ref, g_ref, b_ref, h2_o):
    h = h_ref[0]
    q = (_dot(h.astype(BF16), wq_ref[...]) * (X_HEAD_DIM ** -0.5 * LOG2E)).astype(BF16)
    cols = [slice(hd * X_HEAD_DIM, (hd + 1) * X_HEAD_DIM) for hd in range(X_HEADS)]
    scores = lambda hd: _dot_nt(q[:, cols[hd]], k_ref[0, :, cols[hd]])
    outs = []
    s_next = scores(0)
    for hd in range(X_HEADS):
        s = s_next
        if hd + 1 < X_HEADS:
            s_next = scores(hd + 1)
        p = jnp.exp2(s - jnp.max(s, axis=-1, keepdims=True))
        norm = 1.0 / jnp.sum(p, axis=-1, keepdims=True)
        outs.append(_dot(p.astype(BF16), v_ref[0, :, cols[hd]]) * norm)
    o = jnp.concatenate(outs, axis=-1).astype(BF16)
    h2_o[0] = _layer_norm(DEEPNORM_ALPHA * h + _dot(o, wo_ref[...]), g_ref[...], b_ref[...])


def _xattn_ln2(h, k, v, wq, wo, g, b, tm):
    bsz, L, d = h.shape
    m = k.shape[1]
    row = pl.BlockSpec((1, tm, d), lambda b, i: (b, i, 0))
    kv = pl.BlockSpec((1, m, d), lambda b, i: (b, 0, 0))
    return pl.pallas_call(
        _xattn_kernel, grid=(bsz, L // tm),
        in_specs=[row, kv, kv, _full((d, d)), _full((d, d)), _full((1, d)), _full((1, d))],
        out_specs=row,
        out_shape=jax.ShapeDtypeStruct((bsz, L, d), F32),
        compiler_params=_cparams("parallel", "parallel"), name="xattn_ln2",
    )(h, k, v, wq, wo, g.reshape(1, d), b.reshape(1, d))


def _ffn_kernel(h_ref, win_ref, wout_ref, g_ref, b_ref, o_ref):
    stream = h_ref.shape[0] // ROW_STREAMS
    for r in range(ROW_STREAMS):
        rows = slice(r * stream, (r + 1) * stream)
        h = h_ref[rows, :]
        gu = _dot(h.astype(BF16), win_ref[...])
        gate = gu[:, :D_FF]
        act = gate * _sigmoid(gate) * gu[:, D_FF:]
        o_ref[rows, :] = _layer_norm(DEEPNORM_ALPHA * h + _dot(act.astype(BF16), wout_ref[...]),
                                     g_ref[...], b_ref[...])


def _ffn_ln3(h, win, wout, g, b, tm):
    rows, d = h.shape
    row = pl.BlockSpec((tm, d), lambda i: (i, 0))
    return pl.pallas_call(
        _ffn_kernel, grid=(rows // tm,),
        in_specs=[row, _full((d, 2 * D_FF)), _full((D_FF, d)), _full((1, d)), _full((1, d))],
        out_specs=row,
        out_shape=jax.ShapeDtypeStruct((rows, d), F32),
        compiler_params=_cparams("parallel"), name="ffn_ln3",
    )(h, win, wout, g.reshape(1, d), b.reshape(1, d))


def _inproj_weights(w_in):
    d = D_MODEL
    c0 = SSM_WIDTH
    c1 = c0 + N_HEADS * HEAD_DIM
    c2 = c1 + 2 * KV_DIM
    c3 = c2 + 2 * KV_DIM
    c4 = c3 + 2 * KV_DIM
    c5 = c4 + 3 * N_HEADS
    values_t = lambda m: jnp.pad(m.T.reshape(N_KV_HEADS, HEAD_DIM, d),
                                 ((0, 0), (0, VT_ROWS - HEAD_DIM), (0, 0))).reshape(N_KV_HEADS * VT_ROWS, d)
    wg = w_in[:, c4:c5].T.reshape(N_KV_HEADS, 3 * GQA_GROUP, d)
    wg = jnp.pad(wg, ((0, 0), (0, GATE_ROWS - 3 * GQA_GROUP), (0, 0))).reshape(N_KV_HEADS * GATE_ROWS, d)
    return {
        "wu": w_in[:, :c0].astype(BF16),
        "wqt": w_in[:, c0:c1].T.astype(BF16),
        "wk": jnp.concatenate([w_in[:, c1:c2], w_in[:, c2:c2 + KV_DIM], w_in[:, c3:c3 + KV_DIM]], axis=1).astype(BF16),
        "wvst": values_t(w_in[:, c2 + KV_DIM:c3]).astype(BF16),
        "wvwt": values_t(w_in[:, c3 + KV_DIM:c4]).astype(BF16),
        "wgt": wg.astype(BF16),
        "wgm": w_in[:, c5:].astype(BF16),
    }


def _s5_weights(bb_re, bb_im, c_re, c_im, d_skip, w_glu, b_glu, w_o):
    per_slab = SSM_LANE_SLAB // SSM_GROUP
    eye = jnp.eye(per_slab, dtype=F32)

    def b_blocks(bb_t):
        bb_t = bb_t.reshape(SSM_GROUP, SSM_SLABS, per_slab, SSM_STATE)
        return jnp.einsum('cjgn,gh->jgchn', bb_t, eye).reshape(SSM_SLABS, SSM_LANE_SLAB, SSM_STATE_SLAB).astype(BF16)

    def c_blocks(c):
        c = c.astype(F32).reshape(SSM_SLABS, per_slab, SSM_GROUP, SSM_STATE)
        return jnp.einsum('jgcn,gh->jgnhc', c, eye).reshape(SSM_SLABS, SSM_STATE_SLAB, SSM_LANE_SLAB).astype(BF16)

    return {
        "wbre": b_blocks(bb_re), "wbim": b_blocks(bb_im), "cre": c_blocks(c_re), "cim": c_blocks(c_im),
        "d": d_skip.astype(F32).reshape(1, SSM_WIDTH), "wglu": w_glu.astype(BF16),
        "bglu": b_glu.astype(F32).reshape(1, SSM_WIDTH), "wo": w_o.astype(BF16),
    }


def _pick(total, want):
    return want if total % want == 0 else total


def kernel(x, mem, ln_emb_g, ln_emb_b, w_in, ssm_a_re, ssm_a_im, ssm_b_re, ssm_b_im, ssm_c_re, ssm_c_im, ssm_d,
           ssm_log_dt, ssm_w_glu, ssm_b_glu, ssm_w_out, cmp_pos, cmp_w1, cmp_b1, cmp_w2, nsa_w_out, w_out,
           ln1_g, ln1_b, xattn_w_q, xattn_w_kv, xattn_w_o, ln2_g, ln2_b, ffn_w_in, ffn_w_out, ln3_g, ln3_b):
    bsz, L, d = x.shape
    assert w_in.shape[0] == 1, "one layer: the trunk-entry LayerNorm is fused into its input projection"
    l = 0
    att_tile = _pick(L, ATT_TILE)
    row_tile = _pick(L, ROW_TILE)
    wi = _inproj_weights(w_in[l])
    h, u, qt, kvc, ks, kw, vst, vwt, gt = _ln_inproj(x, ln_emb_g, ln_emb_b, wi, row_tile)

    lb_re, lb_im, bb_re, bb_im = _zoh_prep(ssm_a_re[l], ssm_a_im[l], ssm_log_dt[l], ssm_b_re[l], ssm_b_im[l])
    ws = _s5_weights(bb_re, bb_im, ssm_c_re[l], ssm_c_im[l], ssm_d[l], ssm_w_glu[l], ssm_b_glu[l], ssm_w_out[l])
    y_ssm = _s5(u, lb_re, lb_im, ws, _pick(L, S5_STEPS))

    kc, vct = _compress(kvc, cmp_pos[l].astype(F32).reshape(2, CMP_BLOCK // CMP_QUAD, CMP_QUAD * HEAD_DIM),
                        cmp_w1[l].astype(BF16),
                        cmp_b1[l].astype(F32).reshape(2, 1, CMP_HIDDEN),
                        cmp_w2[l, 0].astype(BF16), cmp_w2[l, 1].T.astype(BF16))
    o_t = _nsa_attn(qt, kc, vct, ks, vst, kw, vwt, gt, att_tile)

    h = _mix_ln1(h, y_ssm, o_t, wi["wgm"], nsa_w_out[l].astype(BF16), w_out[l].astype(BF16),
                 ln1_g[l], ln1_b[l], row_tile)
    mk, mv = _mem_kv(mem, xattn_w_kv[l].astype(BF16))
    h = _xattn_ln2(h, mk, mv, xattn_w_q[l].astype(BF16), xattn_w_o[l].astype(BF16), ln2_g[l], ln2_b[l],
                   _pick(L, XATTN_ROW_TILE))
    h = _ffn_ln3(h.reshape(bsz * L, d), ffn_w_in[l].astype(BF16), ffn_w_out[l].astype(BF16),
                 ln3_g[l], ln3_b[l], row_tile)
    return h.reshape(bsz, L, d)
```
